```python
import math
import jax, jax.numpy as jnp
from jax import lax
import numpy as np

D_MODEL = 1024
BATCH = 4
SEQ = 4096
DEPTH = 2
DEC_BATCH = 128
DEC_SEQ = 4
PAST_LEN = 2048
PAGE_SIZE = 128

N_EVEN = (DEPTH + 1) // 2
N_ODD = DEPTH // 2
H_A = 4
DH_A = 64
A_QK = H_A * 2 * DH_A
A_V = H_A * 2 * DH_A
Q_BLK = 128
S5_CH = 512
S5_GROUP = 16
S5_G = S5_CH // S5_GROUP
S5_P = 64
H_C = 8
DH_C = 64
C_W = H_C * DH_C
C_BRANCHES = ((128, 1), (512, 4), (2048, 16))
C_MAX_WIN = 2048
C_BLK = 128
H_D = 8
N_D = 64
D_W = H_D * N_D
D_LORA_W = 64
D_LORA_A = 64
D_LORA_G = 128
D_COLS = 3 * D_W + D_LORA_W + D_LORA_A + D_LORA_G
GN_EPS = 64e-5
EVEN_IN = 2 * A_QK + A_V + S5_CH
ODD_IN = 3 * C_W + D_COLS
MIX_W = 1024
D_FF = 2816
ALPHA = (2.0 * DEPTH) ** 0.25
BETA = (8.0 * DEPTH) ** -0.25
LN_EPS = 1e-5
NEG = -1e30

kernel_name = 'hybrid_diffattn_s5_dilated_rwkv7_step'


def _layer_norm(x, g, b):
    xf = x.astype(jnp.float32)
    mu = jnp.mean(xf, -1, keepdims=True)
    var = jnp.mean(jnp.square(xf - mu), -1, keepdims=True)
    return ((xf - mu) * lax.rsqrt(var + LN_EPS) * g + b).astype(x.dtype)


def _rms_norm(x, g):
    xf = x.astype(jnp.float32)
    return (xf * lax.rsqrt(jnp.mean(jnp.square(xf), -1, keepdims=True) + LN_EPS) * g).astype(x.dtype)


def _modulate(x, m):
    return x * (1.0 + m[:, 1]) + m[:, 0]


def _swiglu(h, wg, wu, wd):
    return (jax.nn.silu(h @ wg) * (h @ wu)) @ wd


def _diff_core(q, k, v, lam, qpos, kpos):
    s = jnp.einsum('bqhmd,bkhmd->bmhqk', q, k).astype(jnp.float32) * (DH_A ** -0.5)
    s = jnp.where(kpos[None, :] <= qpos[:, None], s, NEG)
    p = jax.nn.softmax(s, axis=-1)
    w = p[:, 0] - lam * p[:, 1]
    return jnp.einsum('bhqk,bkhe->bqhe', w.astype(v.dtype), v)


def _diff_attn_prompt(q, k, v, lam):
    bq, t = q.shape[0], q.shape[1]
    kpos = jnp.arange(t)

    def block(i):
        start = i * Q_BLK
        qb = lax.dynamic_slice_in_dim(q, start, Q_BLK, axis=1)
        return _diff_core(qb, k, v, lam, start + jnp.arange(Q_BLK), kpos)

    out = lax.map(block, jnp.arange(t // Q_BLK))
    return jnp.moveaxis(out, 0, 1).reshape(bq, t, H_A, 2 * DH_A)


def _complex_affine_combine(e1, e2):
    ar1, ai1, br1, bi1 = e1
    ar2, ai2, br2, bi2 = e2
    return (ar2 * ar1 - ai2 * ai1, ar2 * ai1 + ai2 * ar1,
            ar2 * br1 - ai2 * bi1 + br2, ar2 * bi1 + ai2 * br1 + bi2)


def _s5(u, h0_re, h0_im, a_re, a_im, log_dt, b_re, b_im, c_re, c_im, d, glu_w, glu_b):
    f32 = jnp.float32
    bu, t, _ = u.shape
    a_re, a_im = a_re.astype(f32), a_im.astype(f32)
    dt = jnp.exp(log_dt.astype(f32))[:, None]
    mag = jnp.exp(a_re * dt)
    lam_re, lam_im = mag * jnp.cos(a_im * dt), mag * jnp.sin(a_im * dt)
    den = a_re * a_re + a_im * a_im
    nr = lam_re - 1.0
    f_re = (nr * a_re + lam_im * a_im) / den
    f_im = (lam_im * a_re - nr * a_im) / den
    b_re, b_im = b_re.astype(f32), b_im.astype(f32)
    bb_re = f_re[..., None] * b_re - f_im[..., None] * b_im
    bb_im = f_re[..., None] * b_im + f_im[..., None] * b_re
    ug = u.astype(f32).reshape(bu, t, S5_G, S5_GROUP)
    x_re = jnp.einsum('btgc,gpc->btgp', ug, bb_re)
    x_im = jnp.einsum('btgc,gpc->btgp', ug, bb_im)
    h0_re, h0_im = h0_re.astype(f32), h0_im.astype(f32)
    x_re = x_re.at[:, 0].add(lam_re * h0_re - lam_im * h0_im)
    x_im = x_im.at[:, 0].add(lam_re * h0_im + lam_im * h0_re)
    ar = jnp.broadcast_to(lam_re, x_re.shape)
    ai = jnp.broadcast_to(lam_im, x_im.shape)
    _, _, h_re, h_im = lax.associative_scan(_complex_affine_combine, (ar, ai, x_re, x_im), axis=1)
    y = (jnp.einsum('btgp,gcp->btgc', h_re, c_re.astype(f32))
         - jnp.einsum('btgp,gcp->btgc', h_im, c_im.astype(f32)))
    y = y.reshape(bu, t, S5_CH) + d * u.astype(f32)
    z = jax.nn.gelu(y)
    out = z * jax.nn.sigmoid(z @ glu_w.astype(f32) + glu_b)
    return out, h_re[:, -1], h_im[:, -1]


def _dilated_branch_prompt(q, k, v, dil):
    bq, t, h, e = q.shape
    span = dil * C_BLK
    t_pad = -(-t // span) * span
    n_sub = t_pad // dil
    nb = n_sub // C_BLK

    def blocks(a):
        a = jnp.pad(a, ((0, 0), (0, t_pad - t), (0, 0), (0, 0)))
        a = a.reshape(bq, n_sub, dil, h, e).transpose(0, 2, 1, 3, 4)
        return a.reshape(bq, dil, nb, C_BLK, h, e)

    def with_prev(a):
        prev = jnp.pad(a, ((0, 0), (0, 0), (1, 0), (0, 0), (0, 0), (0, 0)))[:, :, :-1]
        return jnp.concatenate([prev, a], axis=3)

    qb = blocks(q)
    kk, vv = with_prev(blocks(k)), with_prev(blocks(v))
    s = jnp.einsum('brnqhe,brnkhe->brnhqk', qb, kk).astype(jnp.float32) * (DH_C ** -0.5)
    qi = jnp.arange(C_BLK)[:, None] + C_BLK
    ki = jnp.arange(2 * C_BLK)[None, :]
    dist = qi - ki
    band = (dist >= 0) & (dist <= C_BLK)
    exists = (jnp.arange(nb)[:, None, None] > 0) | (ki[None] >= C_BLK)
    mask = band[None] & exists
    s = jnp.where(mask[None, None, :, None], s, NEG)
    m = jnp.max(s, -1, keepdims=True)
    pr = jnp.exp(s - m)
    den = jnp.sum(pr, -1, keepdims=True)
    o = jnp.einsum('brnhqk,brnkhe->brnqhe', (pr / den).astype(v.dtype), vv)
    lse = (m + jnp.log(den))[..., 0]
    o = o.reshape(bq, dil, n_sub, h, e).transpose(0, 2, 1, 3, 4).reshape(bq, t_pad, h, e)[:, :t]
    lse = lse.transpose(0, 1, 2, 4, 3).reshape(bq, dil, n_sub, h)
    lse = lse.transpose(0, 2, 1, 3).reshape(bq, t_pad, h)[:, :t]
    return o, lse


def _dilated_branch_sample(q, k_all, v_all, buf_len, dil):
    s_len = q.shape[1]
    j = jnp.arange(C_BLK + 1)
    idx = buf_len + jnp.arange(s_len)[:, None] - dil * j[None, :]
    valid = idx >= 0
    idx = jnp.maximum(idx, 0)
    kg, vg = k_all[:, idx], v_all[:, idx]
    s = jnp.einsum('bshe,bsjhe->bhsj', q, kg).astype(jnp.float32) * (DH_C ** -0.5)
    s = jnp.where(valid[None, None], s, NEG)
    m = jnp.max(s, -1, keepdims=True)
    pr = jnp.exp(s - m)
    den = jnp.sum(pr, -1, keepdims=True)
    o = jnp.einsum('bhsj,bsjhe->bshe', (pr / den).astype(v_all.dtype), vg)
    lse = jnp.transpose((m + jnp.log(den))[..., 0], (0, 2, 1))
    return o, lse


def _combine_by_denominator(outs, lses):
    wts = jax.nn.softmax(jnp.stack(lses, 0), axis=0)
    return jnp.einsum('nbth,nbthe->bthe', wts, jnp.stack(outs, 0).astype(jnp.float32))


def _rwkv7(pd, shift0, s0, mu, w0, w2, a0, a2, g2, k_k, k_a, r_k, gn_w, gn_b):
    f32 = jnp.float32
    bh, t, _ = pd.shape
    prev = jnp.concatenate([shift0[:, None].astype(pd.dtype), pd[:, :-1]], axis=1)
    xm = (pd + (prev - pd) * mu).astype(f32)
    o1, o2, o3 = D_W, 2 * D_W, 3 * D_W
    o4, o5 = o3 + D_LORA_W, o3 + D_LORA_W + D_LORA_A
    r, k, v = xm[..., :o1], xm[..., o1:o2], xm[..., o2:o3]
    wl, al, gl = xm[..., o3:o4], xm[..., o4:o5], xm[..., o5:]
    w_log = -jax.nn.softplus(-(w0 + jnp.tanh(wl) @ w2)) - 0.5
    decay = jnp.exp(-jnp.exp(w_log.astype(f32)))
    a = jax.nn.sigmoid(a0 + al @ a2).astype(f32)
    g = (jax.nn.sigmoid(gl) @ g2).astype(f32)

    def heads(z):
        return z.reshape(bh, t, H_D, N_D)

    rh, kh, vh, dh, ah = heads(r), heads(k), heads(v), heads(decay), heads(a)
    kk = kh * k_k.reshape(H_D, N_D)
    kk = kk / jnp.maximum(jnp.sqrt(jnp.sum(kk * kk, -1, keepdims=True)), 1e-12)
    kh = kh * (1.0 + (ah - 1.0) * k_a.reshape(H_D, N_D))

    def step(s, inp):
        r_t, w_t, k_t, v_t, kk_t, a_t = inp
        sk = jnp.einsum('bhvk,bhk->bhv', s, kk_t)
        s = (s * w_t[:, :, None, :] - sk[..., None] * (kk_t * a_t)[:, :, None, :]
             + v_t[..., None] * k_t[:, :, None, :])
        return s, jnp.einsum('bhvk,bhk->bhv', s, r_t)

    seq = tuple(jnp.moveaxis(z.astype(f32), 1, 0) for z in (rh, dh, kh, vh, kk, ah))
    s_final, ys = lax.scan(step, s0.astype(f32), seq)
    y = jnp.moveaxis(ys, 0, 1)
    my = jnp.mean(y, -1, keepdims=True)
    vy = jnp.mean(jnp.square(y - my), -1, keepdims=True)
    y = ((y - my) * lax.rsqrt(vy + GN_EPS)).reshape(bh, t, D_W) * gn_w + gn_b
    bonus = (jnp.sum(rh * kh * r_k, -1, keepdims=True) * vh).reshape(bh, t, D_W)
    return (y + bonus) * g, pd[:, -1], s_final


def setup_inputs(seed: int = 0) -> dict:
    key = jax.random.key(seed)
    ks = iter(jax.random.split(key, 64))

    def nrm(shape, scale=1.0):
        return jax.random.normal(next(ks), shape, jnp.float32) * scale

    def unif(shape, lo, hi):
        return jax.random.uniform(next(ks), shape, jnp.float32, lo, hi)

    n_pages = PAST_LEN // PAGE_SIZE
    used = DEC_BATCH * n_pages
    n_pool = used + (used + 3) // 4
    win_buf = min(C_MAX_WIN, PAST_LEN)
    page_table = jax.random.permutation(next(ks), n_pool)[:used].reshape(DEC_BATCH, n_pages).astype(jnp.int32)

    even_w_in = nrm((N_EVEN, D_MODEL, EVEN_IN), D_MODEL ** -0.5)
    even_w_in = even_w_in.at[..., 2 * A_QK:2 * A_QK + A_V].multiply(BETA)
    odd_w_in = nrm((N_ODD, D_MODEL, ODD_IN), D_MODEL ** -0.5)
    odd_w_in = odd_w_in.at[..., 2 * C_W:3 * C_W].multiply(BETA)
    odd_w_in = odd_w_in.at[..., 3 * C_W + 2 * D_W:3 * C_W + 3 * D_W].multiply(BETA)
    n_idx = jnp.arange(S5_P, dtype=jnp.float32)

    return {
        'x_prompt': nrm((BATCH, SEQ, D_MODEL)),
        'x_sample': nrm((DEC_BATCH, DEC_SEQ, D_MODEL)),
        'cache_a_k': nrm((N_EVEN, n_pool, PAGE_SIZE, H_A, 2, DH_A)),
        'cache_a_v': nrm((N_EVEN, n_pool, PAGE_SIZE, H_A, 2 * DH_A), BETA),
        'state_s5_re': nrm((N_EVEN, DEC_BATCH, S5_G, S5_P), 0.1),
        'state_s5_im': nrm((N_EVEN, DEC_BATCH, S5_G, S5_P), 0.1),
        'cache_c_k': nrm((N_ODD, DEC_BATCH, win_buf, H_C, DH_C)),
        'cache_c_v': nrm((N_ODD, DEC_BATCH, win_buf, H_C, DH_C), BETA),
        'state_d_wkv': nrm((N_ODD, DEC_BATCH, H_D, N_D, N_D), 0.1),
        'state_d_shift': nrm((N_ODD, DEC_BATCH, D_COLS)),
        'page_table': page_table,
        'c_prompt': nrm((BATCH, D_MODEL)),
        'c_sample': nrm((DEC_BATCH, D_MODEL)),
        'ada_w': nrm((DEPTH, D_MODEL, 9 * D_MODEL), 0.2 * D_MODEL ** -0.5),
        'ada_b': nrm((DEPTH, 9 * D_MODEL), 0.02),
        'ln_g': 1.0 + nrm((DEPTH, 3, D_MODEL), 0.02),
        'ln_b': nrm((DEPTH, 3, D_MODEL), 0.02),
        'ffn_w_gate': nrm((DEPTH, 2, D_MODEL, D_FF), D_MODEL ** -0.5),
        'ffn_w_up': nrm((DEPTH, 2, D_MODEL, D_FF), D_MODEL ** -0.5),
        'ffn_w_down': nrm((DEPTH, 2, D_FF, D_MODEL), BETA * D_FF ** -0.5),
        'even_w_in': even_w_in,
        'even_w_out': nrm((N_EVEN, MIX_W, D_MODEL), BETA * MIX_W ** -0.5),
        'diff_lambda': nrm((N_EVEN, 4, DH_A), 0.1),
        'diff_subln': 1.0 + nrm((N_EVEN, 2 * DH_A), 0.02),
        's5_a_re': -0.5 + nrm((N_EVEN, S5_G, S5_P), 0.02),
        's5_a_im': math.pi * n_idx + nrm((N_EVEN, S5_G, S5_P), 0.02),
        's5_log_dt': unif((N_EVEN, S5_G), math.log(1e-3), math.log(1e-1)),
        's5_b_re': nrm((N_EVEN, S5_G, S5_P, S5_GROUP), (2 * S5_GROUP) ** -0.5),
        's5_b_im': nrm((N_EVEN, S5_G, S5_P, S5_GROUP), (2 * S5_GROUP) ** -0.5),
        's5_c_re': nrm((N_EVEN, S5_G, S5_GROUP, S5_P), (2 * S5_P) ** -0.5),
        's5_c_im': nrm((N_EVEN, S5_G, S5_GROUP, S5_P), (2 * S5_P) ** -0.5),
        's5_d': nrm((N_EVEN, S5_CH)),
        's5_glu_w': nrm((N_EVEN, S5_CH, S5_CH), S5_CH ** -0.5),
        's5_glu_b': nrm((N_EVEN, S5_CH), 0.02),
        'odd_w_in': odd_w_in,
        'odd_w_out': nrm((N_ODD, MIX_W, D_MODEL), BETA * MIX_W ** -0.5),
        'rwkv_mu': unif((N_ODD, D_COLS), 0.0, 1.0),
        'rwkv_w0': unif((N_ODD, D_W), -6.0, -1.0),
        'rwkv_w2': nrm((N_ODD, D_LORA_W, D_W), 0.5 * D_LORA_W ** -0.5),
        'rwkv_a0': nrm((N_ODD, D_W), 0.1),
        'rwkv_a2': nrm((N_ODD, D_LORA_A, D_W), 0.5 * D_LORA_A ** -0.5),
        'rwkv_g2': nrm((N_ODD, D_LORA_G, D_W), D_LORA_G ** -0.5),
        'rwkv_k_k': 0.85 + nrm((N_ODD, D_W), 0.02),
        'rwkv_k_a': 1.0 + nrm((N_ODD, D_W), 0.02),
        'rwkv_r_k': nrm((N_ODD, H_D, N_D), 0.1),
        'rwkv_gn_w': 1.0 + nrm((N_ODD, D_W), 0.02),
        'rwkv_gn_b': nrm((N_ODD, D_W), 0.02),
    }


def reference(x_prompt, x_sample, cache_a_k, cache_a_v, state_s5_re, state_s5_im, cache_c_k, cache_c_v,
              state_d_wkv, state_d_shift, page_table, c_prompt, c_sample, ada_w, ada_b, ln_g, ln_b,
              ffn_w_gate, ffn_w_up, ffn_w_down, even_w_in, even_w_out, diff_lambda, diff_subln,
              s5_a_re, s5_a_im, s5_log_dt, s5_b_re, s5_b_im, s5_c_re, s5_c_im, s5_d, s5_glu_w, s5_glu_b,
              odd_w_in, odd_w_out, rwkv_mu, rwkv_w0, rwkv_w2, rwkv_a0, rwkv_a2, rwkv_g2, rwkv_k_k,
              rwkv_k_a, rwkv_r_k, rwkv_gn_w, rwkv_gn_b):
    past_len = page_table.shape[1] * cache_a_k.shape[2]
    win_buf = cache_c_k.shape[2]

    def even_mix(h, l, sample):
        e = l // 2
        bh, t, _ = h.shape
        p = h @ even_w_in[e]
        q = p[..., :A_QK].reshape(bh, t, H_A, 2, DH_A)
        k = p[..., A_QK:2 * A_QK].reshape(bh, t, H_A, 2, DH_A)
        v = p[..., 2 * A_QK:2 * A_QK + A_V].reshape(bh, t, H_A, 2 * DH_A)
        u = p[..., 2 * A_QK + A_V:]
        lam_init = 0.8 - 0.6 * math.exp(-0.3 * l)
        lp = diff_lambda[e].astype(jnp.float32)
        lam = jnp.exp(jnp.sum(lp[0] * lp[1])) - jnp.exp(jnp.sum(lp[2] * lp[3])) + lam_init
        if sample:
            def gather(pool):
                return pool[page_table].reshape((bh, past_len) + pool.shape[2:])
            k_all = jnp.concatenate([gather(cache_a_k[e]), k], axis=1)
            v_all = jnp.concatenate([gather(cache_a_v[e]), v], axis=1)
            att = _diff_core(q, k_all, v_all, lam, past_len + jnp.arange(t), jnp.arange(past_len + t))
            h0r, h0i = state_s5_re[e], state_s5_im[e]
        else:
            att = _diff_attn_prompt(q, k, v, lam)
            h0r = jnp.zeros((bh, S5_G, S5_P), jnp.float32)
            h0i = jnp.zeros((bh, S5_G, S5_P), jnp.float32)
        att = _rms_norm(att, diff_subln[e]) * (1.0 - lam_init)
        y5, hr, hi = _s5(u, h0r, h0i, s5_a_re[e], s5_a_im[e], s5_log_dt[e], s5_b_re[e], s5_b_im[e],
                         s5_c_re[e], s5_c_im[e], s5_d[e], s5_glu_w[e], s5_glu_b[e])
        mix = jnp.concatenate([att.reshape(bh, t, A_V).astype(h.dtype), y5.astype(h.dtype)], -1) @ even_w_out[e]
        return mix, (k, v, hr, hi)

    def odd_mix(h, l, sample):
        o = l // 2
        bh, t, _ = h.shape
        p = h @ odd_w_in[o]
        q = p[..., :C_W].reshape(bh, t, H_C, DH_C)
        k = p[..., C_W:2 * C_W].reshape(bh, t, H_C, DH_C)
        v = p[..., 2 * C_W:3 * C_W].reshape(bh, t, H_C, DH_C)
        pd = p[..., 3 * C_W:]
        outs, lses = [], []
        if sample:
            k_all = jnp.concatenate([cache_c_k[o], k], axis=1)
            v_all = jnp.concatenate([cache_c_v[o], v], axis=1)
            for _, dil in C_BRANCHES:
                ob, lb = _dilated_branch_sample(q, k_all, v_all, win_buf, dil)
                outs.append(ob)
                lses.append(lb)
            shift0, s0 = state_d_shift[o], state_d_wkv[o]
            k_keep, v_keep = k, v
        else:
            for _, dil in C_BRANCHES:
                ob, lb = _dilated_branch_prompt(q, k, v, dil)
                outs.append(ob)
                lses.append(lb)
            shift0 = jnp.zeros((bh, D_COLS), pd.dtype)
            s0 = jnp.zeros((bh, H_D, N_D, N_D), jnp.float32)
            keep = min(C_MAX_WIN, t)
            k_keep, v_keep = k[:, t - keep:], v[:, t - keep:]
        att = _combine_by_denominator(outs, lses)
        yd, shift_new, s_new = _rwkv7(pd, shift0, s0, rwkv_mu[o], rwkv_w0[o], rwkv_w2[o], rwkv_a0[o],
                                      rwkv_a2[o], rwkv_g2[o], rwkv_k_k[o], rwkv_k_a[o], rwkv_r_k[o],
                                      rwkv_gn_w[o], rwkv_gn_b[o])
        mix = jnp.concatenate([att.reshape(bh, t, C_W).astype(h.dtype), yd.astype(h.dtype)], -1) @ odd_w_out[o]
        return mix, (k_keep, v_keep, s_new, shift_new)

    def trunk(x, c, sample):
        bx = x.shape[0]
        ak, av, s5r, s5i, ck, cv, dw, ds = [], [], [], [], [], [], [], []
        for l in range(DEPTH):
            mod = (jax.nn.silu(c) @ ada_w[l] + ada_b[l]).reshape(bx, 3, 3, 1, D_MODEL)
            f = _swiglu(_modulate(x, mod[:, 0]), ffn_w_gate[l, 0], ffn_w_up[l, 0], ffn_w_down[l, 0])
            x = _layer_norm(ALPHA * x + 0.5 * (1.0 + mod[:, 0, 2]) * f, ln_g[l, 0], ln_b[l, 0])
            h = _modulate(x, mod[:, 1])
            if l % 2 == 0:
                m, (k_r, v_r, hr, hi) = even_mix(h, l, sample)
                ak.append(k_r)
                av.append(v_r)
                s5r.append(hr)
                s5i.append(hi)
            else:
                m, (k_r, v_r, s_new, sh_new) = odd_mix(h, l, sample)
                ck.append(k_r)
                cv.append(v_r)
                dw.append(s_new)
                ds.append(sh_new)
            x = _layer_norm(ALPHA * x + (1.0 + mod[:, 1, 2]) * m, ln_g[l, 1], ln_b[l, 1])
            f = _swiglu(_modulate(x, mod[:, 2]), ffn_w_gate[l, 1], ffn_w_up[l, 1], ffn_w_down[l, 1])
            x = _layer_norm(ALPHA * x + 0.5 * (1.0 + mod[:, 2, 2]) * f, ln_g[l, 2], ln_b[l, 2])
        return x, [jnp.stack(z, 0) for z in (ak, av, s5r, s5i, ck, cv, dw, ds)]

    y_prompt, st_p = trunk(x_prompt, c_prompt, False)
    y_sample, st_s = trunk(x_sample, c_sample, True)
    a_k_p, a_v_p, s5_re_p, s5_im_p, c_k_p, c_v_p, d_wkv_p, d_shift_p = st_p
    a_k_s, a_v_s, s5_re_s, s5_im_s, c_k_s, c_v_s, d_wkv_s, d_shift_s = st_s
    return (y_prompt, y_sample, a_k_p, a_k_s, a_v_p, a_v_s, s5_re_p, s5_re_s, s5_im_p, s5_im_s,
            c_k_p, c_k_s, c_v_p, c_v_s, d_wkv_p, d_wkv_s, d_shift_p, d_shift_s)
```

```python
import functools
import math

import jax
import jax.numpy as jnp
from jax import lax
from jax.experimental import pallas as pl
from jax.experimental.pallas import tpu as pltpu

F32 = jnp.float32
BF16 = jnp.bfloat16

DEPTH = 2
H_A, DH_A = 4, 64
S5_GROUP, S5_G, S5_P = 16, 32, 64
S5_CH = S5_GROUP * S5_G
S5_N = S5_G * S5_P
H_C, DH_C = 8, 64
C_BLK = 128
C_DILATIONS = (1, 4, 16)
H_D, N_D = 8, 64
D_W = H_D * N_D
D_LORA_W, D_LORA_A, D_LORA_G = 64, 64, 128
GN_EPS = 64e-5
ALPHA = (2.0 * DEPTH) ** 0.25
LN_EPS = 1e-5
NEG = -1e30

LANES = 128
SUBLANES = 8
VMEM_LIMIT_BYTES = 56 * 1024 * 1024
TOKEN_TILE = 512
RWKV_CHUNK = 64


def _params(*sem):
    return pltpu.CompilerParams(dimension_semantics=sem, vmem_limit_bytes=VMEM_LIMIT_BYTES)


def _const_spec(shape):
    nd = len(shape)
    return pl.BlockSpec(shape, lambda *_: (0,) * nd, pipeline_mode=pl.Buffered(1))


def _dot(a, b):
    return jnp.dot(a, b, preferred_element_type=F32)


def _dot_nt(a, b):
    return lax.dot_general(a, b, (((1,), (1,)), ((), ())), preferred_element_type=F32)


def _dot_tn(a, b):
    return lax.dot_general(a, b, (((0,), (0,)), ((), ())), preferred_element_type=F32)


def _split3(x):
    hi = x.astype(BF16)
    r1 = x - hi.astype(F32)
    mid = r1.astype(BF16)
    lo = (r1 - mid.astype(F32)).astype(BF16)
    return hi, mid, lo


def _dot_exact_rhs(x, m01):
    hi, mid, lo = _split3(x)
    return _dot(hi, m01) + _dot(mid, m01) + _dot(lo, m01)


def _layer_norm(y, g, b):
    mu = jnp.mean(y, -1, keepdims=True)
    yc = y - mu
    var = jnp.mean(yc * yc, -1, keepdims=True)
    return yc * lax.rsqrt(var + LN_EPS) * g + b


def _sigmoid(x):
    return jax.nn.sigmoid(x)


def _mod_spec(mod, tiles_per_group):
    _, rm, d = mod.shape
    return pl.BlockSpec((1, rm, d), lambda i: (i // tiles_per_group, 0, 0))


def _ada_kernel(c_ref, w_ref, b_ref, o_ref):
    c = c_ref[...]
    s = (c * _sigmoid(c)).astype(BF16)
    o_ref[0] = _dot(s, w_ref[0].astype(BF16)) + b_ref[0]


def _ada(c_all, ada_w, ada_b):
    nl, d, w = ada_w.shape
    r = c_all.shape[0]
    tn = 1152 if w % 1152 == 0 else w
    return pl.pallas_call(
        _ada_kernel,
        grid=(nl, w // tn),
        in_specs=[pl.BlockSpec((r, d), lambda l, j: (0, 0)),
                  pl.BlockSpec((1, d, tn), lambda l, j: (l, 0, j)),
                  pl.BlockSpec((1, 1, tn), lambda l, j: (l, 0, j))],
        out_specs=pl.BlockSpec((1, r, tn), lambda l, j: (l, 0, j)),
        out_shape=jax.ShapeDtypeStruct((nl, r, w), F32),
        compiler_params=_params("parallel", "parallel"),
        name="ada_mod",
    )(c_all, ada_w, ada_b.reshape(nl, 1, w))


def _ffn_kernel(x_ref, sh_ref, sc_ref, gt_ref, wg_ref, wu_ref, wd_ref, g_ref, b_ref, o_ref):
    x = x_ref[...]
    h = (x * (1.0 + sc_ref[0]) + sh_ref[0]).astype(BF16)
    g = _dot(h, wg_ref[...])
    u = _dot(h, wu_ref[...])
    a = (g * _sigmoid(g) * u).astype(BF16)
    f = _dot(a, wd_ref[...])
    y = ALPHA * x + 0.5 * (1.0 + gt_ref[0]) * f
    o_ref[...] = _layer_norm(y, g_ref[...], b_ref[...])


def _ffn(x, mods, tpg, wg, wu, wd, ln_g, ln_b):
    n, d = x.shape
    f = wg.shape[1]
    tm = min(TOKEN_TILE, n)
    sh, sc, gt = mods
    return pl.pallas_call(
        _ffn_kernel,
        grid=(n // tm,),
        in_specs=[pl.BlockSpec((tm, d), lambda i: (i, 0)),
                  _mod_spec(sh, tpg), _mod_spec(sc, tpg), _mod_spec(gt, tpg),
                  _const_spec((d, f)), _const_spec((d, f)), _const_spec((f, d)),
                  _const_spec((1, d)), _const_spec((1, d))],
        out_specs=pl.BlockSpec((tm, d), lambda i: (i, 0)),
        out_shape=jax.ShapeDtypeStruct((n, d), F32),
        compiler_params=_params("parallel"),
        name="ffn_ln",
    )(x, sh, sc, gt, wg, wu, wd, ln_g.reshape(1, d), ln_b.reshape(1, d))


def _inproj_kernel(x_ref, sh_ref, sc_ref, w_ref, *o_refs, widths):
    h = (x_ref[...] * (1.0 + sc_ref[0]) + sh_ref[0]).astype(BF16)
    p = _dot(h, w_ref[...])
    off = 0
    for o_ref, wd in zip(o_refs, widths):
        o_ref[...] = p[:, off:off + wd]
        off += wd


def _inproj(x, mods, tpg, w, widths):
    n, d = x.shape
    tm = min(TOKEN_TILE, n)
    sh, sc, _ = mods
    return pl.pallas_call(
        functools.partial(_inproj_kernel, widths=widths),
        grid=(n // tm,),
        in_specs=[pl.BlockSpec((tm, d), lambda i: (i, 0)), _mod_spec(sh, tpg), _mod_spec(sc, tpg),
                  _const_spec(w.shape)],
        out_specs=[pl.BlockSpec((tm, wd), lambda i: (i, 0)) for wd in widths],
        out_shape=[jax.ShapeDtypeStruct((n, wd), F32) for wd in widths],
        compiler_params=_params("parallel"),
        name="in_proj",
    )(x, sh, sc, w)


def _outproj_kernel(x_ref, gt_ref, *refs, n_branch):
    att_refs = refs[:2 * n_branch] if n_branch > 1 else refs[:1]
    y_ref, w_ref, g_ref, b_ref, o_ref = refs[len(att_refs):]
    if n_branch > 1:
        outs = [r[...] for r in att_refs[:n_branch]]
        lses = [r[...] for r in att_refs[n_branch:]]
        m = functools.reduce(jnp.maximum, lses)
        ws = [jnp.exp(l - m) for l in lses]
        den = functools.reduce(lambda a, b: a + b, ws)
        att = functools.reduce(lambda a, b: a + b, [w * o for w, o in zip(ws, outs)]) / den
    else:
        att = att_refs[0][...]
    half = att.shape[1]
    mix = _dot(att.astype(BF16), w_ref[:half, :]) + _dot(y_ref[...].astype(BF16), w_ref[half:, :])
    y = ALPHA * x_ref[...] + (1.0 + gt_ref[0]) * mix
    o_ref[...] = _layer_norm(y, g_ref[...], b_ref[...])


def _outproj(x, gate, tpg, atts, y, w, ln_g, ln_b):
    n, d = x.shape
    tm = min(TOKEN_TILE, n)
    half = y.shape[1]
    n_branch = len(atts) // 2 if len(atts) > 1 else 1
    tok = lambda wd: pl.BlockSpec((tm, wd), lambda i: (i, 0))
    return pl.pallas_call(
        functools.partial(_outproj_kernel, n_branch=n_branch),
        grid=(n // tm,),
        in_specs=[tok(d), _mod_spec(gate, tpg)] + [tok(half)] * len(atts) + [tok(half), _const_spec(w.shape),
                                                                           _const_spec((1, d)), _const_spec((1, d))],
        out_specs=tok(d),
        out_shape=jax.ShapeDtypeStruct((n, d), F32),
        compiler_params=_params("parallel"),
        name="out_proj_ln",
    )(x, gate, *atts, y, w, ln_g.reshape(1, d), ln_b.reshape(1, d))


def _diff_lambda(dl_ref, lam_init):
    lp = dl_ref[...]
    a = jnp.sum(lp[0:1] * lp[1:2], axis=-1, keepdims=True)
    b = jnp.sum(lp[2:3] * lp[3:4], axis=-1, keepdims=True)
    return jnp.exp(a) - jnp.exp(b) + lam_init


def _head_rms(o, g, lam_init):
    return o * lax.rsqrt(jnp.mean(o * o, -1, keepdims=True) + LN_EPS) * g * (1.0 - lam_init)


def _diff_prompt_kernel(dl_ref, q_ref, k_ref, v_ref, g_ref, o_ref, *, tq, lam_init):
    i = pl.program_id(2)
    lam = _diff_lambda(dl_ref, lam_init)
    q = q_ref[...] * (DH_A ** -0.5)
    lane = lax.broadcasted_iota(jnp.int32, (1, 2 * DH_A), 1)
    lo = lane < DH_A
    q1 = jnp.where(lo, q, 0.0).astype(BF16)
    q2 = jnp.where(lo, 0.0, q).astype(BF16)
    qpos = i * tq + lax.broadcasted_iota(jnp.int32, (tq, 1), 0)

    def update(s, vb, m, l, acc):
        m_new = jnp.maximum(m, jnp.max(s, -1, keepdims=True))
        p = jnp.exp(s - m_new)
        corr = jnp.exp(m - m_new)
        return m_new, corr * l + jnp.sum(p, -1, keepdims=True), corr * acc + _dot(p.astype(BF16), vb)

    def body(j, carry):
        m1, l1, a1, m2, l2, a2 = carry
        start = pl.multiple_of(j * tq, tq)
        kb = k_ref[pl.ds(start, tq), :].astype(BF16)
        vb = v_ref[pl.ds(start, tq), :].astype(BF16)
        kpos = j * tq + lax.broadcasted_iota(jnp.int32, (1, tq), 1)
        ok = kpos <= qpos
        s1 = jnp.where(ok, _dot_nt(q1, kb), NEG)
        s2 = jnp.where(ok, _dot_nt(q2, kb), NEG)
        m1, l1, a1 = update(s1, vb, m1, l1, a1)
        m2, l2, a2 = update(s2, vb, m2, l2, a2)
        return m1, l1, a1, m2, l2, a2

    mz = jnp.full((tq, 1), NEG, F32)
    lz = jnp.zeros((tq, 1), F32)
    az = jnp.zeros((tq, 2 * DH_A), F32)
    _, l1, a1, _, l2, a2 = lax.fori_loop(0, i + 1, body, (mz, lz, az, mz, lz, az))
    o = a1 / l1 - lam * (a2 / l2)
    o_ref[...] = _head_rms(o, g_ref[...], lam_init)


def _diff_prompt(q, k, v, dl, subln, b, t, lam_init):
    n, w = q.shape
    hw = 2 * DH_A
    tq = min(256, t)
    nq = t // tq
    return pl.pallas_call(
        functools.partial(_diff_prompt_kernel, tq=tq, lam_init=lam_init),
        grid=(b, w // hw, nq),
        in_specs=[pl.BlockSpec(dl.shape, lambda bi, h, i: (0, 0)),
                  pl.BlockSpec((tq, hw), lambda bi, h, i: (bi * nq + i, h)),
                  pl.BlockSpec((t, hw), lambda bi, h, i: (bi, h)),
                  pl.BlockSpec((t, hw), lambda bi, h, i: (bi, h)),
                  pl.BlockSpec((1, hw), lambda bi, h, i: (0, 0))],
        out_specs=pl.BlockSpec((tq, hw), lambda bi, h, i: (bi * nq + i, h)),
        out_shape=jax.ShapeDtypeStruct((n, w), F32),
        compiler_params=_params("parallel", "parallel", "arbitrary"),
        name="diff_attn_prompt",
    )(dl, q, k, v, subln.reshape(1, hw))


def _diff_sample_kernel(pt_ref, dl_ref, q_ref, kp_ref, vp_ref, kn_ref, vn_ref, g_ref, o_ref,
                        qbd, m_sc, l_sc, acc, *, lam_init):
    del pt_ref
    p = pl.program_id(1)
    rows = qbd.shape[0]
    width = qbd.shape[1]
    row_g = lax.broadcasted_iota(jnp.int32, (rows, 1), 0) // SUBLANES
    row_t = lax.broadcasted_iota(jnp.int32, (rows, 1), 0) % SUBLANES

    @pl.when(p == 0)
    def _():
        q8 = q_ref[0] * (DH_A ** -0.5)
        lane_g = lax.broadcasted_iota(jnp.int32, (1, width), 1) // DH_A
        qrep = jnp.concatenate([q8] * (rows // SUBLANES), axis=0)
        qbd[...] = jnp.where(lane_g == row_g, qrep, 0.0).astype(BF16)
        m_sc[...] = jnp.full(m_sc.shape, NEG, F32)
        l_sc[...] = jnp.zeros(l_sc.shape, F32)
        acc[...] = jnp.zeros(acc.shape, F32)

    def update(s, vb):
        m_old = m_sc[...]
        m_new = jnp.maximum(m_old, jnp.max(s, -1, keepdims=True))
        pr = jnp.exp(s - m_new)
        corr = jnp.exp(m_old - m_new)
        l_sc[...] = corr * l_sc[...] + jnp.sum(pr, -1, keepdims=True)
        acc[...] = corr * acc[...] + _dot(pr.astype(BF16), vb)
        m_sc[...] = m_new

    update(_dot_nt(qbd[...], kp_ref[0].astype(BF16)), vp_ref[0].astype(BF16))

    @pl.when(p == pl.num_programs(1) - 1)
    def _():
        col = lax.broadcasted_iota(jnp.int32, (1, SUBLANES), 1)
        s = jnp.where(col <= row_t, _dot_nt(qbd[...], kn_ref[0].astype(BF16)), NEG)
        update(s, vn_ref[0].astype(BF16))
        lam = _diff_lambda(dl_ref, lam_init)
        a = acc[...] / l_sc[...]
        lane_h = lax.broadcasted_iota(jnp.int32, (1, width), 1) // (2 * DH_A)
        d = jnp.zeros((SUBLANES, width), F32)
        for g in range(rows // SUBLANES):
            coef = 1.0 if g % 2 == 0 else -lam
            blk = a[g * SUBLANES:(g + 1) * SUBLANES, :]
            d = d + jnp.where(lane_h == g // 2, coef * blk, 0.0)
        hw = 2 * DH_A
        outs = [_head_rms(d[:, h * hw:(h + 1) * hw], g_ref[...], lam_init) for h in range(width // hw)]
        o_ref[0] = jnp.concatenate(outs, axis=-1)


def _diff_sample(q8, kn8, vn8, pool_k, pool_v, page_table, dl, subln, lam_init):
    bs, _, w = q8.shape
    n_pages = page_table.shape[1]
    page = pool_k.shape[1]
    rows = (w // DH_A) * SUBLANES
    tok = pl.BlockSpec((1, SUBLANES, w), lambda b, p, pt: (b, 0, 0))
    pg = pl.BlockSpec((1, page, w), lambda b, p, pt: (pt[b, p], 0, 0))
    return pl.pallas_call(
        functools.partial(_diff_sample_kernel, lam_init=lam_init),
        grid_spec=pltpu.PrefetchScalarGridSpec(
            num_scalar_prefetch=1,
            grid=(bs, n_pages),
            in_specs=[pl.BlockSpec(dl.shape, lambda b, p, pt: (0, 0)), tok, pg, pg, tok, tok,
                      pl.BlockSpec((1, 2 * DH_A), lambda b, p, pt: (0, 0))],
            out_specs=tok,
            scratch_shapes=[pltpu.VMEM((rows, w), BF16), pltpu.VMEM((rows, 1), F32),
                            pltpu.VMEM((rows, 1), F32), pltpu.VMEM((rows, w), F32)]),
        out_shape=jax.ShapeDtypeStruct((bs, SUBLANES, w), F32),
        compiler_params=_params("parallel", "arbitrary"),
        name="diff_attn_sample",
    )(page_table, dl, q8, pool_k, pool_v, kn8, vn8, subln.reshape(1, 2 * DH_A))


def _s5_prep_kernel(are_ref, aim_ref, ldt_ref, bre_ref, bim_ref, lr_ref, li_ref, bbr_ref, bbi_ref):
    a_re, a_im = are_ref[...], aim_ref[...]
    dt = jnp.exp(ldt_ref[...])
    mag = jnp.exp(a_re * dt)
    lam_re, lam_im = mag * jnp.cos(a_im * dt), mag * jnp.sin(a_im * dt)
    den = a_re * a_re + a_im * a_im
    nr = lam_re - 1.0
    f_re = (nr * a_re + lam_im * a_im) / den
    f_im = (lam_im * a_re - nr * a_im) / den
    lr_ref[...] = lam_re
    li_ref[...] = lam_im
    for g in range(a_re.shape[0]):
        fr, fi = f_re[g:g + 1, :], f_im[g:g + 1, :]
        br, bi = bre_ref[g], bim_ref[g]
        bbr_ref[g] = fr * br - fi * bi
        bbi_ref[g] = fr * bi + fi * br


def _s5_prep(a_re, a_im, log_dt, b_re, b_im):
    g, p = a_re.shape
    c = b_re.shape[-1]
    bt = lambda b: jnp.transpose(b, (0, 2, 1))
    sd = jax.ShapeDtypeStruct
    return pl.pallas_call(
        _s5_prep_kernel,
        out_shape=[sd((g, p), F32), sd((g, p), F32), sd((g, c, p), F32), sd((g, c, p), F32)],
        name="s5_prep",
    )(a_re, a_im, log_dt.reshape(g, 1), bt(b_re), bt(b_im))


def _s5_tail(u, hr, hi, ccr, cci, d, gw, gb):
    y = _dot(hr.astype(BF16), ccr) - _dot(hi.astype(BF16), cci) + d * u
    z = jax.nn.gelu(y)
    return z * _sigmoid(_dot(z.astype(BF16), gw) + gb)


def _s5_prompt_kernel(u_ref, bbr_ref, bbi_ref, lr_ref, li_ref, h0r_ref, h0i_ref, ccr_ref, cci_ref,
                      d_ref, gw_ref, gb_ref, y_ref, hr_out, hi_out, xr, xi, cr, ci):
    i = pl.program_id(1)
    tc = u_ref.shape[0]

    @pl.when(i == 0)
    def _():
        cr[...] = h0r_ref[0]
        ci[...] = h0i_ref[0]

    u = u_ref[...]
    ub = u.astype(BF16)
    xr[...] = _dot(ub, bbr_ref[...])
    xi[...] = _dot(ub, bbi_ref[...])
    lr, li = lr_ref[...], li_ref[...]

    def step(t, carry):
        hr, hi = carry
        nhr = lr * hr - li * hi + xr[pl.ds(t, 1), :]
        nhi = lr * hi + li * hr + xi[pl.ds(t, 1), :]
        xr[pl.ds(t, 1), :] = nhr
        xi[pl.ds(t, 1), :] = nhi
        return nhr, nhi

    hr, hi = lax.fori_loop(0, tc, step, (cr[...], ci[...]), unroll=8)
    cr[...] = hr
    ci[...] = hi
    y_ref[...] = _s5_tail(u, xr[...], xi[...], ccr_ref[...], cci_ref[...], d_ref[...], gw_ref[...], gb_ref[...])

    @pl.when(i == pl.num_programs(1) - 1)
    def _():
        hr_out[0] = hr
        hi_out[0] = hi


def _s5_prompt(u, b, t, h0r, h0i, prm):
    bbr, bbi, lr, li, ccr, cci, d, gw, gb = prm
    n, ch = u.shape
    ns = lr.shape[1]
    tc = min(512, t)
    nt = t // tc
    st = pl.BlockSpec((1, 1, ns), lambda bi, i: (bi, 0, 0))
    sd = jax.ShapeDtypeStruct
    return pl.pallas_call(
        _s5_prompt_kernel,
        grid=(b, nt),
        in_specs=[pl.BlockSpec((tc, ch), lambda bi, i: (bi * nt + i, 0)),
                  _const_spec(bbr.shape), _const_spec(bbi.shape), _const_spec(lr.shape), _const_spec(li.shape),
                  st, st, _const_spec(ccr.shape), _const_spec(cci.shape), _const_spec(d.shape),
                  _const_spec(gw.shape), _const_spec(gb.shape)],
        out_specs=[pl.BlockSpec((tc, ch), lambda bi, i: (bi * nt + i, 0)), st, st],
        out_shape=[sd((n, ch), F32), sd((b, 1, ns), F32), sd((b, 1, ns), F32)],
        scratch_shapes=[pltpu.VMEM((tc, ns), F32), pltpu.VMEM((tc, ns), F32),
                        pltpu.VMEM((1, ns), F32), pltpu.VMEM((1, ns), F32)],
        compiler_params=_params("parallel", "arbitrary"),
        name="s5_prompt",
    )(u, bbr, bbi, lr, li, h0r, h0i, ccr, cci, d, gw, gb)


def _s5_sample_kernel(u_ref, bbr_ref, bbi_ref, lr_ref, li_ref, h0r_ref, h0i_ref, ccr_ref, cci_ref,
                      d_ref, gw_ref, gb_ref, y_ref, hr_out, hi_out):
    lr, li = lr_ref[...], li_ref[...]
    hr, hi = h0r_ref[...], h0i_ref[...]
    for t in range(u_ref.shape[0]):
        u = u_ref[t]
        ub = u.astype(BF16)
        xr = _dot(ub, bbr_ref[...])
        xi = _dot(ub, bbi_ref[...])
        hr, hi = lr * hr - li * hi + xr, lr * hi + li * hr + xi
        y_ref[t] = _s5_tail(u, hr, hi, ccr_ref[...], cci_ref[...], d_ref[...], gw_ref[...], gb_ref[...])
    hr_out[...] = hr
    hi_out[...] = hi


def _s5_sample(u_tm, h0r, h0i, prm):
    bbr, bbi, lr, li, ccr, cci, d, gw, gb = prm
    sd = jax.ShapeDtypeStruct
    return pl.pallas_call(
        _s5_sample_kernel,
        out_shape=[sd(u_tm.shape, F32), sd(h0r.shape, F32), sd(h0i.shape, F32)],
        compiler_params=pltpu.CompilerParams(vmem_limit_bytes=VMEM_LIMIT_BYTES),
        name="s5_sample",
    )(u_tm, bbr, bbi, lr, li, h0r, h0i, ccr, cci, d, gw, gb)


def _dil_prompt_kernel(q_ref, kp_ref, kc_ref, vp_ref, vc_ref, o_ref, l_ref):
    n = pl.program_id(1)
    blk = q_ref.shape[1]
    q = q_ref[0] * (DH_C ** -0.5)
    kcat = jnp.concatenate([kp_ref[0], kc_ref[0]], axis=0).astype(BF16)
    vcat = jnp.concatenate([vp_ref[0], vc_ref[0]], axis=0).astype(BF16)
    qi = lax.broadcasted_iota(jnp.int32, (blk, 1), 0) + blk
    ki = lax.broadcasted_iota(jnp.int32, (1, 2 * blk), 1)
    dist = qi - ki
    lo_k = jnp.where(n > 0, 0, blk)
    ok = (dist >= 0) & (dist <= blk) & (ki >= lo_k)
    lane = lax.broadcasted_iota(jnp.int32, (1, q.shape[1]), 1)
    first = lane < DH_C
    o = None
    lse = None
    for hh in range(2):
        sel = first if hh == 0 else jnp.logical_not(first)
        qm = jnp.where(sel, q, 0.0).astype(BF16)
        s = jnp.where(ok, _dot_nt(qm, kcat), NEG)
        m = jnp.max(s, -1, keepdims=True)
        pr = jnp.exp(s - m)
        den = jnp.sum(pr, -1, keepdims=True)
        oh = _dot((pr / den).astype(BF16), vcat)
        lh = m + jnp.log(den)
        o = oh if o is None else jnp.where(first, o, oh)
        lse = jnp.broadcast_to(lh, oh.shape) if lse is None else jnp.where(first, lse, lh)
    o_ref[0] = o
    l_ref[0] = lse


def _dil_prompt(qd, kd, vd):
    bd, ns, w = qd.shape
    pw = 2 * DH_C
    cur = pl.BlockSpec((1, C_BLK, pw), lambda b, n, h: (b, n, h))
    prev = pl.BlockSpec((1, C_BLK, pw), lambda b, n, h: (b, jnp.maximum(n - 1, 0), h))
    sd = jax.ShapeDtypeStruct((bd, ns, w), F32)
    return pl.pallas_call(
        _dil_prompt_kernel,
        grid=(bd, ns // C_BLK, w // pw),
        in_specs=[cur, prev, cur, prev, cur],
        out_specs=[cur, cur],
        out_shape=[sd, sd],
        compiler_params=_params("parallel", "parallel", "parallel"),
        name="dilated_attn_prompt",
    )(qd, kd, kd, vd, vd)


def _dil_sample_kernel(q_ref, kn_ref, vn_ref, k1_ref, k4_ref, k16_ref, v1_ref, v4_ref, v16_ref, o_ref, *, s_len):
    w = q_ref.shape[2]
    nh = w // DH_C
    row_h = lax.broadcasted_iota(jnp.int32, (nh, 1), 0)
    lane_h = lax.broadcasted_iota(jnp.int32, (1, w), 1) // DH_C
    diag = lane_h == row_h
    q8 = q_ref[0] * (DH_C ** -0.5)
    kn, vn = kn_ref[0].astype(BF16), vn_ref[0].astype(BF16)
    k1, v1 = k1_ref[0].astype(BF16), v1_ref[0].astype(BF16)
    col = lax.broadcasted_iota(jnp.int32, (1, C_BLK), 1)
    coln = lax.broadcasted_iota(jnp.int32, (1, SUBLANES), 1)
    out_rows = []
    for s in range(s_len):
        qbd = jnp.where(diag, jnp.broadcast_to(q8[s:s + 1, :], (nh, w)), 0.0).astype(BF16)
        k4 = k4_ref[0, :, s * w:(s + 1) * w].astype(BF16)
        v4 = v4_ref[0, :, s * w:(s + 1) * w].astype(BF16)
        k16 = k16_ref[0, :, s * w:(s + 1) * w].astype(BF16)
        v16 = v16_ref[0, :, s * w:(s + 1) * w].astype(BF16)
        s1 = jnp.where(col >= s, _dot_nt(qbd, k1), NEG)
        s4 = _dot_nt(qbd, k4)
        s16 = _dot_nt(qbd, k16)
        mult = jnp.where(coln <= s, 1.0, 0.0) + jnp.where(coln == s, float(len(C_DILATIONS) - 1), 0.0)
        sn = jnp.where(coln <= s, _dot_nt(qbd, kn), NEG)
        m = jnp.maximum(jnp.maximum(jnp.max(s1, -1, keepdims=True), jnp.max(s4, -1, keepdims=True)),
                        jnp.maximum(jnp.max(s16, -1, keepdims=True), jnp.max(sn, -1, keepdims=True)))
        p1, p4, p16 = jnp.exp(s1 - m), jnp.exp(s4 - m), jnp.exp(s16 - m)
        pn = mult * jnp.exp(sn - m)
        den = (jnp.sum(p1, -1, keepdims=True) + jnp.sum(p4, -1, keepdims=True)
               + jnp.sum(p16, -1, keepdims=True) + jnp.sum(pn, -1, keepdims=True))
        inv = 1.0 / den
        o = (_dot((p1 * inv).astype(BF16), v1) + _dot((p4 * inv).astype(BF16), v4)
             + _dot((p16 * inv).astype(BF16), v16) + _dot((pn * inv).astype(BF16), vn))
        out_rows.append(jnp.sum(jnp.where(diag, o, 0.0), axis=0, keepdims=True))
    out_rows.append(jnp.zeros((SUBLANES - s_len, w), F32))
    o_ref[0] = jnp.concatenate(out_rows, axis=0)


def _dil_sample(q8, kn8, vn8, cache_k, cache_v, s_len):
    bs, buf, w = cache_k.shape
    assert buf == C_BLK * C_DILATIONS[-1] and s_len <= C_DILATIONS[1]
    tok = pl.BlockSpec((1, SUBLANES, w), lambda b: (b, 0, 0))
    d1 = lambda c: c
    d4 = lambda c: c.reshape(bs, buf // 4, 4 * w)
    d16 = lambda c: c.reshape(bs, buf // 16, 16 * w)
    s1 = pl.BlockSpec((1, C_BLK, w), lambda b: (b, buf // C_BLK - 1, 0))
    s4 = pl.BlockSpec((1, C_BLK, 4 * w), lambda b: (b, buf // 4 // C_BLK - 1, 0))
    s16 = pl.BlockSpec((1, C_BLK, 4 * w), lambda b: (b, 0, 0))
    return pl.pallas_call(
        functools.partial(_dil_sample_kernel, s_len=s_len),
        grid=(bs,),
        in_specs=[tok, tok, tok, s1, s4, s16, s1, s4, s16],
        out_specs=tok,
        out_shape=jax.ShapeDtypeStruct((bs, SUBLANES, w), F32),
        compiler_params=_params("parallel"),
        name="dilated_attn_sample",
    )(q8, kn8, vn8, d1(cache_k), d4(cache_k), d16(cache_k), d1(cache_v), d4(cache_v), d16(cache_v))


def _softplus(x):
    return jnp.maximum(x, 0.0) + jnp.log1p(jnp.exp(-jnp.abs(x)))


def _rwkv_pre_kernel(pd_ref, pv_ref, mu_ref, w0_ref, w2_ref, a0_ref, a2_ref, g2_ref, kk_ref, ka_ref, rk_ref, seg_ref,
                     r_o, ld_o, k_o, v_o, kk_o, b_o, g_o, bonus_o):
    pd = pd_ref[...]
    xm = pd + (pv_ref[...] - pd) * mu_ref[...]
    o1, o2, o3 = D_W, 2 * D_W, 3 * D_W
    o5 = o3 + D_LORA_W + D_LORA_A
    r, k, v = xm[:, :o1], xm[:, o1:o2], xm[:, o2:o3]
    wa, gl = xm[:, o3:o5], xm[:, o5:]
    lw = _dot(jnp.tanh(wa).astype(BF16), w2_ref[...])
    la = _dot(wa.astype(BF16), a2_ref[...])
    g = _dot(_sigmoid(gl).astype(BF16), g2_ref[...])
    w_log = -_softplus(-(w0_ref[...] + lw)) - 0.5
    a = _sigmoid(a0_ref[...] + la)
    seg = seg_ref[...]
    kk = k * kk_ref[...]
    kk = kk / jnp.maximum(jnp.sqrt(_dot_exact_rhs(kk * kk, seg)), 1e-12)
    k2 = k * (1.0 + (a - 1.0) * ka_ref[...])
    r_o[...] = r
    ld_o[...] = -jnp.exp(w_log)
    k_o[...] = k2
    v_o[...] = v
    kk_o[...] = kk
    b_o[...] = kk * a
    g_o[...] = g
    bonus_o[...] = _dot_exact_rhs(r * k2 * rk_ref[...], seg) * v


def _rwkv_pre(pd, prev, prm):
    n, cols = pd.shape
    tm = min(TOKEN_TILE, n)
    tok = lambda wd: pl.BlockSpec((tm, wd), lambda i: (i, 0))
    return pl.pallas_call(
        _rwkv_pre_kernel,
        grid=(n // tm,),
        in_specs=[tok(cols), tok(cols)] + [_const_spec(p.shape) for p in prm],
        out_specs=[tok(D_W)] * 8,
        out_shape=[jax.ShapeDtypeStruct((n, D_W), F32)] * 8,
        compiler_params=_params("parallel"),
        name="rwkv_pre",
    )(pd, prev, *prm)


def _rwkv_chunk_kernel(r_ref, ld_ref, k_ref, v_ref, kk_ref, b_ref, y_ref, s_out, st):
    c = pl.program_id(1)
    ch = r_ref.shape[0]
    pw = 2 * N_D

    @pl.when(c == 0)
    def _():
        st[...] = jnp.zeros(st.shape, F32)

    ri = lax.broadcasted_iota(jnp.int32, (ch, ch), 0)
    ci = lax.broadcasted_iota(jnp.int32, (ch, ch), 1)
    tri_incl = ci <= ri
    tri_strict = ci < ri
    eye_c = jnp.where(ci == ri, 1.0, 0.0)
    ld = ld_ref[...]
    cum = _dot_exact_rhs_left(jnp.where(tri_incl, 1.0, 0.0).astype(BF16), ld)
    cum_end = cum[ch - 1:ch, :]
    g_inc = jnp.exp(cum)
    g_inv = jnp.exp(-cum)
    g_end = jnp.exp(cum_end - cum)
    rho = r_ref[...] * g_inc
    kap = kk_ref[...] * jnp.exp(cum - ld)
    kh = k_ref[...] * g_inv
    bh = b_ref[...] * g_inv
    khg = k_ref[...] * g_end
    bhg = b_ref[...] * g_end
    gam_end = jnp.exp(cum_end)
    v = v_ref[...]

    pi = lax.broadcasted_iota(jnp.int32, (pw, pw), 0)
    pj = lax.broadcasted_iota(jnp.int32, (pw, pw), 1)
    same_head = (pi // N_D) == (pj // N_D)
    lane = lax.broadcasted_iota(jnp.int32, (1, pw), 1)
    first = lane < N_D

    for pr in range(r_ref.shape[1] // pw):
        sl = slice(pr * pw, (pr + 1) * pw)
        kap_p, rho_p = kap[:, sl], rho[:, sl]
        kh_b, bh_b, v_b = kh[:, sl].astype(BF16), bh[:, sl].astype(BF16), v[:, sl].astype(BF16)
        kap_b = kap_p.astype(BF16)
        kap2 = wr = rho2 = yloc = None
        for hh in range(2):
            sel = first if hh == 0 else jnp.logical_not(first)
            kap_m = jnp.where(sel, kap_p, 0.0).astype(BF16)
            rho_m = jnp.where(sel, rho_p, 0.0).astype(BF16)
            a_b = jnp.where(tri_strict, _dot_nt(kap_m, bh_b), 0.0)
            a_k = jnp.where(tri_strict, _dot_nt(kap_m, kh_b), 0.0)
            ap_b = jnp.where(tri_incl, _dot_nt(rho_m, bh_b), 0.0).astype(BF16)
            ap_k = jnp.where(tri_incl, _dot_nt(rho_m, kh_b), 0.0).astype(BF16)
            tl = eye_c - a_b
            pw2 = _dot(a_b.astype(BF16), a_b.astype(BF16))
            span = 2
            while span < ch:
                tl = tl + _dot(tl.astype(BF16), pw2.astype(BF16))
                span *= 2
                if span < ch:
                    pw2 = _dot(pw2.astype(BF16), pw2.astype(BF16))
            tl_b = tl.astype(BF16)
            kap2_h = _dot(tl_b, kap_b)
            wr_h = _dot(tl_b, _dot(a_k.astype(BF16), v_b).astype(BF16))
            rho2_h = rho_p - _dot(ap_b, kap2_h.astype(BF16))
            yloc_h = _dot(ap_k, v_b) - _dot(ap_b, wr_h.astype(BF16))
            if hh == 0:
                kap2, wr, rho2, yloc = kap2_h, wr_h, rho2_h, yloc_h
            else:
                kap2 = jnp.where(first, kap2, kap2_h)
                wr = jnp.where(first, wr, wr_h)
                rho2 = jnp.where(first, rho2, rho2_h)
                yloc = jnp.where(first, yloc, yloc_h)
        bhg_b = bhg[:, sl].astype(BF16)
        diag = jnp.where(pi == pj, jnp.broadcast_to(gam_end[:, sl], (pw, pw)), 0.0)
        phi = jnp.where(same_head, diag - _dot_tn(bhg_b, kap2.astype(BF16)), 0.0)
        gmat = jnp.where(same_head, _dot_tn(khg[:, sl].astype(BF16), v_b) - _dot_tn(bhg_b, wr.astype(BF16)), 0.0)
        s_b = st[pr].astype(BF16)
        y_ref[:, sl] = _dot(rho2.astype(BF16), s_b) + yloc
        st[pr] = _dot(phi.astype(BF16), s_b) + gmat

    @pl.when(c == pl.num_programs(1) - 1)
    def _():
        s_out[0] = st[...]


def _dot_exact_rhs_left(m01, x):
    hi, mid, lo = _split3(x)
    return _dot(m01, hi) + _dot(m01, mid) + _dot(m01, lo)


def _rwkv_chunk(r, ld, k2, v, kk, bb, b, t):
    n, w = r.shape
    ch = min(RWKV_CHUNK, t)
    nc = t // ch
    npair = w // (2 * N_D)
    tok = pl.BlockSpec((ch, w), lambda bi, c: (bi * nc + c, 0))
    return pl.pallas_call(
        _rwkv_chunk_kernel,
        grid=(b, nc),
        in_specs=[tok] * 6,
        out_specs=[tok, pl.BlockSpec((1, npair, 2 * N_D, 2 * N_D), lambda bi, c: (bi, 0, 0, 0))],
        out_shape=[jax.ShapeDtypeStruct((n, w), F32), jax.ShapeDtypeStruct((b, npair, 2 * N_D, 2 * N_D), F32)],
        scratch_shapes=[pltpu.VMEM((npair, 2 * N_D, 2 * N_D), F32)],
        compiler_params=_params("parallel", "arbitrary"),
        name="rwkv_chunk_scan",
    )(r, ld, k2, v, kk, bb)


def _rwkv_lane_kernel(r_ref, ld_ref, k_ref, v_ref, kk_ref, b_ref, s_ref, y_ref, s_out):
    steps = r_ref.shape[0]

    def body(vi, carry):
        s = s_ref[0, vi]
        for t in range(steps):
            sk = jnp.sum(s * kk_ref[t, 0], axis=0, keepdims=True)
            vv = v_ref[t, 0, pl.ds(vi, 1), :]
            s = s * jnp.exp(ld_ref[t, 0]) - sk * b_ref[t, 0] + vv * k_ref[t, 0]
            y_ref[t, 0, pl.ds(vi, 1), :] = jnp.sum(s * r_ref[t, 0], axis=0, keepdims=True)
        s_out[0, vi] = s
        return carry

    lax.fori_loop(0, s_ref.shape[1], body, 0)


def _rwkv_lane(rt, ldt, kt, vt, kkt, bt, s0):
    steps, nh, nd, bs = rt.shape
    tok = pl.BlockSpec((steps, 1, nd, bs), lambda h: (0, h, 0, 0))
    stt = pl.BlockSpec((1, nd, nd, bs), lambda h: (h, 0, 0, 0))
    return pl.pallas_call(
        _rwkv_lane_kernel,
        grid=(nh,),
        in_specs=[tok] * 6 + [stt],
        out_specs=[tok, stt],
        out_shape=[jax.ShapeDtypeStruct(rt.shape, F32), jax.ShapeDtypeStruct(s0.shape, F32)],
        compiler_params=_params("parallel"),
        name="rwkv_lane_scan",
    )(rt, ldt, kt, vt, kkt, bt, s0)


def _rwkv_post_kernel(y_ref, bonus_ref, g_ref, gw_ref, gb_ref, seg_ref, o_ref):
    y = y_ref[...]
    seg = seg_ref[...]
    mu = _dot_exact_rhs(y, seg) * (1.0 / N_D)
    yc = y - mu
    var = _dot_exact_rhs(yc * yc, seg) * (1.0 / N_D)
    yn = yc * lax.rsqrt(var + GN_EPS) * gw_ref[...] + gb_ref[...]
    o_ref[...] = (yn + bonus_ref[...]) * g_ref[...]


def _rwkv_post(y, bonus, g, gn_w, gn_b, seg):
    n, w = y.shape
    tm = min(TOKEN_TILE, n)
    tok = pl.BlockSpec((tm, w), lambda i: (i, 0))
    return pl.pallas_call(
        _rwkv_post_kernel,
        grid=(n // tm,),
        in_specs=[tok, tok, tok, _const_spec((1, w)), _const_spec((1, w)), _const_spec(seg.shape)],
        out_specs=tok,
        out_shape=jax.ShapeDtypeStruct((n, w), F32),
        compiler_params=_params("parallel"),
        name="rwkv_post",
    )(y, bonus, g, gn_w.reshape(1, w), gn_b.reshape(1, w), seg)


def _pad_tokens(x, bs, s_len):
    x = x.reshape(bs, s_len, x.shape[-1])
    return jnp.pad(x, ((0, 0), (0, SUBLANES - s_len), (0, 0)))


def _deinterleave(x, b, t, d):
    return x.reshape(b, t // d, d, x.shape[-1]).transpose(0, 2, 1, 3).reshape(b * d, t // d, x.shape[-1])


def _interleave(x, b, t, d):
    return x.reshape(b, d, t // d, x.shape[-1]).transpose(0, 2, 1, 3).reshape(b * t, x.shape[-1])


def _block_diag_in(bb):
    g, c, p = bb.shape
    return jnp.einsum('gcp,gh->gchp', bb, jnp.eye(g, dtype=bb.dtype)).reshape(g * c, g * p)


def _block_diag_out(cc):
    g, c, p = cc.shape
    return jnp.einsum('gcp,gh->gphc', cc, jnp.eye(g, dtype=cc.dtype)).reshape(g * p, g * c)


def _shifted(pd, shift0, b, t):
    pd3 = pd.reshape(b, t, pd.shape[-1])
    return jnp.concatenate([shift0[:, None, :], pd3[:, :-1]], axis=1).reshape(b * t, pd.shape[-1])


def _trunk(x, mod_all, row0, rows, per_token_mod, sample, st, p):
    b, t, d = x.shape
    n = b * t
    xt = x.reshape(n, d)
    tm = min(TOKEN_TILE, n)
    outs = {}
    for l in range(DEPTH):
        mod = mod_all[l, row0:row0 + rows].reshape(rows, 9, d)

        def mods(i):
            sel = [mod[:, 3 * i + j] for j in range(3)]
            if per_token_mod:
                return [jnp.repeat(m, t, axis=0).reshape(n // tm, tm, d) for m in sel], 1
            return [m.reshape(b, 1, d) for m in sel], t // tm

        m0, tpg = mods(0)
        xt = _ffn(xt, m0, tpg, p['wg'][l][0], p['wu'][l][0], p['wd'][l][0], p['ln_g'][l, 0], p['ln_b'][l, 0])
        m1, _ = mods(1)
        if l % 2 == 0:
            e = l // 2
            lam_init = 0.8 - 0.6 * math.exp(-0.3 * l)
            q, k, v, u = _inproj(xt, m1, tpg, p['even_w_in'][e], (512, 512, 512, S5_CH))
            dl, subln = p['diff_lambda'][e], p['diff_subln'][e]
            if sample:
                att8 = _diff_sample(_pad_tokens(q, b, t), _pad_tokens(k, b, t), _pad_tokens(v, b, t),
                                    st['pool_k'][e], st['pool_v'][e], st['page_table'], dl, subln, lam_init)
                att = att8[:, :t].reshape(n, -1)
                u_tm = u.reshape(b, t, -1).transpose(1, 0, 2)
                y_tm, hr, hi = _s5_sample(u_tm, st['s5_re'][e].reshape(b, S5_N), st['s5_im'][e].reshape(b, S5_N),
                                          p['s5'][e])
                y5 = y_tm.transpose(1, 0, 2).reshape(n, -1)
            else:
                att = _diff_prompt(q, k, v, dl, subln, b, t, lam_init)
                zero = jnp.zeros((b, 1, S5_N), F32)
                y5, hr, hi = _s5_prompt(u, b, t, zero, zero, p['s5'][e])
            outs.setdefault('ak', []).append(k.reshape(b, t, H_A, 2, DH_A))
            outs.setdefault('av', []).append(v.reshape(b, t, H_A, 2 * DH_A))
            outs.setdefault('s5r', []).append(hr.reshape(b, S5_G, S5_P))
            outs.setdefault('s5i', []).append(hi.reshape(b, S5_G, S5_P))
            xt = _outproj(xt, m1[2], tpg, [att], y5, p['even_w_out'][e], p['ln_g'][l, 1], p['ln_b'][l, 1])
        else:
            o = l // 2
            q, k, v, pd = _inproj(xt, m1, tpg, p['odd_w_in'][o], (512, 512, 512, pd_cols(p)))
            rp = p['rwkv'][o]
            if sample:
                att8 = _dil_sample(_pad_tokens(q, b, t), _pad_tokens(k, b, t), _pad_tokens(v, b, t),
                                   st['cache_c_k'][o], st['cache_c_v'][o], t)
                atts = [att8[:, :t].reshape(n, -1)]
                prev = _shifted(pd, st['d_shift'][o], b, t)
                r, ld, k2, vv, kk, bb, g, bonus = _rwkv_pre(pd, prev, rp['pre'])
                tl = lambda z: z.reshape(b, t, H_D, N_D).transpose(1, 2, 3, 0)
                s0 = st['d_wkv'][o].transpose(1, 2, 3, 0)
                y_l, s_new = _rwkv_lane(tl(r), tl(ld), tl(k2), tl(vv), tl(kk), tl(bb), s0)
                y = y_l.transpose(3, 0, 1, 2).reshape(n, D_W)
                s_new = s_new.transpose(3, 0, 1, 2)
                k_keep, v_keep = k.reshape(b, t, H_C, DH_C), v.reshape(b, t, H_C, DH_C)
            else:
                atts_o, atts_l = [], []
                for dil in C_DILATIONS:
                    if dil == 1:
                        ob, lb = _dil_prompt(q.reshape(b, t, -1), k.reshape(b, t, -1), v.reshape(b, t, -1))
                        ob, lb = ob.reshape(n, -1), lb.reshape(n, -1)
                    else:
                        ob, lb = _dil_prompt(_deinterleave(q, b, t, dil), _deinterleave(k, b, t, dil),
                                             _deinterleave(v, b, t, dil))
                        ob, lb = _interleave(ob, b, t, dil), _interleave(lb, b, t, dil)
                    atts_o.append(ob)
                    atts_l.append(lb)
                atts = atts_o + atts_l
                prev = _shifted(pd, jnp.zeros((b, pd.shape[-1]), F32), b, t)
                r, ld, k2, vv, kk, bb, g, bonus = _rwkv_pre(pd, prev, rp['pre'])
                y, s_pairs = _rwkv_chunk(r, ld, k2, vv, kk, bb, b, t)
                sp = s_pairs.reshape(b, H_D // 2, 2, N_D, 2, N_D)
                s_new = jnp.stack([sp[:, :, 0, :, 0, :], sp[:, :, 1, :, 1, :]], axis=2)
                s_new = s_new.reshape(b, H_D, N_D, N_D).transpose(0, 1, 3, 2)
                keep = min(C_BLK * C_DILATIONS[-1], t)
                k_keep = k.reshape(b, t, H_C, DH_C)[:, t - keep:]
                v_keep = v.reshape(b, t, H_C, DH_C)[:, t - keep:]
            yd = _rwkv_post(y, bonus, g, rp['gn_w'], rp['gn_b'], rp['seg'])
            outs.setdefault('ck', []).append(k_keep)
            outs.setdefault('cv', []).append(v_keep)
            outs.setdefault('dw', []).append(s_new)
            outs.setdefault('ds', []).append(pd.reshape(b, t, -1)[:, -1])
            xt = _outproj(xt, m1[2], tpg, atts, yd, p['odd_w_out'][o], p['ln_g'][l, 1], p['ln_b'][l, 1])
        m2, _ = mods(2)
        xt = _ffn(xt, m2, tpg, p['wg'][l][1], p['wu'][l][1], p['wd'][l][1], p['ln_g'][l, 2], p['ln_b'][l, 2])
    stacked = [jnp.stack(outs[key], 0) for key in ('ak', 'av', 's5r', 's5i', 'ck', 'cv', 'dw', 'ds')]
    return xt.reshape(b, t, d), stacked


def pd_cols(p):
    return p['odd_w_in'].shape[-1] - 3 * H_C * DH_C


def kernel(x_prompt, x_sample, cache_a_k, cache_a_v, state_s5_re, state_s5_im, cache_c_k, cache_c_v, state_d_wkv, state_d_shift, page_table, c_prompt, c_sample, ada_w, ada_b, ln_g, ln_b, ffn_w_gate, ffn_w_up, ffn_w_down, even_w_in, even_w_out, diff_lambda, diff_subln, s5_a_re, s5_a_im, s5_log_dt, s5_b_re, s5_b_im, s5_c_re, s5_c_im, s5_d, s5_glu_w, s5_glu_b, odd_w_in, odd_w_out, rwkv_mu, rwkv_w0, rwkv_w2, rwkv_a0, rwkv_a2, rwkv_g2, rwkv_k_k, rwkv_k_a, rwkv_r_k, rwkv_gn_w, rwkv_gn_b):
    bp, bs = x_prompt.shape[0], x_sample.shape[0]
    n_even, n_odd = even_w_in.shape[0], odd_w_in.shape[0]
    bf = lambda w: w.astype(BF16)

    seg = jnp.kron(jnp.eye(H_D, dtype=F32), jnp.ones((N_D, N_D), F32)).astype(BF16)
    s5 = []
    for e in range(n_even):
        lr, li, bbr, bbi = _s5_prep(s5_a_re[e], s5_a_im[e], s5_log_dt[e], s5_b_re[e], s5_b_im[e])
        s5.append((bf(_block_diag_in(bbr)), bf(_block_diag_in(bbi)), lr.reshape(1, S5_N), li.reshape(1, S5_N),
                   bf(_block_diag_out(s5_c_re[e])), bf(_block_diag_out(s5_c_im[e])),
                   s5_d[e].reshape(1, S5_CH), bf(s5_glu_w[e]), s5_glu_b[e].reshape(1, S5_CH)))
    rwkv = []
    for o in range(n_odd):
        row = lambda z: z.reshape(1, -1)
        w2p = jnp.concatenate([rwkv_w2[o], jnp.zeros_like(rwkv_a2[o])], axis=0)
        a2p = jnp.concatenate([jnp.zeros_like(rwkv_w2[o]), rwkv_a2[o]], axis=0)
        pre = (row(rwkv_mu[o]), row(rwkv_w0[o]), bf(w2p), row(rwkv_a0[o]), bf(a2p), bf(rwkv_g2[o]),
               row(rwkv_k_k[o]), row(rwkv_k_a[o]), row(rwkv_r_k[o]), seg)
        rwkv.append(dict(pre=pre, gn_w=rwkv_gn_w[o], gn_b=rwkv_gn_b[o], seg=seg))
    p = dict(wg=bf(ffn_w_gate), wu=bf(ffn_w_up), wd=bf(ffn_w_down), ln_g=ln_g, ln_b=ln_b,
             even_w_in=bf(even_w_in), even_w_out=bf(even_w_out), odd_w_in=bf(odd_w_in), odd_w_out=bf(odd_w_out),
             diff_lambda=diff_lambda, diff_subln=diff_subln, s5=s5, rwkv=rwkv)

    mod_all = _ada(jnp.concatenate([c_prompt, c_sample], axis=0), ada_w, ada_b)

    y_prompt, st_p = _trunk(x_prompt, mod_all, 0, bp, False, False, None, p)
    pool_w = cache_a_k.shape[3] * cache_a_k.shape[4] * cache_a_k.shape[5]
    st = dict(pool_k=cache_a_k.reshape(n_even, cache_a_k.shape[1], cache_a_k.shape[2], pool_w),
              pool_v=cache_a_v.reshape(n_even, cache_a_v.shape[1], cache_a_v.shape[2], pool_w),
              page_table=page_table, s5_re=state_s5_re, s5_im=state_s5_im,
              cache_c_k=cache_c_k.reshape(n_odd, bs, cache_c_k.shape[2], -1),
              cache_c_v=cache_c_v.reshape(n_odd, bs, cache_c_v.shape[2], -1),
              d_wkv=state_d_wkv, d_shift=state_d_shift)
    y_sample, st_s = _trunk(x_sample, mod_all, bp, bs, True, True, st, p)
    a_k_p, a_v_p, s5_re_p, s5_im_p, c_k_p, c_v_p, d_wkv_p, d_shift_p = st_p
    a_k_s, a_v_s, s5_re_s, s5_im_s, c_k_s, c_v_s, d_wkv_s, d_shift_s = st_s
    return (y_prompt, y_sample, a_k_p, a_k_s, a_v_p, a_v_s, s5_re_p, s5_re_s, s5_im_p, s5_im_s,
            c_k_p, c_k_s, c_v_p, c_v_s, d_wkv_p, d_wkv_s, d_shift_p, d_shift_s)
```

```python
import functools
import math

import jax
import jax.numpy as jnp
from jax import lax
from jax.experimental import pallas as pl
from jax.experimental.pallas import tpu as pltpu

F32 = jnp.float32
BF16 = jnp.bfloat16

DEPTH = 2
H_A, DH_A = 4, 64
S5_GROUP, S5_G, S5_P = 16, 32, 64
S5_CH = S5_GROUP * S5_G
S5_N = S5_G * S5_P
H_C, DH_C = 8, 64
C_BLK = 128
C_DILATIONS = (1, 4, 16)
H_D, N_D = 8, 64
D_W = H_D * N_D
D_LORA_W, D_LORA_A, D_LORA_G = 64, 64, 128
GN_EPS = 64e-5
ALPHA = (2.0 * DEPTH) ** 0.25
LN_EPS = 1e-5
NEG = -1e30

LANES = 128
SUBLANES = 8
VMEM_LIMIT_BYTES = 56 * 1024 * 1024
TOKEN_TILE = 512
RWKV_CHUNK = 64


def _params(*sem):
    return pltpu.CompilerParams(dimension_semantics=sem, vmem_limit_bytes=VMEM_LIMIT_BYTES)


def _const_spec(shape):
    nd = len(shape)
    return pl.BlockSpec(shape, lambda *_: (0,) * nd, pipeline_mode=pl.Buffered(1))


def _dot(a, b):
    return jnp.dot(a, b, preferred_element_type=F32)


def _dot_nt(a, b):
    return lax.dot_general(a, b, (((1,), (1,)), ((), ())), preferred_element_type=F32)


def _dot_tn(a, b):
    return lax.dot_general(a, b, (((0,), (0,)), ((), ())), preferred_element_type=F32)


def _split3(x):
    hi = x.astype(BF16)
    r1 = x - hi.astype(F32)
    mid = r1.astype(BF16)
    lo = (r1 - mid.astype(F32)).astype(BF16)
    return hi, mid, lo


def _dot_exact_rhs(x, m01):
    hi, mid, lo = _split3(x)
    return _dot(hi, m01) + _dot(mid, m01) + _dot(lo, m01)


def _layer_norm(y, g, b):
    mu = jnp.mean(y, -1, keepdims=True)
    yc = y - mu
    var = jnp.mean(yc * yc, -1, keepdims=True)
    return yc * lax.rsqrt(var + LN_EPS) * g + b


def _sigmoid(x):
    return jax.nn.sigmoid(x)


def _mod_spec(mod, tiles_per_group):
    _, rm, d = mod.shape
    return pl.BlockSpec((1, rm, d), lambda i: (i // tiles_per_group, 0, 0))


def _ada_kernel(c_ref, w_ref, b_ref, o_ref):
    c = c_ref[...]
    s = (c * _sigmoid(c)).astype(BF16)
    o_ref[0] = _dot(s, w_ref[0].astype(BF16)) + b_ref[0]


def _ada(c_all, ada_w, ada_b):
    nl, d, w = ada_w.shape
    r = c_all.shape[0]
    tn = 1152 if w % 1152 == 0 else w
    return pl.pallas_call(
        _ada_kernel,
        grid=(nl, w // tn),
        in_specs=[pl.BlockSpec((r, d), lambda l, j: (0, 0)),
                  pl.BlockSpec((1, d, tn), lambda l, j: (l, 0, j)),
                  pl.BlockSpec((1, 1, tn), lambda l, j: (l, 0, j))],
        out_specs=pl.BlockSpec((1, r, tn), lambda l, j: (l, 0, j)),
        out_shape=jax.ShapeDtypeStruct((nl, r, w), F32),
        compiler_params=_params("parallel", "parallel"),
        name="ada_mod",
    )(c_all, ada_w, ada_b.reshape(nl, 1, w))


def _ffn_kernel(x_ref, sh_ref, sc_ref, gt_ref, wg_ref, wu_ref, wd_ref, g_ref, b_ref, o_ref):
    x = x_ref[...]
    h = (x * (1.0 + sc_ref[0]) + sh_ref[0]).astype(BF16)
    g = _dot(h, wg_ref[...])
    u = _dot(h, wu_ref[...])
    a = (g * _sigmoid(g) * u).astype(BF16)
    f = _dot(a, wd_ref[...])
    y = ALPHA * x + 0.5 * (1.0 + gt_ref[0]) * f
    o_ref[...] = _layer_norm(y, g_ref[...], b_ref[...])


def _ffn(x, mods, tpg, wg, wu, wd, ln_g, ln_b):
    n, d = x.shape
    f = wg.shape[1]
    tm = min(TOKEN_TILE, n)
    sh, sc, gt = mods
    return pl.pallas_call(
        _ffn_kernel,
        grid=(n // tm,),
        in_specs=[pl.BlockSpec((tm, d), lambda i: (i, 0)),
                  _mod_spec(sh, tpg), _mod_spec(sc, tpg), _mod_spec(gt, tpg),
                  _const_spec((d, f)), _const_spec((d, f)), _const_spec((f, d)),
                  _const_spec((1, d)), _const_spec((1, d))],
        out_specs=pl.BlockSpec((tm, d), lambda i: (i, 0)),
        out_shape=jax.ShapeDtypeStruct((n, d), F32),
        compiler_params=_params("parallel"),
        name="ffn_ln",
    )(x, sh, sc, gt, wg, wu, wd, ln_g.reshape(1, d), ln_b.reshape(1, d))


def _inproj_kernel(x_ref, sh_ref, sc_ref, w_ref, *o_refs, widths):
    h = (x_ref[...] * (1.0 + sc_ref[0]) + sh_ref[0]).astype(BF16)
    p = _dot(h, w_ref[...])
    off = 0
    for o_ref, wd in zip(o_refs, widths):
        o_ref[...] = p[:, off:off + wd]
        off += wd


def _inproj(x, mods, tpg, w, widths):
    n, d = x.shape
    tm = min(TOKEN_TILE, n)
    sh, sc, _ = mods
    return pl.pallas_call(
        functools.partial(_inproj_kernel, widths=widths),
        grid=(n // tm,),
        in_specs=[pl.BlockSpec((tm, d), lambda i: (i, 0)), _mod_spec(sh, tpg), _mod_spec(sc, tpg),
                  _const_spec(w.shape)],
        out_specs=[pl.BlockSpec((tm, wd), lambda i: (i, 0)) for wd in widths],
        out_shape=[jax.ShapeDtypeStruct((n, wd), F32) for wd in widths],
        compiler_params=_params("parallel"),
        name="in_proj",
    )(x, sh, sc, w)


def _outproj_kernel(x_ref, gt_ref, *refs, n_branch):
    att_refs = refs[:2 * n_branch] if n_branch > 1 else refs[:1]
    y_ref, w_ref, g_ref, b_ref, o_ref = refs[len(att_refs):]
    if n_branch > 1:
        outs = [r[...] for r in att_refs[:n_branch]]
        lses = [r[...] for r in att_refs[n_branch:]]
        m = functools.reduce(jnp.maximum, lses)
        ws = [jnp.exp(l - m) for l in lses]
        den = functools.reduce(lambda a, b: a + b, ws)
        att = functools.reduce(lambda a, b: a + b, [w * o for w, o in zip(ws, outs)]) / den
    else:
        att = att_refs[0][...]
    half = att.shape[1]
    mix = _dot(att.astype(BF16), w_ref[:half, :]) + _dot(y_ref[...].astype(BF16), w_ref[half:, :])
    y = ALPHA * x_ref[...] + (1.0 + gt_ref[0]) * mix
    o_ref[...] = _layer_norm(y, g_ref[...], b_ref[...])


def _outproj(x, gate, tpg, atts, y, w, ln_g, ln_b):
    n, d = x.shape
    tm = min(TOKEN_TILE, n)
    half = y.shape[1]
    n_branch = len(atts) // 2 if len(atts) > 1 else 1
    tok = lambda wd: pl.BlockSpec((tm, wd), lambda i: (i, 0))
    return pl.pallas_call(
        functools.partial(_outproj_kernel, n_branch=n_branch),
        grid=(n // tm,),
        in_specs=[tok(d), _mod_spec(gate, tpg)] + [tok(half)] * len(atts) + [tok(half), _const_spec(w.shape),
                                                                           _const_spec((1, d)), _const_spec((1, d))],
        out_specs=tok(d),
        out_shape=jax.ShapeDtypeStruct((n, d), F32),
        compiler_params=_params("parallel"),
        name="out_proj_ln",
    )(x, gate, *atts, y, w, ln_g.reshape(1, d), ln_b.reshape(1, d))


def _diff_lambda(dl_ref, lam_init):
    lp = dl_ref[...]
    a = jnp.sum(lp[0:1] * lp[1:2], axis=-1, keepdims=True)
    b = jnp.sum(lp[2:3] * lp[3:4], axis=-1, keepdims=True)
    return jnp.exp(a) - jnp.exp(b) + lam_init


def _head_rms(o, g, lam_init):
    return o * lax.rsqrt(jnp.mean(o * o, -1, keepdims=True) + LN_EPS) * g * (1.0 - lam_init)


def _diff_prompt_kernel(dl_ref, q_ref, k_ref, v_ref, g_ref, o_ref, *, tq, lam_init):
    i = pl.program_id(2)
    lam = _diff_lambda(dl_ref, lam_init)
    q = q_ref[...] * (DH_A ** -0.5)
    lane = lax.broadcasted_iota(jnp.int32, (1, 2 * DH_A), 1)
    lo = lane < DH_A
    q1 = jnp.where(lo, q, 0.0).astype(BF16)
    q2 = jnp.where(lo, 0.0, q).astype(BF16)
    qpos = i * tq + lax.broadcasted_iota(jnp.int32, (tq, 1), 0)

    def update(s, vb, m, l, acc):
        m_new = jnp.maximum(m, jnp.max(s, -1, keepdims=True))
        p = jnp.exp(s - m_new)
        corr = jnp.exp(m - m_new)
        return m_new, corr * l + jnp.sum(p, -1, keepdims=True), corr * acc + _dot(p.astype(BF16), vb)

    def body(j, carry):
        m1, l1, a1, m2, l2, a2 = carry
        start = pl.multiple_of(j * tq, tq)
        kb = k_ref[pl.ds(start, tq), :].astype(BF16)
        vb = v_ref[pl.ds(start, tq), :].astype(BF16)
        kpos = j * tq + lax.broadcasted_iota(jnp.int32, (1, tq), 1)
        ok = kpos <= qpos
        s1 = jnp.where(ok, _dot_nt(q1, kb), NEG)
        s2 = jnp.where(ok, _dot_nt(q2, kb), NEG)
        m1, l1, a1 = update(s1, vb, m1, l1, a1)
        m2, l2, a2 = update(s2, vb, m2, l2, a2)
        return m1, l1, a1, m2, l2, a2

    mz = jnp.full((tq, 1), NEG, F32)
    lz = jnp.zeros((tq, 1), F32)
    az = jnp.zeros((tq, 2 * DH_A), F32)
    _, l1, a1, _, l2, a2 = lax.fori_loop(0, i + 1, body, (mz, lz, az, mz, lz, az))
    o = a1 / l1 - lam * (a2 / l2)
    o_ref[...] = _head_rms(o, g_ref[...], lam_init)


def _diff_prompt(q, k, v, dl, subln, b, t, lam_init):
    n, w = q.shape
    hw = 2 * DH_A
    tq = min(256, t)
    nq = t // tq
    return pl.pallas_call(
        functools.partial(_diff_prompt_kernel, tq=tq, lam_init=lam_init),
        grid=(b, w // hw, nq),
        in_specs=[pl.BlockSpec(dl.shape, lambda bi, h, i: (0, 0)),
                  pl.BlockSpec((tq, hw), lambda bi, h, i: (bi * nq + i, h)),
                  pl.BlockSpec((t, hw), lambda bi, h, i: (bi, h)),
                  pl.BlockSpec((t, hw), lambda bi, h, i: (bi, h)),
                  pl.BlockSpec((1, hw), lambda bi, h, i: (0, 0))],
        out_specs=pl.BlockSpec((tq, hw), lambda bi, h, i: (bi * nq + i, h)),
        out_shape=jax.ShapeDtypeStruct((n, w), F32),
        compiler_params=_params("parallel", "parallel", "arbitrary"),
        name="diff_attn_prompt",
    )(dl, q, k, v, subln.reshape(1, hw))


def _block_diag_queries(q8, groups, group_width):
    rows = groups * SUBLANES
    row_g = lax.broadcasted_iota(jnp.int32, (rows, 1), 0) // SUBLANES
    lane_g = lax.broadcasted_iota(jnp.int32, (1, q8.shape[1]), 1) // group_width
    return jnp.where(lane_g == row_g, jnp.concatenate([q8] * groups, axis=0), 0.0).astype(BF16)


def _diff_sample_kernel(pt_ref, dl_ref, q_ref, kn_ref, vn_ref, g_ref, *refs, n_pages, lam_init):
    del pt_ref
    kt_refs, v_refs, o_ref = refs[:n_pages], refs[n_pages:2 * n_pages], refs[2 * n_pages]
    width = q_ref.shape[2]
    groups = width // DH_A
    hw = 2 * DH_A
    qbd = _block_diag_queries(q_ref[0] * (DH_A ** -0.5), groups, DH_A)
    row_t = lax.broadcasted_iota(jnp.int32, (groups * SUBLANES, 1), 0) % SUBLANES
    col = lax.broadcasted_iota(jnp.int32, (1, SUBLANES), 1)
    s_pages = [_dot(qbd, kt[0].astype(BF16)) for kt in kt_refs]
    s_new = jnp.where(col <= row_t, _dot_nt(qbd, kn_ref[0].astype(BF16)), NEG)
    m = jnp.max(s_new, -1, keepdims=True)
    for s in s_pages:
        m = jnp.maximum(m, jnp.max(s, -1, keepdims=True))
    p_new = jnp.exp(s_new - m)
    den = jnp.sum(p_new, -1, keepdims=True)
    p_pages = []
    for s in s_pages:
        pr = jnp.exp(s - m)
        den = den + jnp.sum(pr, -1, keepdims=True)
        p_pages.append(pr.astype(BF16))
    p_new = p_new.astype(BF16)
    lam = _diff_lambda(dl_ref, lam_init)
    vn = vn_ref[0].astype(BF16)
    outs = []
    for h in range(width // hw):
        r0 = 2 * h * SUBLANES
        acc = _dot(p_new[r0:r0 + 2 * SUBLANES], vn[:, h * hw:(h + 1) * hw])
        for pr, v_ref in zip(p_pages, v_refs):
            acc = acc + _dot(pr[r0:r0 + 2 * SUBLANES], v_ref[0, :, h, :].astype(BF16))
        a = acc / den[r0:r0 + 2 * SUBLANES]
        outs.append(_head_rms(a[:SUBLANES] - lam * a[SUBLANES:], g_ref[...], lam_init))
    o_ref[0] = jnp.concatenate(outs, axis=-1)


def _diff_sample(q8, kn8, vn8, pool_kt, pool_v, page_table, dl, subln, lam_init):
    bs, _, w = q8.shape
    n_pages = page_table.shape[1]
    tok = pl.BlockSpec((1, SUBLANES, w), lambda b, pt: (b, 0, 0))
    kspecs = [pl.BlockSpec((1,) + pool_kt.shape[1:], lambda b, pt, j=j: (pt[b, j], 0, 0)) for j in range(n_pages)]
    vspecs = [pl.BlockSpec((1,) + pool_v.shape[1:], lambda b, pt, j=j: (pt[b, j], 0, 0, 0)) for j in range(n_pages)]
    return pl.pallas_call(
        functools.partial(_diff_sample_kernel, n_pages=n_pages, lam_init=lam_init),
        grid_spec=pltpu.PrefetchScalarGridSpec(
            num_scalar_prefetch=1,
            grid=(bs,),
            in_specs=[pl.BlockSpec(dl.shape, lambda b, pt: (0, 0)), tok, tok, tok,
                      pl.BlockSpec((1, 2 * DH_A), lambda b, pt: (0, 0))] + kspecs + vspecs,
            out_specs=tok),
        out_shape=jax.ShapeDtypeStruct((bs, SUBLANES, w), F32),
        compiler_params=_params("parallel"),
        name="diff_attn_sample",
    )(page_table, dl, q8, kn8, vn8, subln.reshape(1, 2 * DH_A), *([pool_kt] * n_pages), *([pool_v] * n_pages))


def _s5_prep_kernel(are_ref, aim_ref, ldt_ref, bre_ref, bim_ref, lr_ref, li_ref, bbr_ref, bbi_ref):
    a_re, a_im = are_ref[...], aim_ref[...]
    dt = jnp.exp(ldt_ref[...])
    mag = jnp.exp(a_re * dt)
    lam_re, lam_im = mag * jnp.cos(a_im * dt), mag * jnp.sin(a_im * dt)
    den = a_re * a_re + a_im * a_im
    nr = lam_re - 1.0
    f_re = (nr * a_re + lam_im * a_im) / den
    f_im = (lam_im * a_re - nr * a_im) / den
    lr_ref[...] = lam_re
    li_ref[...] = lam_im
    for g in range(a_re.shape[0]):
        fr, fi = f_re[g:g + 1, :], f_im[g:g + 1, :]
        br, bi = bre_ref[g], bim_ref[g]
        bbr_ref[g] = fr * br - fi * bi
        bbi_ref[g] = fr * bi + fi * br


def _s5_prep(a_re, a_im, log_dt, b_re, b_im):
    g, p = a_re.shape
    c = b_re.shape[-1]
    bt = lambda b: jnp.transpose(b, (0, 2, 1))
    sd = jax.ShapeDtypeStruct
    return pl.pallas_call(
        _s5_prep_kernel,
        out_shape=[sd((g, p), F32), sd((g, p), F32), sd((g, c, p), F32), sd((g, c, p), F32)],
        name="s5_prep",
    )(a_re, a_im, log_dt.reshape(g, 1), bt(b_re), bt(b_im))


def _s5_tail(u, hr, hi, ccr, cci, d, gw, gb):
    y = _dot(hr.astype(BF16), ccr) - _dot(hi.astype(BF16), cci) + d * u
    z = jax.nn.gelu(y)
    return z * _sigmoid(_dot(z.astype(BF16), gw) + gb)


def _s5_prompt_kernel(u_ref, bbr_ref, bbi_ref, lr_ref, li_ref, h0r_ref, h0i_ref, ccr_ref, cci_ref,
                      d_ref, gw_ref, gb_ref, y_ref, hr_out, hi_out, xr, xi, cr, ci):
    i = pl.program_id(1)
    tc = u_ref.shape[0]

    @pl.when(i == 0)
    def _():
        cr[...] = h0r_ref[0]
        ci[...] = h0i_ref[0]

    u = u_ref[...]
    ub = u.astype(BF16)
    xr[...] = _dot(ub, bbr_ref[...])
    xi[...] = _dot(ub, bbi_ref[...])
    lr, li = lr_ref[...], li_ref[...]

    def step(t, carry):
        hr, hi = carry
        nhr = lr * hr - li * hi + xr[pl.ds(t, 1), :]
        nhi = lr * hi + li * hr + xi[pl.ds(t, 1), :]
        xr[pl.ds(t, 1), :] = nhr
        xi[pl.ds(t, 1), :] = nhi
        return nhr, nhi

    hr, hi = lax.fori_loop(0, tc, step, (cr[...], ci[...]), unroll=8)
    cr[...] = hr
    ci[...] = hi
    y_ref[...] = _s5_tail(u, xr[...], xi[...], ccr_ref[...], cci_ref[...], d_ref[...], gw_ref[...], gb_ref[...])

    @pl.when(i == pl.num_programs(1) - 1)
    def _():
        hr_out[0] = hr
        hi_out[0] = hi


def _s5_prompt(u, b, t, h0r, h0i, prm):
    bbr, bbi, lr, li, ccr, cci, d, gw, gb = prm
    n, ch = u.shape
    ns = lr.shape[1]
    tc = min(512, t)
    nt = t // tc
    st = pl.BlockSpec((1, 1, ns), lambda bi, i: (bi, 0, 0))
    sd = jax.ShapeDtypeStruct
    return pl.pallas_call(
        _s5_prompt_kernel,
        grid=(b, nt),
        in_specs=[pl.BlockSpec((tc, ch), lambda bi, i: (bi * nt + i, 0)),
                  _const_spec(bbr.shape), _const_spec(bbi.shape), _const_spec(lr.shape), _const_spec(li.shape),
                  st, st, _const_spec(ccr.shape), _const_spec(cci.shape), _const_spec(d.shape),
                  _const_spec(gw.shape), _const_spec(gb.shape)],
        out_specs=[pl.BlockSpec((tc, ch), lambda bi, i: (bi * nt + i, 0)), st, st],
        out_shape=[sd((n, ch), F32), sd((b, 1, ns), F32), sd((b, 1, ns), F32)],
        scratch_shapes=[pltpu.VMEM((tc, ns), F32), pltpu.VMEM((tc, ns), F32),
                        pltpu.VMEM((1, ns), F32), pltpu.VMEM((1, ns), F32)],
        compiler_params=_params("parallel", "arbitrary"),
        name="s5_prompt",
    )(u, bbr, bbi, lr, li, h0r, h0i, ccr, cci, d, gw, gb)


def _s5_sample_kernel(u_ref, bbr_ref, bbi_ref, lr_ref, li_ref, h0r_ref, h0i_ref, ccr_ref, cci_ref,
                      d_ref, gw_ref, gb_ref, y_ref, hr_out, hi_out):
    lr, li = lr_ref[...], li_ref[...]
    hr, hi = h0r_ref[...], h0i_ref[...]
    for t in range(u_ref.shape[0]):
        u = u_ref[t]
        ub = u.astype(BF16)
        xr = _dot(ub, bbr_ref[...])
        xi = _dot(ub, bbi_ref[...])
        hr, hi = lr * hr - li * hi + xr, lr * hi + li * hr + xi
        y_ref[t] = _s5_tail(u, hr, hi, ccr_ref[...], cci_ref[...], d_ref[...], gw_ref[...], gb_ref[...])
    hr_out[...] = hr
    hi_out[...] = hi


def _s5_sample(u_tm, h0r, h0i, prm):
    bbr, bbi, lr, li, ccr, cci, d, gw, gb = prm
    sd = jax.ShapeDtypeStruct
    return pl.pallas_call(
        _s5_sample_kernel,
        out_shape=[sd(u_tm.shape, F32), sd(h0r.shape, F32), sd(h0i.shape, F32)],
        compiler_params=pltpu.CompilerParams(vmem_limit_bytes=VMEM_LIMIT_BYTES),
        name="s5_sample",
    )(u_tm, bbr, bbi, lr, li, h0r, h0i, ccr, cci, d, gw, gb)


def _dil_prompt_kernel(q_ref, kp_ref, kc_ref, vp_ref, vc_ref, o_ref, l_ref):
    n = pl.program_id(1)
    blk = q_ref.shape[1]
    q = q_ref[0] * (DH_C ** -0.5)
    kcat = jnp.concatenate([kp_ref[0], kc_ref[0]], axis=0).astype(BF16)
    vcat = jnp.concatenate([vp_ref[0], vc_ref[0]], axis=0).astype(BF16)
    qi = lax.broadcasted_iota(jnp.int32, (blk, 1), 0) + blk
    ki = lax.broadcasted_iota(jnp.int32, (1, 2 * blk), 1)
    dist = qi - ki
    lo_k = jnp.where(n > 0, 0, blk)
    ok = (dist >= 0) & (dist <= blk) & (ki >= lo_k)
    lane = lax.broadcasted_iota(jnp.int32, (1, q.shape[1]), 1)
    first = lane < DH_C
    o = None
    lse = None
    for hh in range(2):
        sel = first if hh == 0 else jnp.logical_not(first)
        qm = jnp.where(sel, q, 0.0).astype(BF16)
        s = jnp.where(ok, _dot_nt(qm, kcat), NEG)
        m = jnp.max(s, -1, keepdims=True)
        pr = jnp.exp(s - m)
        den = jnp.sum(pr, -1, keepdims=True)
        oh = _dot((pr / den).astype(BF16), vcat)
        lh = m + jnp.log(den)
        o = oh if o is None else jnp.where(first, o, oh)
        lse = jnp.broadcast_to(lh, oh.shape) if lse is None else jnp.where(first, lse, lh)
    o_ref[0] = o
    l_ref[0] = lse


def _dil_prompt(qd, kd, vd):
    bd, ns, w = qd.shape
    pw = 2 * DH_C
    cur = pl.BlockSpec((1, C_BLK, pw), lambda b, n, h: (b, n, h))
    prev = pl.BlockSpec((1, C_BLK, pw), lambda b, n, h: (b, jnp.maximum(n - 1, 0), h))
    sd = jax.ShapeDtypeStruct((bd, ns, w), F32)
    return pl.pallas_call(
        _dil_prompt_kernel,
        grid=(bd, ns // C_BLK, w // pw),
        in_specs=[cur, prev, cur, prev, cur],
        out_specs=[cur, cur],
        out_shape=[sd, sd],
        compiler_params=_params("parallel", "parallel", "parallel"),
        name="dilated_attn_prompt",
    )(qd, kd, kd, vd, vd)


def _dil_sample_kernel(q_ref, kn_ref, vn_ref, kt_ref, vt_ref, o_ref):
    w = q_ref.shape[2]
    nh = w // DH_C
    buf = kt_ref.shape[2]
    rows = nh * SUBLANES
    qbd = _block_diag_queries(q_ref[0] * (DH_C ** -0.5), nh, DH_C)
    row_q = lax.broadcasted_iota(jnp.int32, (rows, 1), 0) % SUBLANES

    def reach(delta):
        mult = jnp.zeros(delta.shape, F32)
        for d in C_DILATIONS:
            ok = (delta >= 0) & (lax.rem(delta, d) == 0) & (delta <= d * C_BLK)
            mult = mult + jnp.where(ok, 1.0, 0.0)
        return mult

    mult = reach(buf + row_q - lax.broadcasted_iota(jnp.int32, (1, buf), 1))
    mult_n = reach(row_q - lax.broadcasted_iota(jnp.int32, (1, SUBLANES), 1))
    s = jnp.where(mult > 0, _dot(qbd, kt_ref[0].astype(BF16)), NEG)
    s_n = jnp.where(mult_n > 0, _dot_nt(qbd, kn_ref[0].astype(BF16)), NEG)
    m = jnp.maximum(jnp.max(s, -1, keepdims=True), jnp.max(s_n, -1, keepdims=True))
    p = mult * jnp.exp(s - m)
    p_n = mult_n * jnp.exp(s_n - m)
    inv = 1.0 / (jnp.sum(p, -1, keepdims=True) + jnp.sum(p_n, -1, keepdims=True))
    o = _dot_nt((p * inv).astype(BF16), vt_ref[0].astype(BF16)) + _dot((p_n * inv).astype(BF16), vn_ref[0].astype(BF16))
    lane_h = lax.broadcasted_iota(jnp.int32, (1, w), 1) // DH_C
    out = jnp.zeros((SUBLANES, w), F32)
    for h in range(nh):
        out = out + jnp.where(lane_h == h, o[h * SUBLANES:(h + 1) * SUBLANES, :], 0.0)
    o_ref[0] = out


def _dil_sample(q8, kn8, vn8, cache_kt, cache_vt):
    bs, w, buf = cache_kt.shape
    tok = pl.BlockSpec((1, SUBLANES, w), lambda b: (b, 0, 0))
    cache = pl.BlockSpec((1, w, buf), lambda b: (b, 0, 0))
    return pl.pallas_call(
        _dil_sample_kernel,
        grid=(bs,),
        in_specs=[tok, tok, tok, cache, cache],
        out_specs=tok,
        out_shape=jax.ShapeDtypeStruct((bs, SUBLANES, w), F32),
        compiler_params=_params("parallel"),
        name="dilated_attn_sample",
    )(q8, kn8, vn8, cache_kt, cache_vt)


def _softplus(x):
    return jnp.maximum(x, 0.0) + jnp.log1p(jnp.exp(-jnp.abs(x)))


def _rwkv_pre_kernel(pd_ref, pv_ref, mu_ref, w0_ref, w2_ref, a0_ref, a2_ref, g2_ref, kk_ref, ka_ref, rk_ref, seg_ref,
                     r_o, ld_o, k_o, v_o, kk_o, b_o, g_o, bonus_o):
    pd = pd_ref[...]
    xm = pd + (pv_ref[...] - pd) * mu_ref[...]
    o1, o2, o3 = D_W, 2 * D_W, 3 * D_W
    o5 = o3 + D_LORA_W + D_LORA_A
    r, k, v = xm[:, :o1], xm[:, o1:o2], xm[:, o2:o3]
    wa, gl = xm[:, o3:o5], xm[:, o5:]
    lw = _dot(jnp.tanh(wa).astype(BF16), w2_ref[...])
    la = _dot(wa.astype(BF16), a2_ref[...])
    g = _dot(_sigmoid(gl).astype(BF16), g2_ref[...])
    w_log = -_softplus(-(w0_ref[...] + lw)) - 0.5
    a = _sigmoid(a0_ref[...] + la)
    seg = seg_ref[...]
    kk = k * kk_ref[...]
    kk = kk / jnp.maximum(jnp.sqrt(_dot_exact_rhs(kk * kk, seg)), 1e-12)
    k2 = k * (1.0 + (a - 1.0) * ka_ref[...])
    r_o[...] = r
    ld_o[...] = -jnp.exp(w_log)
    k_o[...] = k2
    v_o[...] = v
    kk_o[...] = kk
    b_o[...] = kk * a
    g_o[...] = g
    bonus_o[...] = _dot_exact_rhs(r * k2 * rk_ref[...], seg) * v


def _rwkv_pre(pd, prev, prm):
    n, cols = pd.shape
    tm = min(TOKEN_TILE, n)
    tok = lambda wd: pl.BlockSpec((tm, wd), lambda i: (i, 0))
    return pl.pallas_call(
        _rwkv_pre_kernel,
        grid=(n // tm,),
        in_specs=[tok(cols), tok(cols)] + [_const_spec(p.shape) for p in prm],
        out_specs=[tok(D_W)] * 8,
        out_shape=[jax.ShapeDtypeStruct((n, D_W), F32)] * 8,
        compiler_params=_params("parallel"),
        name="rwkv_pre",
    )(pd, prev, *prm)


def _rwkv_chunk_kernel(r_ref, ld_ref, k_ref, v_ref, kk_ref, b_ref, y_ref, s_out, st):
    c = pl.program_id(1)
    ch = r_ref.shape[0]
    pw = 2 * N_D

    @pl.when(c == 0)
    def _():
        st[...] = jnp.zeros(st.shape, F32)

    ri = lax.broadcasted_iota(jnp.int32, (ch, ch), 0)
    ci = lax.broadcasted_iota(jnp.int32, (ch, ch), 1)
    tri_incl = ci <= ri
    tri_strict = ci < ri
    eye_c = jnp.where(ci == ri, 1.0, 0.0)
    ld = ld_ref[...]
    cum = _dot_exact_rhs_left(jnp.where(tri_incl, 1.0, 0.0).astype(BF16), ld)
    cum_end = cum[ch - 1:ch, :]
    g_inc = jnp.exp(cum)
    g_inv = jnp.exp(-cum)
    g_end = jnp.exp(cum_end - cum)
    rho = r_ref[...] * g_inc
    kap = kk_ref[...] * jnp.exp(cum - ld)
    kh = k_ref[...] * g_inv
    bh = b_ref[...] * g_inv
    khg = k_ref[...] * g_end
    bhg = b_ref[...] * g_end
    gam_end = jnp.exp(cum_end)
    v = v_ref[...]

    pi = lax.broadcasted_iota(jnp.int32, (pw, pw), 0)
    pj = lax.broadcasted_iota(jnp.int32, (pw, pw), 1)
    same_head = (pi // N_D) == (pj // N_D)
    lane = lax.broadcasted_iota(jnp.int32, (1, pw), 1)
    first = lane < N_D

    npair = r_ref.shape[1] // pw
    heads = [(pr, hh) for pr in range(npair) for hh in range(2)]
    sls = [slice(pr * pw, (pr + 1) * pw) for pr in range(npair)]
    sel = (first, jnp.logical_not(first))
    bf = lambda x: x.astype(BF16)
    kap_p = [kap[:, s] for s in sls]
    rho_p = [rho[:, s] for s in sls]
    kap_b = [bf(x) for x in kap_p]
    kh_b = [bf(kh[:, s]) for s in sls]
    bh_b = [bf(bh[:, s]) for s in sls]
    v_b = [bf(v[:, s]) for s in sls]
    kap_m = [bf(jnp.where(sel[hh], kap_p[pr], 0.0)) for pr, hh in heads]
    rho_m = [bf(jnp.where(sel[hh], rho_p[pr], 0.0)) for pr, hh in heads]
    a_b = [jnp.where(tri_strict, _dot_nt(kap_m[i], bh_b[pr]), 0.0) for i, (pr, _) in enumerate(heads)]
    a_k = [bf(jnp.where(tri_strict, _dot_nt(kap_m[i], kh_b[pr]), 0.0)) for i, (pr, _) in enumerate(heads)]
    ap_b = [bf(jnp.where(tri_incl, _dot_nt(rho_m[i], bh_b[pr]), 0.0)) for i, (pr, _) in enumerate(heads)]
    ap_k = [bf(jnp.where(tri_incl, _dot_nt(rho_m[i], kh_b[pr]), 0.0)) for i, (pr, _) in enumerate(heads)]
    tl = [eye_c - a for a in a_b]
    pw2 = [_dot(bf(a), bf(a)) for a in a_b]
    span = 2
    while span < ch:
        tl = [t + _dot(bf(t), bf(p2)) for t, p2 in zip(tl, pw2)]
        span *= 2
        if span < ch:
            pw2 = [_dot(bf(p2), bf(p2)) for p2 in pw2]
    tl_b = [bf(t) for t in tl]
    kap2_h = [_dot(tl_b[i], kap_b[pr]) for i, (pr, _) in enumerate(heads)]
    akv = [bf(_dot(a_k[i], v_b[pr])) for i, (pr, _) in enumerate(heads)]
    wr_h = [_dot(tl_b[i], akv[i]) for i in range(len(heads))]
    rho2_h = [rho_p[pr] - _dot(ap_b[i], bf(kap2_h[i])) for i, (pr, _) in enumerate(heads)]
    yloc_h = [_dot(ap_k[i], v_b[pr]) - _dot(ap_b[i], bf(wr_h[i])) for i, (pr, _) in enumerate(heads)]
    pair = lambda xs: [jnp.where(first, xs[2 * pr], xs[2 * pr + 1]) for pr in range(npair)]
    kap2, wr, rho2, yloc = pair(kap2_h), pair(wr_h), pair(rho2_h), pair(yloc_h)
    bhg_b = [bf(bhg[:, s]) for s in sls]
    khg_b = [bf(khg[:, s]) for s in sls]
    diag = [jnp.where(pi == pj, jnp.broadcast_to(gam_end[:, s], (pw, pw)), 0.0) for s in sls]
    phi = [jnp.where(same_head, diag[pr] - _dot_tn(bhg_b[pr], bf(kap2[pr])), 0.0) for pr in range(npair)]
    gmat = [jnp.where(same_head, _dot_tn(khg_b[pr], v_b[pr]) - _dot_tn(bhg_b[pr], bf(wr[pr])), 0.0)
            for pr in range(npair)]
    s_b = [bf(st[pr]) for pr in range(npair)]
    for pr in range(npair):
        y_ref[:, sls[pr]] = _dot(bf(rho2[pr]), s_b[pr]) + yloc[pr]
    for pr in range(npair):
        st[pr] = _dot(bf(phi[pr]), s_b[pr]) + gmat[pr]

    @pl.when(c == pl.num_programs(1) - 1)
    def _():
        s_out[0] = st[...]


def _dot_exact_rhs_left(m01, x):
    hi, mid, lo = _split3(x)
    return _dot(m01, hi) + _dot(m01, mid) + _dot(m01, lo)


def _rwkv_chunk(r, ld, k2, v, kk, bb, b, t):
    n, w = r.shape
    ch = min(RWKV_CHUNK, t)
    nc = t // ch
    npair = w // (2 * N_D)
    tok = pl.BlockSpec((ch, w), lambda bi, c: (bi * nc + c, 0))
    return pl.pallas_call(
        _rwkv_chunk_kernel,
        grid=(b, nc),
        in_specs=[tok] * 6,
        out_specs=[tok, pl.BlockSpec((1, npair, 2 * N_D, 2 * N_D), lambda bi, c: (bi, 0, 0, 0))],
        out_shape=[jax.ShapeDtypeStruct((n, w), F32), jax.ShapeDtypeStruct((b, npair, 2 * N_D, 2 * N_D), F32)],
        scratch_shapes=[pltpu.VMEM((npair, 2 * N_D, 2 * N_D), F32)],
        compiler_params=_params("parallel", "arbitrary"),
        name="rwkv_chunk_scan",
    )(r, ld, k2, v, kk, bb)


def _rwkv_lane_kernel(r_ref, ld_ref, k_ref, v_ref, kk_ref, b_ref, s_ref, y_ref, s_out):
    steps = r_ref.shape[0]

    def body(vi, carry):
        s = s_ref[0, vi]
        for t in range(steps):
            sk = jnp.sum(s * kk_ref[t, 0], axis=0, keepdims=True)
            vv = v_ref[t, 0, pl.ds(vi, 1), :]
            s = s * jnp.exp(ld_ref[t, 0]) - sk * b_ref[t, 0] + vv * k_ref[t, 0]
            y_ref[t, 0, pl.ds(vi, 1), :] = jnp.sum(s * r_ref[t, 0], axis=0, keepdims=True)
        s_out[0, vi] = s
        return carry

    lax.fori_loop(0, s_ref.shape[1], body, 0)


def _rwkv_lane(rt, ldt, kt, vt, kkt, bt, s0):
    steps, nh, nd, bs = rt.shape
    tok = pl.BlockSpec((steps, 1, nd, bs), lambda h: (0, h, 0, 0))
    stt = pl.BlockSpec((1, nd, nd, bs), lambda h: (h, 0, 0, 0))
    return pl.pallas_call(
        _rwkv_lane_kernel,
        grid=(nh,),
        in_specs=[tok] * 6 + [stt],
        out_specs=[tok, stt],
        out_shape=[jax.ShapeDtypeStruct(rt.shape, F32), jax.ShapeDtypeStruct(s0.shape, F32)],
        compiler_params=_params("parallel"),
        name="rwkv_lane_scan",
    )(rt, ldt, kt, vt, kkt, bt, s0)


def _rwkv_post_kernel(y_ref, bonus_ref, g_ref, gw_ref, gb_ref, seg_ref, o_ref):
    y = y_ref[...]
    seg = seg_ref[...]
    mu = _dot_exact_rhs(y, seg) * (1.0 / N_D)
    yc = y - mu
    var = _dot_exact_rhs(yc * yc, seg) * (1.0 / N_D)
    yn = yc * lax.rsqrt(var + GN_EPS) * gw_ref[...] + gb_ref[...]
    o_ref[...] = (yn + bonus_ref[...]) * g_ref[...]


def _rwkv_post(y, bonus, g, gn_w, gn_b, seg):
    n, w = y.shape
    tm = min(TOKEN_TILE, n)
    tok = pl.BlockSpec((tm, w), lambda i: (i, 0))
    return pl.pallas_call(
        _rwkv_post_kernel,
        grid=(n // tm,),
        in_specs=[tok, tok, tok, _const_spec((1, w)), _const_spec((1, w)), _const_spec(seg.shape)],
        out_specs=tok,
        out_shape=jax.ShapeDtypeStruct((n, w), F32),
        compiler_params=_params("parallel"),
        name="rwkv_post",
    )(y, bonus, g, gn_w.reshape(1, w), gn_b.reshape(1, w), seg)


def _pad_tokens(x, bs, s_len):
    x = x.reshape(bs, s_len, x.shape[-1])
    return jnp.pad(x, ((0, 0), (0, SUBLANES - s_len), (0, 0)))


def _deinterleave(x, b, t, d):
    return x.reshape(b, t // d, d, x.shape[-1]).transpose(0, 2, 1, 3).reshape(b * d, t // d, x.shape[-1])


def _interleave(x, b, t, d):
    return x.reshape(b, d, t // d, x.shape[-1]).transpose(0, 2, 1, 3).reshape(b * t, x.shape[-1])


def _block_diag_in(bb):
    g, c, p = bb.shape
    return jnp.einsum('gcp,gh->gchp', bb, jnp.eye(g, dtype=bb.dtype)).reshape(g * c, g * p)


def _block_diag_out(cc):
    g, c, p = cc.shape
    return jnp.einsum('gcp,gh->gphc', cc, jnp.eye(g, dtype=cc.dtype)).reshape(g * p, g * c)


def _shifted(pd, shift0, b, t):
    pd3 = pd.reshape(b, t, pd.shape[-1])
    return jnp.concatenate([shift0[:, None, :], pd3[:, :-1]], axis=1).reshape(b * t, pd.shape[-1])


def _trunk(x, mod_all, row0, rows, per_token_mod, sample, st, p):
    b, t, d = x.shape
    n = b * t
    xt = x.reshape(n, d)
    tm = min(TOKEN_TILE, n)
    outs = {}
    for l in range(DEPTH):
        mod = mod_all[l, row0:row0 + rows].reshape(rows, 9, d)

        def mods(i):
            sel = [mod[:, 3 * i + j] for j in range(3)]
            if per_token_mod:
                return [jnp.repeat(m, t, axis=0).reshape(n // tm, tm, d) for m in sel], 1
            return [m.reshape(b, 1, d) for m in sel], t // tm

        m0, tpg = mods(0)
        xt = _ffn(xt, m0, tpg, p['wg'][l][0], p['wu'][l][0], p['wd'][l][0], p['ln_g'][l, 0], p['ln_b'][l, 0])
        m1, _ = mods(1)
        if l % 2 == 0:
            e = l // 2
            lam_init = 0.8 - 0.6 * math.exp(-0.3 * l)
            q, k, v, u = _inproj(xt, m1, tpg, p['even_w_in'][e], (512, 512, 512, S5_CH))
            dl, subln = p['diff_lambda'][e], p['diff_subln'][e]
            if sample:
                att8 = _diff_sample(_pad_tokens(q, b, t), _pad_tokens(k, b, t), _pad_tokens(v, b, t),
                                    st['pool_k'][e], st['pool_v'][e], st['page_table'], dl, subln, lam_init)
                att = att8[:, :t].reshape(n, -1)
                u_tm = u.reshape(b, t, -1).transpose(1, 0, 2)
                y_tm, hr, hi = _s5_sample(u_tm, st['s5_re'][e].reshape(b, S5_N), st['s5_im'][e].reshape(b, S5_N),
                                          p['s5'][e])
                y5 = y_tm.transpose(1, 0, 2).reshape(n, -1)
            else:
                att = _diff_prompt(q, k, v, dl, subln, b, t, lam_init)
                zero = jnp.zeros((b, 1, S5_N), F32)
                y5, hr, hi = _s5_prompt(u, b, t, zero, zero, p['s5'][e])
            outs.setdefault('ak', []).append(k.reshape(b, t, H_A, 2, DH_A))
            outs.setdefault('av', []).append(v.reshape(b, t, H_A, 2 * DH_A))
            outs.setdefault('s5r', []).append(hr.reshape(b, S5_G, S5_P))
            outs.setdefault('s5i', []).append(hi.reshape(b, S5_G, S5_P))
            xt = _outproj(xt, m1[2], tpg, [att], y5, p['even_w_out'][e], p['ln_g'][l, 1], p['ln_b'][l, 1])
        else:
            o = l // 2
            q, k, v, pd = _inproj(xt, m1, tpg, p['odd_w_in'][o], (512, 512, 512, pd_cols(p)))
            rp = p['rwkv'][o]
            if sample:
                att8 = _dil_sample(_pad_tokens(q, b, t), _pad_tokens(k, b, t), _pad_tokens(v, b, t),
                                   st['cache_c_k'][o], st['cache_c_v'][o])
                atts = [att8[:, :t].reshape(n, -1)]
                prev = _shifted(pd, st['d_shift'][o], b, t)
                r, ld, k2, vv, kk, bb, g, bonus = _rwkv_pre(pd, prev, rp['pre'])
                tl = lambda z: z.reshape(b, t, H_D, N_D).transpose(1, 2, 3, 0)
                s0 = st['d_wkv'][o].transpose(1, 2, 3, 0)
                y_l, s_new = _rwkv_lane(tl(r), tl(ld), tl(k2), tl(vv), tl(kk), tl(bb), s0)
                y = y_l.transpose(3, 0, 1, 2).reshape(n, D_W)
                s_new = s_new.transpose(3, 0, 1, 2)
                k_keep, v_keep = k.reshape(b, t, H_C, DH_C), v.reshape(b, t, H_C, DH_C)
            else:
                atts_o, atts_l = [], []
                for dil in C_DILATIONS:
                    if dil == 1:
                        ob, lb = _dil_prompt(q.reshape(b, t, -1), k.reshape(b, t, -1), v.reshape(b, t, -1))
                        ob, lb = ob.reshape(n, -1), lb.reshape(n, -1)
                    else:
                        ob, lb = _dil_prompt(_deinterleave(q, b, t, dil), _deinterleave(k, b, t, dil),
                                             _deinterleave(v, b, t, dil))
                        ob, lb = _interleave(ob, b, t, dil), _interleave(lb, b, t, dil)
                    atts_o.append(ob)
                    atts_l.append(lb)
                atts = atts_o + atts_l
                prev = _shifted(pd, jnp.zeros((b, pd.shape[-1]), F32), b, t)
                r, ld, k2, vv, kk, bb, g, bonus = _rwkv_pre(pd, prev, rp['pre'])
                y, s_pairs = _rwkv_chunk(r, ld, k2, vv, kk, bb, b, t)
                sp = s_pairs.reshape(b, H_D // 2, 2, N_D, 2, N_D)
                s_new = jnp.stack([sp[:, :, 0, :, 0, :], sp[:, :, 1, :, 1, :]], axis=2)
                s_new = s_new.reshape(b, H_D, N_D, N_D).transpose(0, 1, 3, 2)
                keep = min(C_BLK * C_DILATIONS[-1], t)
                k_keep = k.reshape(b, t, H_C, DH_C)[:, t - keep:]
                v_keep = v.reshape(b, t, H_C, DH_C)[:, t - keep:]
            yd = _rwkv_post(y, bonus, g, rp['gn_w'], rp['gn_b'], rp['seg'])
            outs.setdefault('ck', []).append(k_keep)
            outs.setdefault('cv', []).append(v_keep)
            outs.setdefault('dw', []).append(s_new)
            outs.setdefault('ds', []).append(pd.reshape(b, t, -1)[:, -1])
            xt = _outproj(xt, m1[2], tpg, atts, yd, p['odd_w_out'][o], p['ln_g'][l, 1], p['ln_b'][l, 1])
        m2, _ = mods(2)
        xt = _ffn(xt, m2, tpg, p['wg'][l][1], p['wu'][l][1], p['wd'][l][1], p['ln_g'][l, 2], p['ln_b'][l, 2])
    stacked = [jnp.stack(outs[key], 0) for key in ('ak', 'av', 's5r', 's5i', 'ck', 'cv', 'dw', 'ds')]
    return xt.reshape(b, t, d), stacked


def pd_cols(p):
    return p['odd_w_in'].shape[-1] - 3 * H_C * DH_C


def kernel(x_prompt, x_sample, cache_a_k, cache_a_v, state_s5_re, state_s5_im, cache_c_k, cache_c_v, state_d_wkv, state_d_shift, page_table, c_prompt, c_sample, ada_w, ada_b, ln_g, ln_b, ffn_w_gate, ffn_w_up, ffn_w_down, even_w_in, even_w_out, diff_lambda, diff_subln, s5_a_re, s5_a_im, s5_log_dt, s5_b_re, s5_b_im, s5_c_re, s5_c_im, s5_d, s5_glu_w, s5_glu_b, odd_w_in, odd_w_out, rwkv_mu, rwkv_w0, rwkv_w2, rwkv_a0, rwkv_a2, rwkv_g2, rwkv_k_k, rwkv_k_a, rwkv_r_k, rwkv_gn_w, rwkv_gn_b):
    bp, bs = x_prompt.shape[0], x_sample.shape[0]
    n_even, n_odd = even_w_in.shape[0], odd_w_in.shape[0]
    bf = lambda w: w.astype(BF16)

    seg = jnp.kron(jnp.eye(H_D, dtype=F32), jnp.ones((N_D, N_D), F32)).astype(BF16)
    s5 = []
    for e in range(n_even):
        lr, li, bbr, bbi = _s5_prep(s5_a_re[e], s5_a_im[e], s5_log_dt[e], s5_b_re[e], s5_b_im[e])
        s5.append((bf(_block_diag_in(bbr)), bf(_block_diag_in(bbi)), lr.reshape(1, S5_N), li.reshape(1, S5_N),
                   bf(_block_diag_out(s5_c_re[e])), bf(_block_diag_out(s5_c_im[e])),
                   s5_d[e].reshape(1, S5_CH), bf(s5_glu_w[e]), s5_glu_b[e].reshape(1, S5_CH)))
    rwkv = []
    for o in range(n_odd):
        row = lambda z: z.reshape(1, -1)
        w2p = jnp.concatenate([rwkv_w2[o], jnp.zeros_like(rwkv_a2[o])], axis=0)
        a2p = jnp.concatenate([jnp.zeros_like(rwkv_w2[o]), rwkv_a2[o]], axis=0)
        pre = (row(rwkv_mu[o]), row(rwkv_w0[o]), bf(w2p), row(rwkv_a0[o]), bf(a2p), bf(rwkv_g2[o]),
               row(rwkv_k_k[o]), row(rwkv_k_a[o]), row(rwkv_r_k[o]), seg)
        rwkv.append(dict(pre=pre, gn_w=rwkv_gn_w[o], gn_b=rwkv_gn_b[o], seg=seg))
    p = dict(wg=bf(ffn_w_gate), wu=bf(ffn_w_up), wd=bf(ffn_w_down), ln_g=ln_g, ln_b=ln_b,
             even_w_in=bf(even_w_in), even_w_out=bf(even_w_out), odd_w_in=bf(odd_w_in), odd_w_out=bf(odd_w_out),
             diff_lambda=diff_lambda, diff_subln=diff_subln, s5=s5, rwkv=rwkv)

    mod_all = _ada(jnp.concatenate([c_prompt, c_sample], axis=0), ada_w, ada_b)

    y_prompt, st_p = _trunk(x_prompt, mod_all, 0, bp, False, False, None, p)
    n_pool, page = cache_a_k.shape[1], cache_a_k.shape[2]
    win_buf = cache_c_k.shape[2]
    pos_minor = lambda c: jnp.transpose(c, (0, 1, 3, 4, 2)).reshape(n_odd, bs, -1, win_buf)
    st = dict(pool_k=jnp.transpose(cache_a_k, (0, 1, 3, 4, 5, 2)).reshape(n_even, n_pool, -1, page),
              pool_v=cache_a_v,
              page_table=page_table, s5_re=state_s5_re, s5_im=state_s5_im,
              cache_c_k=pos_minor(cache_c_k), cache_c_v=pos_minor(cache_c_v),
              d_wkv=state_d_wkv, d_shift=state_d_shift)
    y_sample, st_s = _trunk(x_sample, mod_all, bp, bs, True, True, st, p)
    a_k_p, a_v_p, s5_re_p, s5_im_p, c_k_p, c_v_p, d_wkv_p, d_shift_p = st_p
    a_k_s, a_v_s, s5_re_s, s5_im_s, c_k_s, c_v_s, d_wkv_s, d_shift_s = st_s
    return (y_prompt, y_sample, a_k_p, a_k_s, a_v_p, a_v_s, s5_re_p, s5_re_s, s5_im_p, s5_im_s,
            c_k_p, c_k_s, c_v_p, c_v_s, d_wkv_p, d_wkv_s, d_shift_p, d_shift_s)
```

```python
import functools
import math

import jax
import jax.numpy as jnp
from jax import lax
from jax.experimental import pallas as pl
from jax.experimental.pallas import tpu as pltpu

F32 = jnp.float32
BF16 = jnp.bfloat16

DEPTH = 2
H_A, DH_A = 4, 64
A_W = H_A * 2 * DH_A
S5_GROUP, S5_G, S5_P = 16, 32, 64
S5_CH = S5_GROUP * S5_G
S5_N = S5_G * S5_P
H_C, DH_C = 8, 64
C_W = H_C * DH_C
C_BLK = 128
C_DILATIONS = (1, 4, 16)
H_D, N_D = 8, 64
D_W = H_D * N_D
D_LORA_W, D_LORA_A, D_LORA_G = 64, 64, 128
GN_EPS = 64e-5
ALPHA = (2.0 * DEPTH) ** 0.25
LN_EPS = 1e-5
NEG = -1e30

LANES = 128
SUBLANES = 8
VMEM_LIMIT_BYTES = 56 * 1024 * 1024
TOKEN_TILE = 512
RWKV_CHUNK = 64


def _params(*sem):
    return pltpu.CompilerParams(dimension_semantics=sem, vmem_limit_bytes=VMEM_LIMIT_BYTES)


def _const_spec(shape):
    nd = len(shape)
    return pl.BlockSpec(shape, lambda *_: (0,) * nd, pipeline_mode=pl.Buffered(1))


def _dot(a, b):
    return jnp.dot(a, b, preferred_element_type=F32)


def _dot_nt(a, b):
    return lax.dot_general(a, b, (((1,), (1,)), ((), ())), preferred_element_type=F32)


def _dot_tn(a, b):
    return lax.dot_general(a, b, (((0,), (0,)), ((), ())), preferred_element_type=F32)


def _split3(x):
    hi = x.astype(BF16)
    r1 = x - hi.astype(F32)
    mid = r1.astype(BF16)
    lo = (r1 - mid.astype(F32)).astype(BF16)
    return hi, mid, lo


def _dot_exact_rhs(x, m01):
    hi, mid, lo = _split3(x)
    return _dot(hi, m01) + _dot(mid, m01) + _dot(lo, m01)


def _layer_norm(y, g, b):
    mu = jnp.mean(y, -1, keepdims=True)
    yc = y - mu
    var = jnp.mean(yc * yc, -1, keepdims=True)
    return yc * lax.rsqrt(var + LN_EPS) * g + b


def _sigmoid(x):
    return jax.nn.sigmoid(x)


def _mod_spec(mod, tiles_per_group):
    _, rm, d = mod.shape
    return pl.BlockSpec((1, rm, d), lambda i: (i // tiles_per_group, 0, 0))


def _ada_kernel(c_ref, w_ref, b_ref, o_ref):
    c = c_ref[...]
    s = (c * _sigmoid(c)).astype(BF16)
    o_ref[0] = _dot(s, w_ref[0].astype(BF16)) + b_ref[0]


def _ada(c_all, ada_w, ada_b):
    nl, d, w = ada_w.shape
    r = c_all.shape[0]
    tn = 1152 if w % 1152 == 0 else w
    return pl.pallas_call(
        _ada_kernel,
        grid=(nl, w // tn),
        in_specs=[pl.BlockSpec((r, d), lambda l, j: (0, 0)),
                  pl.BlockSpec((1, d, tn), lambda l, j: (l, 0, j)),
                  pl.BlockSpec((1, 1, tn), lambda l, j: (l, 0, j))],
        out_specs=pl.BlockSpec((1, r, tn), lambda l, j: (l, 0, j)),
        out_shape=jax.ShapeDtypeStruct((nl, r, w), F32),
        compiler_params=_params("parallel", "parallel"),
        name="ada_mod",
    )(c_all, ada_w, ada_b.reshape(nl, 1, w))


def _ffn_kernel(x_ref, sh_ref, sc_ref, gt_ref, wg_ref, wu_ref, wd_ref, g_ref, b_ref, o_ref):
    x = x_ref[...]
    h = (x * (1.0 + sc_ref[0]) + sh_ref[0]).astype(BF16)
    g = _dot(h, wg_ref[...])
    u = _dot(h, wu_ref[...])
    a = (g * _sigmoid(g) * u).astype(BF16)
    f = _dot(a, wd_ref[...])
    y = ALPHA * x + 0.5 * (1.0 + gt_ref[0]) * f
    o_ref[...] = _layer_norm(y, g_ref[...], b_ref[...])


def _ffn(x, mods, tpg, wg, wu, wd, ln_g, ln_b):
    n, d = x.shape
    f = wg.shape[1]
    tm = min(TOKEN_TILE, n)
    sh, sc, gt = mods
    return pl.pallas_call(
        _ffn_kernel,
        grid=(n // tm,),
        in_specs=[pl.BlockSpec((tm, d), lambda i: (i, 0)),
                  _mod_spec(sh, tpg), _mod_spec(sc, tpg), _mod_spec(gt, tpg),
                  _const_spec((d, f)), _const_spec((d, f)), _const_spec((f, d)),
                  _const_spec((1, d)), _const_spec((1, d))],
        out_specs=pl.BlockSpec((tm, d), lambda i: (i, 0)),
        out_shape=jax.ShapeDtypeStruct((n, d), F32),
        compiler_params=_params("parallel"),
        name="ffn_ln",
    )(x, sh, sc, gt, wg, wu, wd, ln_g.reshape(1, d), ln_b.reshape(1, d))


def _inproj_kernel(x_ref, sh_ref, sc_ref, w_ref, *refs, widths, t_blocked, dils, n_dil):
    n_t = len(t_blocked)
    n_split = n_dil * len(dils)
    wt_refs, outs = refs[:n_t], refs[n_t:]
    o_refs = outs[:len(widths)]
    dil_refs = outs[len(widths):len(widths) + n_split]
    t_refs = outs[len(widths) + n_split:len(widths) + n_split + n_t]
    tm = x_ref.shape[0]
    h = (x_ref[...] * (1.0 + sc_ref[0]) + sh_ref[0]).astype(BF16)
    p = _dot(h, w_ref[...])
    off = 0
    for o_ref, wd in zip(o_refs, widths):
        o_ref[...] = p[:, off:off + wd]
        off += wd
    if n_split:
        stage = refs[-1]
        off = 0
        for s in range(n_dil):
            for c in range(widths[s] // LANES):
                stage[s, c] = p[:, off + c * LANES:off + (c + 1) * LANES]
            off += widths[s]
        for di, d in enumerate(dils):
            for s in range(n_dil):
                for r in range(d):
                    for c in range(widths[s] // LANES):
                        dil_refs[di * n_dil + s][0, r, :, c * LANES:(c + 1) * LANES] = (
                            stage[s, c, pl.ds(r, tm // d, stride=d), :])
    for wt_ref, t_ref, blocked in zip(wt_refs, t_refs, t_blocked):
        vt = _dot_nt(wt_ref[...], h)
        if blocked:
            blk = t_ref.shape[2]
            for c in range(t_ref.shape[0]):
                t_ref[c] = vt[:, c * blk:(c + 1) * blk]
        else:
            t_ref[0] = vt


def _inproj(x, mods, tpg, w, widths, *, t_weights=(), t_block=None, seq_tiles=None, n_dil=0, dils=()):
    n, d = x.shape
    tm = min(TOKEN_TILE, n)
    sh, sc, _ = mods
    in_specs = [pl.BlockSpec((tm, d), lambda i: (i, 0)), _mod_spec(sh, tpg), _mod_spec(sc, tpg), _const_spec(w.shape)]
    out_specs = [pl.BlockSpec((tm, wd), lambda i: (i, 0)) for wd in widths]
    out_shape = [jax.ShapeDtypeStruct((n, wd), F32) for wd in widths]
    args = [x, sh, sc, w]
    nb = n // (seq_tiles * tm) if seq_tiles else None
    for dd in dils:
        for s in range(n_dil):
            out_specs.append(pl.BlockSpec((1, dd, tm // dd, widths[s]),
                                          lambda i: (i // seq_tiles, 0, i % seq_tiles, 0)))
            out_shape.append(jax.ShapeDtypeStruct((nb, dd, seq_tiles * tm // dd, widths[s]), F32))
    for wt, blocked in t_weights:
        in_specs.append(_const_spec(wt.shape))
        args.append(wt)
        if blocked:
            out_specs.append(pl.BlockSpec((tm // t_block, wt.shape[0], t_block), lambda i: (i, 0, 0)))
            out_shape.append(jax.ShapeDtypeStruct((n // t_block, wt.shape[0], t_block), F32))
        else:
            out_specs.append(pl.BlockSpec((1, wt.shape[0], tm), lambda i: (i // seq_tiles, 0, i % seq_tiles)))
            out_shape.append(jax.ShapeDtypeStruct((nb, wt.shape[0], seq_tiles * tm), F32))
    return pl.pallas_call(
        functools.partial(_inproj_kernel, widths=widths, t_blocked=tuple(bl for _, bl in t_weights),
                          dils=tuple(dils), n_dil=n_dil),
        grid=(n // tm,),
        in_specs=in_specs,
        out_specs=out_specs,
        out_shape=out_shape,
        scratch_shapes=[pltpu.VMEM((n_dil, widths[0] // LANES, tm, LANES), F32)] if n_dil and dils else [],
        compiler_params=_params("parallel"),
        name="in_proj",
    )(*args)


def _outproj_kernel(x_ref, gt_ref, *refs, n_branch, dil_of):
    n_att = len(dil_of)
    att_refs = refs[:n_att]
    y_ref, w_ref, g_ref, b_ref, o_ref = refs[n_att:n_att + 5]
    scratch = list(refs[n_att + 5:])
    tm = x_ref.shape[0]
    vals = []
    for ref, d in zip(att_refs, dil_of):
        if d == 1:
            vals.append(ref[...])
        else:
            buf = scratch.pop(0)
            for r in range(d):
                for c in range(buf.shape[0]):
                    buf[c, pl.ds(r, tm // d, stride=d), :] = ref[0, r, :, c * LANES:(c + 1) * LANES]
            vals.append(jnp.concatenate([buf[c] for c in range(buf.shape[0])], axis=-1))
    if n_branch > 1:
        outs = vals[:n_branch]
        lses = vals[n_branch:]
        m = functools.reduce(jnp.maximum, lses)
        ws = [jnp.exp(l - m) for l in lses]
        den = functools.reduce(lambda a, b: a + b, ws)
        att = functools.reduce(lambda a, b: a + b, [w * o for w, o in zip(ws, outs)]) / den
    else:
        att = vals[0]
    half = att.shape[1]
    mix = _dot(att.astype(BF16), w_ref[:half, :]) + _dot(y_ref[...].astype(BF16), w_ref[half:, :])
    y = ALPHA * x_ref[...] + (1.0 + gt_ref[0]) * mix
    o_ref[...] = _layer_norm(y, g_ref[...], b_ref[...])


def _outproj(x, gate, tpg, atts, y, w, ln_g, ln_b, dil_of=None, seq_tiles=None):
    n, d = x.shape
    tm = min(TOKEN_TILE, n)
    half = y.shape[1]
    dil_of = tuple(dil_of) if dil_of else (1,) * len(atts)
    n_branch = len(atts) // 2 if len(atts) > 1 else 1
    tok = lambda wd: pl.BlockSpec((tm, wd), lambda i: (i, 0))
    split = lambda dd: pl.BlockSpec((1, dd, tm // dd, half), lambda i: (i // seq_tiles, 0, i % seq_tiles, 0))
    return pl.pallas_call(
        functools.partial(_outproj_kernel, n_branch=n_branch, dil_of=dil_of),
        grid=(n // tm,),
        in_specs=[tok(d), _mod_spec(gate, tpg)] + [tok(half) if dd == 1 else split(dd) for dd in dil_of]
        + [tok(half), _const_spec(w.shape), _const_spec((1, d)), _const_spec((1, d))],
        out_specs=tok(d),
        out_shape=jax.ShapeDtypeStruct((n, d), F32),
        scratch_shapes=[pltpu.VMEM((half // LANES, tm, LANES), F32) for dd in dil_of if dd > 1],
        compiler_params=_params("parallel"),
        name="out_proj_ln",
    )(x, gate, *atts, y, w, ln_g.reshape(1, d), ln_b.reshape(1, d))


def _diff_lambda(dl_ref, lam_init):
    lp = dl_ref[...]
    a = jnp.sum(lp[0:1] * lp[1:2], axis=-1, keepdims=True)
    b = jnp.sum(lp[2:3] * lp[3:4], axis=-1, keepdims=True)
    return jnp.exp(a) - jnp.exp(b) + lam_init


def _head_rms(o, g, lam_init):
    return o * lax.rsqrt(jnp.mean(o * o, -1, keepdims=True) + LN_EPS) * g * (1.0 - lam_init)


def _diff_prompt_kernel(dl_ref, q_ref, k_ref, vt_ref, g_ref, o_ref, *, tq, lam_init):
    i = pl.program_id(2)
    lam = _diff_lambda(dl_ref, lam_init)
    q = q_ref[...] * (DH_A ** -0.5 * math.log2(math.e))
    lane = lax.broadcasted_iota(jnp.int32, (1, 2 * DH_A), 1)
    lo = lane < DH_A
    q1 = jnp.where(lo, q, 0.0).astype(BF16)
    q2 = jnp.where(lo, 0.0, q).astype(BF16)

    def scores(j):
        kb = k_ref[pl.ds(pl.multiple_of(j * tq, tq), tq), :].astype(BF16)
        return _dot_nt(kb, q1), _dot_nt(kb, q2)

    def block(j, s1, s2, carry, diagonal):
        m1, l1, a1, m2, l2, a2 = carry
        vt = vt_ref[j].astype(BF16)
        if diagonal:
            ok = (lax.broadcasted_iota(jnp.int32, (tq, 1), 0) <= lax.broadcasted_iota(jnp.int32, (1, tq), 1))
            s1 = jnp.where(ok, s1, NEG)
            s2 = jnp.where(ok, s2, NEG)
        n1 = jnp.maximum(m1, jnp.max(s1, 0, keepdims=True))
        n2 = jnp.maximum(m2, jnp.max(s2, 0, keepdims=True))
        p1 = jnp.exp2(s1 - n1)
        p2 = jnp.exp2(s2 - n2)
        c1 = jnp.exp2(m1 - n1)
        c2 = jnp.exp2(m2 - n2)
        l1 = c1 * l1 + jnp.sum(p1, 0, keepdims=True)
        l2 = c2 * l2 + jnp.sum(p2, 0, keepdims=True)
        a1 = c1 * a1 + _dot(vt, p1.astype(BF16))
        a2 = c2 * a2 + _dot(vt, p2.astype(BF16))
        return n1, l1, a1, n2, l2, a2

    mz = jnp.full((1, tq), NEG, F32)
    lz = jnp.zeros((1, tq), F32)
    az = jnp.zeros((2 * DH_A, tq), F32)

    def step(j, c):
        nxt = scores(j + 1)
        return nxt + block(j, c[0], c[1], c[2:], False)

    c = lax.fori_loop(0, i, step, scores(0) + (mz, lz, az, mz, lz, az))
    _, l1, a1, _, l2, a2 = block(i, c[0], c[1], c[2:], True)
    o = (a1 / l1 - lam * (a2 / l2)).T
    o_ref[...] = _head_rms(o, g_ref[...], lam_init)


def _diff_prompt(q, k, vt, dl, subln, b, t, lam_init):
    n, w = q.shape
    hw = 2 * DH_A
    tq = vt.shape[2]
    nq = t // tq
    return pl.pallas_call(
        functools.partial(_diff_prompt_kernel, tq=tq, lam_init=lam_init),
        grid=(b, w // hw, nq),
        in_specs=[pl.BlockSpec(dl.shape, lambda bi, h, i: (0, 0)),
                  pl.BlockSpec((tq, hw), lambda bi, h, i: (bi * nq + i, h)),
                  pl.BlockSpec((t, hw), lambda bi, h, i: (bi, h)),
                  pl.BlockSpec((nq, hw, tq), lambda bi, h, i: (bi, h, 0)),
                  pl.BlockSpec((1, hw), lambda bi, h, i: (0, 0))],
        out_specs=pl.BlockSpec((tq, hw), lambda bi, h, i: (bi * nq + i, h)),
        out_shape=jax.ShapeDtypeStruct((n, w), F32),
        compiler_params=_params("parallel", "parallel", "arbitrary"),
        name="diff_attn_prompt",
    )(dl, q, k, vt, subln.reshape(1, hw))


def _block_diag_queries(q8, groups, group_width):
    rows = groups * SUBLANES
    row_g = lax.broadcasted_iota(jnp.int32, (rows, 1), 0) // SUBLANES
    lane_g = lax.broadcasted_iota(jnp.int32, (1, q8.shape[1]), 1) // group_width
    return jnp.where(lane_g == row_g, jnp.concatenate([q8] * groups, axis=0), 0.0).astype(BF16)


def _diff_sample_kernel(pt_ref, dl_ref, q_ref, kn_ref, vn_ref, g_ref, *refs, n_pages, lam_init):
    del pt_ref
    kt_refs, v_refs, o_ref = refs[:n_pages], refs[n_pages:2 * n_pages], refs[2 * n_pages]
    width = q_ref.shape[2]
    groups = width // DH_A
    hw = 2 * DH_A
    qbd = _block_diag_queries(q_ref[0] * (DH_A ** -0.5), groups, DH_A)
    row_t = lax.broadcasted_iota(jnp.int32, (groups * SUBLANES, 1), 0) % SUBLANES
    col = lax.broadcasted_iota(jnp.int32, (1, SUBLANES), 1)
    s_pages = [_dot(qbd, kt[0].astype(BF16)) for kt in kt_refs]
    s_new = jnp.where(col <= row_t, _dot_nt(qbd, kn_ref[0].astype(BF16)), NEG)
    m = jnp.max(s_new, -1, keepdims=True)
    for s in s_pages:
        m = jnp.maximum(m, jnp.max(s, -1, keepdims=True))
    p_new = jnp.exp(s_new - m)
    den = jnp.sum(p_new, -1, keepdims=True)
    p_pages = []
    for s in s_pages:
        pr = jnp.exp(s - m)
        den = den + jnp.sum(pr, -1, keepdims=True)
        p_pages.append(pr.astype(BF16))
    p_new = p_new.astype(BF16)
    lam = _diff_lambda(dl_ref, lam_init)
    vn = vn_ref[0].astype(BF16)
    n_heads = width // hw
    page = v_refs[0].shape[1]
    spread = jnp.where(lax.broadcasted_iota(jnp.int32, (page, page * n_heads), 1) // n_heads
                       == lax.broadcasted_iota(jnp.int32, (page, page * n_heads), 0), 1.0, 0.0).astype(BF16)
    row_h = lax.broadcasted_iota(jnp.int32, (groups * SUBLANES, 1), 0) // (2 * SUBLANES)
    own = lax.broadcasted_iota(jnp.int32, (1, page * n_heads), 1) % n_heads == row_h
    p_wide = [jnp.where(own, _dot(pr, spread), 0.0).astype(BF16) for pr in p_pages]
    v_rows = [v_ref[0].reshape(page * n_heads, hw).astype(BF16) for v_ref in v_refs]
    acc = jnp.concatenate([_dot(p_new[2 * h * SUBLANES:2 * (h + 1) * SUBLANES], vn[:, h * hw:(h + 1) * hw])
                           for h in range(n_heads)], axis=0)
    for pw_, vr in zip(p_wide, v_rows):
        acc = acc + _dot(pw_, vr)
    a = acc / den
    outs = []
    for h in range(n_heads):
        r0 = 2 * h * SUBLANES
        outs.append(_head_rms(a[r0:r0 + SUBLANES] - lam * a[r0 + SUBLANES:r0 + 2 * SUBLANES], g_ref[...], lam_init))
    o_ref[0] = jnp.concatenate(outs, axis=-1)


def _diff_sample(q8, kn8, vn8, pool_kt, pool_v, page_table, dl, subln, lam_init):
    bs, _, w = q8.shape
    n_pages = page_table.shape[1]
    tok = pl.BlockSpec((1, SUBLANES, w), lambda b, pt: (b, 0, 0))
    kspecs = [pl.BlockSpec((1,) + pool_kt.shape[1:], lambda b, pt, j=j: (pt[b, j], 0, 0)) for j in range(n_pages)]
    vspecs = [pl.BlockSpec((1,) + pool_v.shape[1:], lambda b, pt, j=j: (pt[b, j], 0, 0, 0)) for j in range(n_pages)]
    return pl.pallas_call(
        functools.partial(_diff_sample_kernel, n_pages=n_pages, lam_init=lam_init),
        grid_spec=pltpu.PrefetchScalarGridSpec(
            num_scalar_prefetch=1,
            grid=(bs,),
            in_specs=[pl.BlockSpec(dl.shape, lambda b, pt: (0, 0)), tok, tok, tok,
                      pl.BlockSpec((1, 2 * DH_A), lambda b, pt: (0, 0))] + kspecs + vspecs,
            out_specs=tok),
        out_shape=jax.ShapeDtypeStruct((bs, SUBLANES, w), F32),
        compiler_params=_params("parallel"),
        name="diff_attn_sample",
    )(page_table, dl, q8, kn8, vn8, subln.reshape(1, 2 * DH_A), *([pool_kt] * n_pages), *([pool_v] * n_pages))


def _s5_prep_kernel(are_ref, aim_ref, ldt_ref, bre_ref, bim_ref, lr_ref, li_ref, bbr_ref, bbi_ref):
    a_re, a_im = are_ref[...], aim_ref[...]
    dt = jnp.exp(ldt_ref[...])
    mag = jnp.exp(a_re * dt)
    lam_re, lam_im = mag * jnp.cos(a_im * dt), mag * jnp.sin(a_im * dt)
    den = a_re * a_re + a_im * a_im
    nr = lam_re - 1.0
    f_re = (nr * a_re + lam_im * a_im) / den
    f_im = (lam_im * a_re - nr * a_im) / den
    lr_ref[...] = lam_re
    li_ref[...] = lam_im
    for g in range(a_re.shape[0]):
        fr, fi = f_re[g:g + 1, :], f_im[g:g + 1, :]
        br, bi = bre_ref[g], bim_ref[g]
        bbr_ref[g] = fr * br - fi * bi
        bbi_ref[g] = fr * bi + fi * br


def _s5_prep(a_re, a_im, log_dt, b_re, b_im):
    g, p = a_re.shape
    c = b_re.shape[-1]
    bt = lambda b: jnp.transpose(b, (0, 2, 1))
    sd = jax.ShapeDtypeStruct
    return pl.pallas_call(
        _s5_prep_kernel,
        out_shape=[sd((g, p), F32), sd((g, p), F32), sd((g, c, p), F32), sd((g, c, p), F32)],
        name="s5_prep",
    )(a_re, a_im, log_dt.reshape(g, 1), bt(b_re), bt(b_im))


def _s5_tail(u, hr, hi, ccr, cci, d, gw, gb):
    y = _dot(hr.astype(BF16), ccr) - _dot(hi.astype(BF16), cci) + d * u
    z = jax.nn.gelu(y)
    return z * _sigmoid(_dot(z.astype(BF16), gw) + gb)


def _s5_prompt_kernel(u_ref, bbr_ref, bbi_ref, lr_ref, li_ref, h0r_ref, h0i_ref, ccr_ref, cci_ref,
                      d_ref, gw_ref, gb_ref, y_ref, hr_out, hi_out, xr, xi, cr, ci):
    i = pl.program_id(1)
    tc = u_ref.shape[0]

    @pl.when(i == 0)
    def _():
        cr[...] = h0r_ref[0]
        ci[...] = h0i_ref[0]

    u = u_ref[...]
    ub = u.astype(BF16)
    xr[...] = _dot(ub, bbr_ref[...])
    xi[...] = _dot(ub, bbi_ref[...])
    lr, li = lr_ref[...], li_ref[...]

    def step(t, carry):
        hr, hi = carry
        nhr = lr * hr - li * hi + xr[pl.ds(t, 1), :]
        nhi = lr * hi + li * hr + xi[pl.ds(t, 1), :]
        xr[pl.ds(t, 1), :] = nhr
        xi[pl.ds(t, 1), :] = nhi
        return nhr, nhi

    hr, hi = lax.fori_loop(0, tc, step, (cr[...], ci[...]), unroll=8)
    cr[...] = hr
    ci[...] = hi
    y_ref[...] = _s5_tail(u, xr[...], xi[...], ccr_ref[...], cci_ref[...], d_ref[...], gw_ref[...], gb_ref[...])

    @pl.when(i == pl.num_programs(1) - 1)
    def _():
        hr_out[0] = hr
        hi_out[0] = hi


def _s5_prompt(u, b, t, h0r, h0i, prm):
    bbr, bbi, lr, li, ccr, cci, d, gw, gb = prm
    n, ch = u.shape
    ns = lr.shape[1]
    tc = min(512, t)
    nt = t // tc
    st = pl.BlockSpec((1, 1, ns), lambda bi, i: (bi, 0, 0))
    sd = jax.ShapeDtypeStruct
    return pl.pallas_call(
        _s5_prompt_kernel,
        grid=(b, nt),
        in_specs=[pl.BlockSpec((tc, ch), lambda bi, i: (bi * nt + i, 0)),
                  _const_spec(bbr.shape), _const_spec(bbi.shape), _const_spec(lr.shape), _const_spec(li.shape),
                  st, st, _const_spec(ccr.shape), _const_spec(cci.shape), _const_spec(d.shape),
                  _const_spec(gw.shape), _const_spec(gb.shape)],
        out_specs=[pl.BlockSpec((tc, ch), lambda bi, i: (bi * nt + i, 0)), st, st],
        out_shape=[sd((n, ch), F32), sd((b, 1, ns), F32), sd((b, 1, ns), F32)],
        scratch_shapes=[pltpu.VMEM((tc, ns), F32), pltpu.VMEM((tc, ns), F32),
                        pltpu.VMEM((1, ns), F32), pltpu.VMEM((1, ns), F32)],
        compiler_params=_params("parallel", "arbitrary"),
        name="s5_prompt",
    )(u, bbr, bbi, lr, li, h0r, h0i, ccr, cci, d, gw, gb)


def _s5_sample_kernel(u_ref, bbr_ref, bbi_ref, lr_ref, li_ref, h0r_ref, h0i_ref, ccr_ref, cci_ref,
                      d_ref, gw_ref, gb_ref, y_ref, hr_out, hi_out):
    lr, li = lr_ref[...], li_ref[...]
    hr, hi = h0r_ref[...], h0i_ref[...]
    for t in range(u_ref.shape[0]):
        u = u_ref[t]
        ub = u.astype(BF16)
        xr = _dot(ub, bbr_ref[...])
        xi = _dot(ub, bbi_ref[...])
        hr, hi = lr * hr - li * hi + xr, lr * hi + li * hr + xi
        y_ref[t] = _s5_tail(u, hr, hi, ccr_ref[...], cci_ref[...], d_ref[...], gw_ref[...], gb_ref[...])
    hr_out[...] = hr
    hi_out[...] = hi


def _s5_sample(u_tm, h0r, h0i, prm):
    bbr, bbi, lr, li, ccr, cci, d, gw, gb = prm
    sd = jax.ShapeDtypeStruct
    return pl.pallas_call(
        _s5_sample_kernel,
        out_shape=[sd(u_tm.shape, F32), sd(h0r.shape, F32), sd(h0i.shape, F32)],
        compiler_params=pltpu.CompilerParams(vmem_limit_bytes=VMEM_LIMIT_BYTES),
        name="s5_sample",
    )(u_tm, bbr, bbi, lr, li, h0r, h0i, ccr, cci, d, gw, gb)


def _dil_prompt_kernel(q_ref, kp_ref, kc_ref, vp_ref, vc_ref, o_ref, l_ref):
    n = pl.program_id(1)
    blk = q_ref.shape[1]
    q = q_ref[0] * (DH_C ** -0.5)
    kcat = jnp.concatenate([kp_ref[0], kc_ref[0]], axis=0).astype(BF16)
    vcat = jnp.concatenate([vp_ref[0], vc_ref[0]], axis=0).astype(BF16)
    qi = lax.broadcasted_iota(jnp.int32, (blk, 1), 0) + blk
    ki = lax.broadcasted_iota(jnp.int32, (1, 2 * blk), 1)
    dist = qi - ki
    lo_k = jnp.where(n > 0, 0, blk)
    ok = (dist >= 0) & (dist <= blk) & (ki >= lo_k)
    w = q.shape[1]
    nh = w // DH_C
    lane_h = lax.broadcasted_iota(jnp.int32, (1, w), 1) // DH_C
    qbd = jnp.concatenate([jnp.where(lane_h == h, q, 0.0).astype(BF16) for h in range(nh)], axis=0)
    okh = jnp.concatenate([ok] * nh, axis=0)
    s = jnp.where(okh, _dot_nt(qbd, kcat), NEG)
    m = jnp.max(s, -1, keepdims=True)
    pr = jnp.exp(s - m)
    den = jnp.sum(pr, -1, keepdims=True)
    oh = _dot((pr / den).astype(BF16), vcat)
    lh = m + jnp.log(den)
    o = jnp.zeros((blk, w), F32)
    lse = jnp.zeros((blk, w), F32)
    for h in range(nh):
        o = o + jnp.where(lane_h == h, oh[h * blk:(h + 1) * blk, :], 0.0)
        lse = lse + jnp.where(lane_h == h, lh[h * blk:(h + 1) * blk, :], 0.0)
    o_ref[0] = o
    l_ref[0] = lse


def _dil_prompt(qd, kd, vd):
    bd, ns, w = qd.shape
    cur = pl.BlockSpec((1, C_BLK, w), lambda b, n: (b, n, 0))
    prev = pl.BlockSpec((1, C_BLK, w), lambda b, n: (b, jnp.maximum(n - 1, 0), 0))
    sd = jax.ShapeDtypeStruct((bd, ns, w), F32)
    return pl.pallas_call(
        _dil_prompt_kernel,
        grid=(bd, ns // C_BLK),
        in_specs=[cur, prev, cur, prev, cur],
        out_specs=[cur, cur],
        out_shape=[sd, sd],
        compiler_params=_params("parallel", "parallel"),
        name="dilated_attn_prompt",
    )(qd, kd, kd, vd, vd)


def _dil_sample_kernel(q_ref, kn_ref, vn_ref, kt_ref, vt_ref, o_ref):
    w = q_ref.shape[2]
    nh = w // DH_C
    buf = kt_ref.shape[2]
    rows = nh * SUBLANES
    qbd = _block_diag_queries(q_ref[0] * (DH_C ** -0.5), nh, DH_C)
    row_q = lax.broadcasted_iota(jnp.int32, (rows, 1), 0) % SUBLANES

    def reach(delta):
        mult = jnp.zeros(delta.shape, F32)
        for d in C_DILATIONS:
            ok = (delta >= 0) & (lax.rem(delta, d) == 0) & (delta <= d * C_BLK)
            mult = mult + jnp.where(ok, 1.0, 0.0)
        return mult

    mult = reach(buf + row_q - lax.broadcasted_iota(jnp.int32, (1, buf), 1))
    mult_n = reach(row_q - lax.broadcasted_iota(jnp.int32, (1, SUBLANES), 1))
    s = jnp.where(mult > 0, _dot(qbd, kt_ref[0].astype(BF16)), NEG)
    s_n = jnp.where(mult_n > 0, _dot_nt(qbd, kn_ref[0].astype(BF16)), NEG)
    m = jnp.maximum(jnp.max(s, -1, keepdims=True), jnp.max(s_n, -1, keepdims=True))
    p = mult * jnp.exp(s - m)
    p_n = mult_n * jnp.exp(s_n - m)
    inv = 1.0 / (jnp.sum(p, -1, keepdims=True) + jnp.sum(p_n, -1, keepdims=True))
    o = _dot_nt((p * inv).astype(BF16), vt_ref[0].astype(BF16)) + _dot((p_n * inv).astype(BF16), vn_ref[0].astype(BF16))
    lane_h = lax.broadcasted_iota(jnp.int32, (1, w), 1) // DH_C
    out = jnp.zeros((SUBLANES, w), F32)
    for h in range(nh):
        out = out + jnp.where(lane_h == h, o[h * SUBLANES:(h + 1) * SUBLANES, :], 0.0)
    o_ref[0] = out


def _dil_sample(q8, kn8, vn8, cache_kt, cache_vt):
    bs, w, buf = cache_kt.shape
    tok = pl.BlockSpec((1, SUBLANES, w), lambda b: (b, 0, 0))
    cache = pl.BlockSpec((1, w, buf), lambda b: (b, 0, 0))
    return pl.pallas_call(
        _dil_sample_kernel,
        grid=(bs,),
        in_specs=[tok, tok, tok, cache, cache],
        out_specs=tok,
        out_shape=jax.ShapeDtypeStruct((bs, SUBLANES, w), F32),
        compiler_params=_params("parallel"),
        name="dilated_attn_sample",
    )(q8, kn8, vn8, cache_kt, cache_vt)


def _softplus(x):
    return jnp.maximum(x, 0.0) + jnp.log1p(jnp.exp(-jnp.abs(x)))


def _rwkv_pre_kernel(pd_ref, pv_ref, mu_ref, w0_ref, w2_ref, a0_ref, a2_ref, g2_ref, kk_ref, ka_ref, rk_ref, seg_ref,
                     r_o, ld_o, k_o, v_o, kk_o, b_o, g_o, bonus_o):
    pd = pd_ref[...]
    xm = pd + (pv_ref[...] - pd) * mu_ref[...]
    o1, o2, o3 = D_W, 2 * D_W, 3 * D_W
    o5 = o3 + D_LORA_W + D_LORA_A
    r, k, v = xm[:, :o1], xm[:, o1:o2], xm[:, o2:o3]
    wa, gl = xm[:, o3:o5], xm[:, o5:]
    lw = _dot(jnp.tanh(wa).astype(BF16), w2_ref[...])
    la = _dot(wa.astype(BF16), a2_ref[...])
    g = _dot(_sigmoid(gl).astype(BF16), g2_ref[...])
    w_log = -_softplus(-(w0_ref[...] + lw)) - 0.5
    a = _sigmoid(a0_ref[...] + la)
    seg = seg_ref[...]
    kk = k * kk_ref[...]
    kk = kk / jnp.maximum(jnp.sqrt(_dot_exact_rhs(kk * kk, seg)), 1e-12)
    k2 = k * (1.0 + (a - 1.0) * ka_ref[...])
    r_o[...] = r
    ld_o[...] = -jnp.exp(w_log)
    k_o[...] = k2
    v_o[...] = v
    kk_o[...] = kk
    b_o[...] = kk * a
    g_o[...] = g
    bonus_o[...] = _dot_exact_rhs(r * k2 * rk_ref[...], seg) * v


def _rwkv_pre(pd, prev, prm):
    n, cols = pd.shape
    tm = min(TOKEN_TILE, n)
    tok = lambda wd: pl.BlockSpec((tm, wd), lambda i: (i, 0))
    return pl.pallas_call(
        _rwkv_pre_kernel,
        grid=(n // tm,),
        in_specs=[tok(cols), tok(cols)] + [_const_spec(p.shape) for p in prm],
        out_specs=[tok(D_W)] * 8,
        out_shape=[jax.ShapeDtypeStruct((n, D_W), F32)] * 8,
        compiler_params=_params("parallel"),
        name="rwkv_pre",
    )(pd, prev, *prm)


def _rwkv_chunk_kernel(r_ref, ld_ref, k_ref, v_ref, kk_ref, b_ref, y_ref, s_out, st):
    c = pl.program_id(1)
    ch = r_ref.shape[0]
    pw = 2 * N_D

    @pl.when(c == 0)
    def _():
        st[...] = jnp.zeros(st.shape, F32)

    ri = lax.broadcasted_iota(jnp.int32, (ch, ch), 0)
    ci = lax.broadcasted_iota(jnp.int32, (ch, ch), 1)
    tri_incl = ci <= ri
    tri_strict = ci < ri
    eye_c = jnp.where(ci == ri, 1.0, 0.0)
    ld = ld_ref[...]
    cum = _dot_exact_rhs_left(jnp.where(tri_incl, 1.0, 0.0).astype(BF16), ld)
    cum_end = cum[ch - 1:ch, :]
    g_inc = jnp.exp(cum)
    g_inv = jnp.exp(-cum)
    g_end = jnp.exp(cum_end - cum)
    rho = r_ref[...] * g_inc
    kap = kk_ref[...] * jnp.exp(cum - ld)
    kh = k_ref[...] * g_inv
    bh = b_ref[...] * g_inv
    khg = k_ref[...] * g_end
    bhg = b_ref[...] * g_end
    gam_end = jnp.exp(cum_end)
    v = v_ref[...]

    pi = lax.broadcasted_iota(jnp.int32, (pw, pw), 0)
    pj = lax.broadcasted_iota(jnp.int32, (pw, pw), 1)
    same_head = (pi // N_D) == (pj // N_D)
    lane = lax.broadcasted_iota(jnp.int32, (1, pw), 1)
    first = lane < N_D

    npair = r_ref.shape[1] // pw
    heads = [(pr, hh) for pr in range(npair) for hh in range(2)]
    sls = [slice(pr * pw, (pr + 1) * pw) for pr in range(npair)]
    sel = (first, jnp.logical_not(first))
    bf = lambda x: x.astype(BF16)
    kap_p = [kap[:, s] for s in sls]
    rho_p = [rho[:, s] for s in sls]
    kap_b = [bf(x) for x in kap_p]
    kh_b = [bf(kh[:, s]) for s in sls]
    bh_b = [bf(bh[:, s]) for s in sls]
    v_b = [bf(v[:, s]) for s in sls]
    kap_m = [bf(jnp.where(sel[hh], kap_p[pr], 0.0)) for pr, hh in heads]
    rho_m = [bf(jnp.where(sel[hh], rho_p[pr], 0.0)) for pr, hh in heads]
    a_b = [jnp.where(tri_strict, _dot_nt(kap_m[i], bh_b[pr]), 0.0) for i, (pr, _) in enumerate(heads)]
    a_k = [bf(jnp.where(tri_strict, _dot_nt(kap_m[i], kh_b[pr]), 0.0)) for i, (pr, _) in enumerate(heads)]
    ap_b = [bf(jnp.where(tri_incl, _dot_nt(rho_m[i], bh_b[pr]), 0.0)) for i, (pr, _) in enumerate(heads)]
    ap_k = [bf(jnp.where(tri_incl, _dot_nt(rho_m[i], kh_b[pr]), 0.0)) for i, (pr, _) in enumerate(heads)]
    tl = [eye_c - a for a in a_b]
    pw2 = [_dot(bf(a), bf(a)) for a in a_b]
    span = 2
    while span < ch:
        tl = [t + _dot(bf(t), bf(p2)) for t, p2 in zip(tl, pw2)]
        span *= 2
        if span < ch:
            pw2 = [_dot(bf(p2), bf(p2)) for p2 in pw2]
    tl_b = [bf(t) for t in tl]
    kap2_h = [_dot(tl_b[i], kap_b[pr]) for i, (pr, _) in enumerate(heads)]
    akv = [bf(_dot(a_k[i], v_b[pr])) for i, (pr, _) in enumerate(heads)]
    wr_h = [_dot(tl_b[i], akv[i]) for i in range(len(heads))]
    rho2_h = [rho_p[pr] - _dot(ap_b[i], bf(kap2_h[i])) for i, (pr, _) in enumerate(heads)]
    yloc_h = [_dot(ap_k[i], v_b[pr]) - _dot(ap_b[i], bf(wr_h[i])) for i, (pr, _) in enumerate(heads)]
    pair = lambda xs: [jnp.where(first, xs[2 * pr], xs[2 * pr + 1]) for pr in range(npair)]
    kap2, wr, rho2, yloc = pair(kap2_h), pair(wr_h), pair(rho2_h), pair(yloc_h)
    bhg_b = [bf(bhg[:, s]) for s in sls]
    khg_b = [bf(khg[:, s]) for s in sls]
    diag = [jnp.where(pi == pj, jnp.broadcast_to(gam_end[:, s], (pw, pw)), 0.0) for s in sls]
    phi = [jnp.where(same_head, diag[pr] - _dot_tn(bhg_b[pr], bf(kap2[pr])), 0.0) for pr in range(npair)]
    gmat = [jnp.where(same_head, _dot_tn(khg_b[pr], v_b[pr]) - _dot_tn(bhg_b[pr], bf(wr[pr])), 0.0)
            for pr in range(npair)]
    s_b = [bf(st[pr]) for pr in range(npair)]
    for pr in range(npair):
        y_ref[:, sls[pr]] = _dot(bf(rho2[pr]), s_b[pr]) + yloc[pr]
    for pr in range(npair):
        st[pr] = _dot(bf(phi[pr]), s_b[pr]) + gmat[pr]

    @pl.when(c == pl.num_programs(1) - 1)
    def _():
        s_out[0] = st[...]


def _dot_exact_rhs_left(m01, x):
    hi, mid, lo = _split3(x)
    return _dot(m01, hi) + _dot(m01, mid) + _dot(m01, lo)


def _rwkv_chunk(r, ld, k2, v, kk, bb, b, t):
    n, w = r.shape
    ch = min(RWKV_CHUNK, t)
    nc = t // ch
    npair = w // (2 * N_D)
    tok = pl.BlockSpec((ch, w), lambda bi, c: (bi * nc + c, 0))
    return pl.pallas_call(
        _rwkv_chunk_kernel,
        grid=(b, nc),
        in_specs=[tok] * 6,
        out_specs=[tok, pl.BlockSpec((1, npair, 2 * N_D, 2 * N_D), lambda bi, c: (bi, 0, 0, 0))],
        out_shape=[jax.ShapeDtypeStruct((n, w), F32), jax.ShapeDtypeStruct((b, npair, 2 * N_D, 2 * N_D), F32)],
        scratch_shapes=[pltpu.VMEM((npair, 2 * N_D, 2 * N_D), F32)],
        compiler_params=_params("parallel", "arbitrary"),
        name="rwkv_chunk_scan",
    )(r, ld, k2, v, kk, bb)


def _rwkv_lane_kernel(r_ref, ld_ref, k_ref, v_ref, kk_ref, b_ref, s_ref, y_ref, s_out):
    steps = r_ref.shape[0]

    def body(vi, carry):
        s = s_ref[0, vi]
        for t in range(steps):
            sk = jnp.sum(s * kk_ref[t, 0], axis=0, keepdims=True)
            vv = v_ref[t, 0, pl.ds(vi, 1), :]
            s = s * jnp.exp(ld_ref[t, 0]) - sk * b_ref[t, 0] + vv * k_ref[t, 0]
            y_ref[t, 0, pl.ds(vi, 1), :] = jnp.sum(s * r_ref[t, 0], axis=0, keepdims=True)
        s_out[0, vi] = s
        return carry

    lax.fori_loop(0, s_ref.shape[1], body, 0)


def _rwkv_lane(rt, ldt, kt, vt, kkt, bt, s0):
    steps, nh, nd, bs = rt.shape
    tok = pl.BlockSpec((steps, 1, nd, bs), lambda h: (0, h, 0, 0))
    stt = pl.BlockSpec((1, nd, nd, bs), lambda h: (h, 0, 0, 0))
    return pl.pallas_call(
        _rwkv_lane_kernel,
        grid=(nh,),
        in_specs=[tok] * 6 + [stt],
        out_specs=[tok, stt],
        out_shape=[jax.ShapeDtypeStruct(rt.shape, F32), jax.ShapeDtypeStruct(s0.shape, F32)],
        compiler_params=_params("parallel"),
        name="rwkv_lane_scan",
    )(rt, ldt, kt, vt, kkt, bt, s0)


def _rwkv_post_kernel(y_ref, bonus_ref, g_ref, gw_ref, gb_ref, seg_ref, o_ref):
    y = y_ref[...]
    seg = seg_ref[...]
    mu = _dot_exact_rhs(y, seg) * (1.0 / N_D)
    yc = y - mu
    var = _dot_exact_rhs(yc * yc, seg) * (1.0 / N_D)
    yn = yc * lax.rsqrt(var + GN_EPS) * gw_ref[...] + gb_ref[...]
    o_ref[...] = (yn + bonus_ref[...]) * g_ref[...]


def _rwkv_post(y, bonus, g, gn_w, gn_b, seg):
    n, w = y.shape
    tm = min(TOKEN_TILE, n)
    tok = pl.BlockSpec((tm, w), lambda i: (i, 0))
    return pl.pallas_call(
        _rwkv_post_kernel,
        grid=(n // tm,),
        in_specs=[tok, tok, tok, _const_spec((1, w)), _const_spec((1, w)), _const_spec(seg.shape)],
        out_specs=tok,
        out_shape=jax.ShapeDtypeStruct((n, w), F32),
        compiler_params=_params("parallel"),
        name="rwkv_post",
    )(y, bonus, g, gn_w.reshape(1, w), gn_b.reshape(1, w), seg)


def _pad_tokens(x, bs, s_len):
    x = x.reshape(bs, s_len, x.shape[-1])
    return jnp.pad(x, ((0, 0), (0, SUBLANES - s_len), (0, 0)))


def _block_diag_in(bb):
    g, c, p = bb.shape
    return jnp.einsum('gcp,gh->gchp', bb, jnp.eye(g, dtype=bb.dtype)).reshape(g * c, g * p)


def _block_diag_out(cc):
    g, c, p = cc.shape
    return jnp.einsum('gcp,gh->gphc', cc, jnp.eye(g, dtype=cc.dtype)).reshape(g * p, g * c)


def _shifted(pd, shift0, b, t):
    pd3 = pd.reshape(b, t, pd.shape[-1])
    return jnp.concatenate([shift0[:, None, :], pd3[:, :-1]], axis=1).reshape(b * t, pd.shape[-1])


def _trunk(x, mod_all, row0, rows, per_token_mod, sample, st, p):
    b, t, d = x.shape
    n = b * t
    xt = x.reshape(n, d)
    tm = min(TOKEN_TILE, n)
    outs = {}
    for l in range(DEPTH):
        mod = mod_all[l, row0:row0 + rows].reshape(rows, 9, d)

        def mods(i):
            sel = [mod[:, 3 * i + j] for j in range(3)]
            if per_token_mod:
                return [jnp.repeat(m, t, axis=0).reshape(n // tm, tm, d) for m in sel], 1
            return [m.reshape(b, 1, d) for m in sel], t // tm

        m0, tpg = mods(0)
        xt = _ffn(xt, m0, tpg, p['wg'][l][0], p['wu'][l][0], p['wd'][l][0], p['ln_g'][l, 0], p['ln_b'][l, 0])
        m1, _ = mods(1)
        if l % 2 == 0:
            e = l // 2
            lam_init = 0.8 - 0.6 * math.exp(-0.3 * l)
            widths = (512, 512, 512, S5_CH)
            dl, subln = p['diff_lambda'][e], p['diff_subln'][e]
            if sample:
                q, k, v, u = _inproj(xt, m1, tpg, p['even_w_in'][e], widths)
            else:
                q, k, v, u, vt, kt = _inproj(xt, m1, tpg, p['even_w_in'][e], widths,
                                             t_weights=((p['even_wv_t'][e], True), (p['even_wk_t'][e], False)),
                                             t_block=min(256, t), seq_tiles=t // tm)
            if sample:
                att8 = _diff_sample(_pad_tokens(q, b, t), _pad_tokens(k, b, t), _pad_tokens(v, b, t),
                                    st['pool_k'][e], st['pool_v'][e], st['page_table'], dl, subln, lam_init)
                att = att8[:, :t].reshape(n, -1)
                u_tm = u.reshape(b, t, -1).transpose(1, 0, 2)
                y_tm, hr, hi = _s5_sample(u_tm, st['s5_re'][e].reshape(b, S5_N), st['s5_im'][e].reshape(b, S5_N),
                                          p['s5'][e])
                y5 = y_tm.transpose(1, 0, 2).reshape(n, -1)
            else:
                att = _diff_prompt(q, k, vt, dl, subln, b, t, lam_init)
                zero = jnp.zeros((b, 1, S5_N), F32)
                y5, hr, hi = _s5_prompt(u, b, t, zero, zero, p['s5'][e])
            if sample:
                k_out = k.reshape(b, t, H_A, 2, DH_A)
            else:
                k_out = kt.reshape(b, H_A, 2, DH_A, t).transpose(0, 4, 1, 2, 3)
            outs.setdefault('ak', []).append(k_out)
            outs.setdefault('av', []).append(v.reshape(b, t, H_A, 2 * DH_A))
            outs.setdefault('s5r', []).append(hr.reshape(b, S5_G, S5_P))
            outs.setdefault('s5i', []).append(hi.reshape(b, S5_G, S5_P))
            xt = _outproj(xt, m1[2], tpg, [att], y5, p['even_w_out'][e], p['ln_g'][l, 1], p['ln_b'][l, 1])
        else:
            o = l // 2
            widths = (512, 512, 512, pd_cols(p))
            rp = p['rwkv'][o]
            dil_of = None
            if sample:
                q, k, v, pd = _inproj(xt, m1, tpg, p['odd_w_in'][o], widths)
            else:
                dils = C_DILATIONS[1:]
                res = _inproj(xt, m1, tpg, p['odd_w_in'][o], widths,
                              t_weights=((p['odd_wk_t'][o], False), (p['odd_wv_t'][o], False)),
                              seq_tiles=t // tm, n_dil=3, dils=dils)
                q, k, v, pd = res[:4]
                split = {dd: res[4 + 3 * i:7 + 3 * i] for i, dd in enumerate(dils)}
                kt, vt = res[4 + 3 * len(dils):]
            if sample:
                att8 = _dil_sample(_pad_tokens(q, b, t), _pad_tokens(k, b, t), _pad_tokens(v, b, t),
                                   st['cache_c_k'][o], st['cache_c_v'][o])
                atts = [att8[:, :t].reshape(n, -1)]
                prev = _shifted(pd, st['d_shift'][o], b, t)
                r, ld, k2, vv, kk, bb, g, bonus = _rwkv_pre(pd, prev, rp['pre'])
                tl = lambda z: z.reshape(b, t, H_D, N_D).transpose(1, 2, 3, 0)
                s0 = st['d_wkv'][o].transpose(1, 2, 3, 0)
                y_l, s_new = _rwkv_lane(tl(r), tl(ld), tl(k2), tl(vv), tl(kk), tl(bb), s0)
                y = y_l.transpose(3, 0, 1, 2).reshape(n, D_W)
                s_new = s_new.transpose(3, 0, 1, 2)
                k_keep, v_keep = k.reshape(b, t, H_C, DH_C), v.reshape(b, t, H_C, DH_C)
            else:
                atts_o, atts_l = [], []
                for dil in C_DILATIONS:
                    if dil == 1:
                        ob, lb = _dil_prompt(q.reshape(b, t, -1), k.reshape(b, t, -1), v.reshape(b, t, -1))
                        ob, lb = ob.reshape(n, -1), lb.reshape(n, -1)
                    else:
                        ob, lb = _dil_prompt(*[z.reshape(b * dil, t // dil, -1) for z in split[dil]])
                        ob, lb = ob.reshape(b, dil, t // dil, -1), lb.reshape(b, dil, t // dil, -1)
                    atts_o.append(ob)
                    atts_l.append(lb)
                atts = atts_o + atts_l
                dil_of = C_DILATIONS + C_DILATIONS
                prev = _shifted(pd, jnp.zeros((b, pd.shape[-1]), F32), b, t)
                r, ld, k2, vv, kk, bb, g, bonus = _rwkv_pre(pd, prev, rp['pre'])
                y, s_pairs = _rwkv_chunk(r, ld, k2, vv, kk, bb, b, t)
                sp = s_pairs.reshape(b, H_D // 2, 2, N_D, 2, N_D)
                s_new = jnp.stack([sp[:, :, 0, :, 0, :], sp[:, :, 1, :, 1, :]], axis=2)
                s_new = s_new.reshape(b, H_D, N_D, N_D).transpose(0, 1, 3, 2)
                keep = min(C_BLK * C_DILATIONS[-1], t)
                keep_t = lambda zt: zt[:, :, t - keep:].reshape(b, H_C, DH_C, keep).transpose(0, 3, 1, 2)
                k_keep, v_keep = keep_t(kt), keep_t(vt)
            yd = _rwkv_post(y, bonus, g, rp['gn_w'], rp['gn_b'], rp['seg'])
            outs.setdefault('ck', []).append(k_keep)
            outs.setdefault('cv', []).append(v_keep)
            outs.setdefault('dw', []).append(s_new)
            outs.setdefault('ds', []).append(pd.reshape(b, t, -1)[:, -1])
            xt = _outproj(xt, m1[2], tpg, atts, yd, p['odd_w_out'][o], p['ln_g'][l, 1], p['ln_b'][l, 1],
                          dil_of=dil_of, seq_tiles=t // tm)
        m2, _ = mods(2)
        xt = _ffn(xt, m2, tpg, p['wg'][l][1], p['wu'][l][1], p['wd'][l][1], p['ln_g'][l, 2], p['ln_b'][l, 2])
    stacked = [jnp.stack(outs[key], 0) for key in ('ak', 'av', 's5r', 's5i', 'ck', 'cv', 'dw', 'ds')]
    return xt.reshape(b, t, d), stacked


def pd_cols(p):
    return p['odd_w_in'].shape[-1] - 3 * H_C * DH_C


def kernel(x_prompt, x_sample, cache_a_k, cache_a_v, state_s5_re, state_s5_im, cache_c_k, cache_c_v, state_d_wkv, state_d_shift, page_table, c_prompt, c_sample, ada_w, ada_b, ln_g, ln_b, ffn_w_gate, ffn_w_up, ffn_w_down, even_w_in, even_w_out, diff_lambda, diff_subln, s5_a_re, s5_a_im, s5_log_dt, s5_b_re, s5_b_im, s5_c_re, s5_c_im, s5_d, s5_glu_w, s5_glu_b, odd_w_in, odd_w_out, rwkv_mu, rwkv_w0, rwkv_w2, rwkv_a0, rwkv_a2, rwkv_g2, rwkv_k_k, rwkv_k_a, rwkv_r_k, rwkv_gn_w, rwkv_gn_b):
    bp, bs = x_prompt.shape[0], x_sample.shape[0]
    n_even, n_odd = even_w_in.shape[0], odd_w_in.shape[0]
    bf = lambda w: w.astype(BF16)

    seg = jnp.kron(jnp.eye(H_D, dtype=F32), jnp.ones((N_D, N_D), F32)).astype(BF16)
    s5 = []
    for e in range(n_even):
        lr, li, bbr, bbi = _s5_prep(s5_a_re[e], s5_a_im[e], s5_log_dt[e], s5_b_re[e], s5_b_im[e])
        s5.append((bf(_block_diag_in(bbr)), bf(_block_diag_in(bbi)), lr.reshape(1, S5_N), li.reshape(1, S5_N),
                   bf(_block_diag_out(s5_c_re[e])), bf(_block_diag_out(s5_c_im[e])),
                   s5_d[e].reshape(1, S5_CH), bf(s5_glu_w[e]), s5_glu_b[e].reshape(1, S5_CH)))
    rwkv = []
    for o in range(n_odd):
        row = lambda z: z.reshape(1, -1)
        w2p = jnp.concatenate([rwkv_w2[o], jnp.zeros_like(rwkv_a2[o])], axis=0)
        a2p = jnp.concatenate([jnp.zeros_like(rwkv_w2[o]), rwkv_a2[o]], axis=0)
        pre = (row(rwkv_mu[o]), row(rwkv_w0[o]), bf(w2p), row(rwkv_a0[o]), bf(a2p), bf(rwkv_g2[o]),
               row(rwkv_k_k[o]), row(rwkv_k_a[o]), row(rwkv_r_k[o]), seg)
        rwkv.append(dict(pre=pre, gn_w=rwkv_gn_w[o], gn_b=rwkv_gn_b[o], seg=seg))
    p = dict(wg=bf(ffn_w_gate), wu=bf(ffn_w_up), wd=bf(ffn_w_down), ln_g=ln_g, ln_b=ln_b,
             even_w_in=bf(even_w_in), even_w_out=bf(even_w_out), odd_w_in=bf(odd_w_in), odd_w_out=bf(odd_w_out),
             even_wv_t=bf(jnp.swapaxes(even_w_in[:, :, 2 * A_W:3 * A_W], 1, 2)),
             even_wk_t=bf(jnp.swapaxes(even_w_in[:, :, A_W:2 * A_W], 1, 2)),
             odd_wk_t=bf(jnp.swapaxes(odd_w_in[:, :, C_W:2 * C_W], 1, 2)),
             odd_wv_t=bf(jnp.swapaxes(odd_w_in[:, :, 2 * C_W:3 * C_W], 1, 2)),
             diff_lambda=diff_lambda, diff_subln=diff_subln, s5=s5, rwkv=rwkv)

    mod_all = _ada(jnp.concatenate([c_prompt, c_sample], axis=0), ada_w, ada_b)

    y_prompt, st_p = _trunk(x_prompt, mod_all, 0, bp, False, False, None, p)
    n_pool, page = cache_a_k.shape[1], cache_a_k.shape[2]
    win_buf = cache_c_k.shape[2]
    pos_minor = lambda c: jnp.transpose(c, (0, 1, 3, 4, 2)).reshape(n_odd, bs, -1, win_buf)
    st = dict(pool_k=jnp.transpose(cache_a_k, (0, 1, 3, 4, 5, 2)).reshape(n_even, n_pool, -1, page),
              pool_v=cache_a_v,
              page_table=page_table, s5_re=state_s5_re, s5_im=state_s5_im,
              cache_c_k=pos_minor(cache_c_k), cache_c_v=pos_minor(cache_c_v),
              d_wkv=state_d_wkv, d_shift=state_d_shift)
    y_sample, st_s = _trunk(x_sample, mod_all, bp, bs, True, True, st, p)
    a_k_p, a_v_p, s5_re_p, s5_im_p, c_k_p, c_v_p, d_wkv_p, d_shift_p = st_p
    a_k_s, a_v_s, s5_re_s, s5_im_s, c_k_s, c_v_s, d_wkv_s, d_shift_s = st_s
    return (y_prompt, y_sample, a_k_p, a_k_s, a_v_p, a_v_s, s5_re_p, s5_re_s, s5_im_p, s5_im_s,
            c_k_p, c_k_s, c_v_p, c_v_s, d_wkv_p, d_wkv_s, d_shift_p, d_shift_s)
```

```python
import functools
import math

import jax
import jax.numpy as jnp
from jax import lax
from jax.experimental import pallas as pl
from jax.experimental.pallas import tpu as pltpu

F32 = jnp.float32
BF16 = jnp.bfloat16

DEPTH = 2
H_A, DH_A = 4, 64
A_W = H_A * 2 * DH_A
S5_GROUP, S5_G, S5_P = 16, 32, 64
S5_CH = S5_GROUP * S5_G
S5_N = S5_G * S5_P
H_C, DH_C = 8, 64
C_W = H_C * DH_C
C_BLK = 128
C_DILATIONS = (1, 4, 16)
H_D, N_D = 8, 64
D_W = H_D * N_D
D_LORA_W, D_LORA_A, D_LORA_G = 64, 64, 128
GN_EPS = 64e-5
ALPHA = (2.0 * DEPTH) ** 0.25
LN_EPS = 1e-5
NEG = -1e30

LANES = 128
SUBLANES = 8
VMEM_LIMIT_BYTES = 56 * 1024 * 1024
TOKEN_TILE = 512
RWKV_CHUNK = 64


def _params(*sem):
    return pltpu.CompilerParams(dimension_semantics=sem, vmem_limit_bytes=VMEM_LIMIT_BYTES)


def _const_spec(shape):
    nd = len(shape)
    return pl.BlockSpec(shape, lambda *_: (0,) * nd, pipeline_mode=pl.Buffered(1))


def _dot(a, b):
    return jnp.dot(a, b, preferred_element_type=F32)


def _dot_nt(a, b):
    return lax.dot_general(a, b, (((1,), (1,)), ((), ())), preferred_element_type=F32)


def _dot_tn(a, b):
    return lax.dot_general(a, b, (((0,), (0,)), ((), ())), preferred_element_type=F32)


def _split3(x):
    hi = x.astype(BF16)
    r1 = x - hi.astype(F32)
    mid = r1.astype(BF16)
    lo = (r1 - mid.astype(F32)).astype(BF16)
    return hi, mid, lo


def _dot_exact_rhs(x, m01):
    hi, mid, lo = _split3(x)
    return _dot(hi, m01) + _dot(mid, m01) + _dot(lo, m01)


def _layer_norm(y, g, b):
    mu = jnp.mean(y, -1, keepdims=True)
    yc = y - mu
    var = jnp.mean(yc * yc, -1, keepdims=True)
    return yc * lax.rsqrt(var + LN_EPS) * g + b


def _sigmoid(x):
    return jax.nn.sigmoid(x)


def _mod_spec(mod, tiles_per_group):
    _, rm, d = mod.shape
    return pl.BlockSpec((1, rm, d), lambda i: (i // tiles_per_group, 0, 0))


def _ada_kernel(c_ref, w_ref, b_ref, o_ref):
    c = c_ref[...]
    s = (c * _sigmoid(c)).astype(BF16)
    o_ref[0] = _dot(s, w_ref[0].astype(BF16)) + b_ref[0]


def _ada(c_all, ada_w, ada_b):
    nl, d, w = ada_w.shape
    r = c_all.shape[0]
    tn = 1152 if w % 1152 == 0 else w
    return pl.pallas_call(
        _ada_kernel,
        grid=(nl, w // tn),
        in_specs=[pl.BlockSpec((r, d), lambda l, j: (0, 0)),
                  pl.BlockSpec((1, d, tn), lambda l, j: (l, 0, j)),
                  pl.BlockSpec((1, 1, tn), lambda l, j: (l, 0, j))],
        out_specs=pl.BlockSpec((1, r, tn), lambda l, j: (l, 0, j)),
        out_shape=jax.ShapeDtypeStruct((nl, r, w), F32),
        compiler_params=_params("parallel", "parallel"),
        name="ada_mod",
    )(c_all, ada_w, ada_b.reshape(nl, 1, w))


def _ffn_kernel(x_ref, sh_ref, sc_ref, gt_ref, wg_ref, wu_ref, wd_ref, g_ref, b_ref, o_ref):
    x = x_ref[...]
    h = (x * (1.0 + sc_ref[0]) + sh_ref[0]).astype(BF16)
    g = _dot(h, wg_ref[...])
    u = _dot(h, wu_ref[...])
    a = (g * _sigmoid(g) * u).astype(BF16)
    f = _dot(a, wd_ref[...])
    y = ALPHA * x + 0.5 * (1.0 + gt_ref[0]) * f
    o_ref[...] = _layer_norm(y, g_ref[...], b_ref[...])


def _ffn(x, mods, tpg, wg, wu, wd, ln_g, ln_b):
    n, d = x.shape
    f = wg.shape[1]
    tm = min(TOKEN_TILE, n)
    sh, sc, gt = mods
    return pl.pallas_call(
        _ffn_kernel,
        grid=(n // tm,),
        in_specs=[pl.BlockSpec((tm, d), lambda i: (i, 0)),
                  _mod_spec(sh, tpg), _mod_spec(sc, tpg), _mod_spec(gt, tpg),
                  _const_spec((d, f)), _const_spec((d, f)), _const_spec((f, d)),
                  _const_spec((1, d)), _const_spec((1, d))],
        out_specs=pl.BlockSpec((tm, d), lambda i: (i, 0)),
        out_shape=jax.ShapeDtypeStruct((n, d), F32),
        compiler_params=_params("parallel"),
        name="ffn_ln",
    )(x, sh, sc, gt, wg, wu, wd, ln_g.reshape(1, d), ln_b.reshape(1, d))


def _inproj_kernel(x_ref, sh_ref, sc_ref, w_ref, *refs, widths, t_blocked, dils, n_dil):
    n_t = len(t_blocked)
    n_split = n_dil * len(dils)
    wt_refs, outs = refs[:n_t], refs[n_t:]
    o_refs = outs[:len(widths)]
    dil_refs = outs[len(widths):len(widths) + n_split]
    t_refs = outs[len(widths) + n_split:len(widths) + n_split + n_t]
    tm = x_ref.shape[0]
    h = (x_ref[...] * (1.0 + sc_ref[0]) + sh_ref[0]).astype(BF16)
    p = _dot(h, w_ref[...])
    off = 0
    for o_ref, wd in zip(o_refs, widths):
        o_ref[...] = p[:, off:off + wd]
        off += wd
    if n_split:
        stage = refs[-1]
        off = 0
        for s in range(n_dil):
            for c in range(widths[s] // LANES):
                stage[s, c] = p[:, off + c * LANES:off + (c + 1) * LANES]
            off += widths[s]
        for di, d in enumerate(dils):
            for s in range(n_dil):
                for r in range(d):
                    for c in range(widths[s] // LANES):
                        dil_refs[di * n_dil + s][0, r, :, c * LANES:(c + 1) * LANES] = (
                            stage[s, c, pl.ds(r, tm // d, stride=d), :])
    for wt_ref, t_ref, blocked in zip(wt_refs, t_refs, t_blocked):
        vt = _dot_nt(wt_ref[...], h)
        if blocked:
            blk = t_ref.shape[2]
            for c in range(t_ref.shape[0]):
                t_ref[c] = vt[:, c * blk:(c + 1) * blk]
        else:
            t_ref[0] = vt


def _inproj(x, mods, tpg, w, widths, *, t_weights=(), t_block=None, seq_tiles=None, n_dil=0, dils=()):
    n, d = x.shape
    tm = min(TOKEN_TILE, n)
    sh, sc, _ = mods
    in_specs = [pl.BlockSpec((tm, d), lambda i: (i, 0)), _mod_spec(sh, tpg), _mod_spec(sc, tpg), _const_spec(w.shape)]
    out_specs = [pl.BlockSpec((tm, wd), lambda i: (i, 0)) for wd in widths]
    out_shape = [jax.ShapeDtypeStruct((n, wd), F32) for wd in widths]
    args = [x, sh, sc, w]
    nb = n // (seq_tiles * tm) if seq_tiles else None
    for dd in dils:
        for s in range(n_dil):
            out_specs.append(pl.BlockSpec((1, dd, tm // dd, widths[s]),
                                          lambda i: (i // seq_tiles, 0, i % seq_tiles, 0)))
            out_shape.append(jax.ShapeDtypeStruct((nb, dd, seq_tiles * tm // dd, widths[s]), F32))
    for wt, blocked in t_weights:
        in_specs.append(_const_spec(wt.shape))
        args.append(wt)
        if blocked:
            out_specs.append(pl.BlockSpec((tm // t_block, wt.shape[0], t_block), lambda i: (i, 0, 0)))
            out_shape.append(jax.ShapeDtypeStruct((n // t_block, wt.shape[0], t_block), F32))
        else:
            out_specs.append(pl.BlockSpec((1, wt.shape[0], tm), lambda i: (i // seq_tiles, 0, i % seq_tiles)))
            out_shape.append(jax.ShapeDtypeStruct((nb, wt.shape[0], seq_tiles * tm), F32))
    return pl.pallas_call(
        functools.partial(_inproj_kernel, widths=widths, t_blocked=tuple(bl for _, bl in t_weights),
                          dils=tuple(dils), n_dil=n_dil),
        grid=(n // tm,),
        in_specs=in_specs,
        out_specs=out_specs,
        out_shape=out_shape,
        scratch_shapes=[pltpu.VMEM((n_dil, widths[0] // LANES, tm, LANES), F32)] if n_dil and dils else [],
        compiler_params=_params("parallel"),
        name="in_proj",
    )(*args)


def _outproj_kernel(x_ref, gt_ref, *refs, n_branch, dil_of):
    n_att = len(dil_of)
    att_refs = refs[:n_att]
    y_ref, w_ref, g_ref, b_ref, o_ref = refs[n_att:n_att + 5]
    scratch = list(refs[n_att + 5:])
    tm = x_ref.shape[0]
    vals = []
    for ref, d in zip(att_refs, dil_of):
        if d == 1:
            vals.append(ref[...])
        else:
            buf = scratch.pop(0)
            for r in range(d):
                for c in range(buf.shape[0]):
                    buf[c, pl.ds(r, tm // d, stride=d), :] = ref[0, r, :, c * LANES:(c + 1) * LANES]
            vals.append(jnp.concatenate([buf[c] for c in range(buf.shape[0])], axis=-1))
    if n_branch > 1:
        outs = vals[:n_branch]
        lses = vals[n_branch:]
        m = functools.reduce(jnp.maximum, lses)
        ws = [jnp.exp(l - m) for l in lses]
        den = functools.reduce(lambda a, b: a + b, ws)
        att = functools.reduce(lambda a, b: a + b, [w * o for w, o in zip(ws, outs)]) / den
    else:
        att = vals[0]
    half = att.shape[1]
    mix = _dot(att.astype(BF16), w_ref[:half, :]) + _dot(y_ref[...].astype(BF16), w_ref[half:, :])
    y = ALPHA * x_ref[...] + (1.0 + gt_ref[0]) * mix
    o_ref[...] = _layer_norm(y, g_ref[...], b_ref[...])


def _outproj(x, gate, tpg, atts, y, w, ln_g, ln_b, dil_of=None, seq_tiles=None):
    n, d = x.shape
    tm = min(TOKEN_TILE, n)
    half = y.shape[1]
    dil_of = tuple(dil_of) if dil_of else (1,) * len(atts)
    n_branch = len(atts) // 2 if len(atts) > 1 else 1
    tok = lambda wd: pl.BlockSpec((tm, wd), lambda i: (i, 0))
    split = lambda dd: pl.BlockSpec((1, dd, tm // dd, half), lambda i: (i // seq_tiles, 0, i % seq_tiles, 0))
    return pl.pallas_call(
        functools.partial(_outproj_kernel, n_branch=n_branch, dil_of=dil_of),
        grid=(n // tm,),
        in_specs=[tok(d), _mod_spec(gate, tpg)] + [tok(half) if dd == 1 else split(dd) for dd in dil_of]
        + [tok(half), _const_spec(w.shape), _const_spec((1, d)), _const_spec((1, d))],
        out_specs=tok(d),
        out_shape=jax.ShapeDtypeStruct((n, d), F32),
        scratch_shapes=[pltpu.VMEM((half // LANES, tm, LANES), F32) for dd in dil_of if dd > 1],
        compiler_params=_params("parallel"),
        name="out_proj_ln",
    )(x, gate, *atts, y, w, ln_g.reshape(1, d), ln_b.reshape(1, d))


def _diff_lambda(dl_ref, lam_init):
    lp = dl_ref[...]
    a = jnp.sum(lp[0:1] * lp[1:2], axis=-1, keepdims=True)
    b = jnp.sum(lp[2:3] * lp[3:4], axis=-1, keepdims=True)
    return jnp.exp(a) - jnp.exp(b) + lam_init


def _head_rms(o, g, lam_init):
    return o * lax.rsqrt(jnp.mean(o * o, -1, keepdims=True) + LN_EPS) * g * (1.0 - lam_init)


def _diff_prompt_kernel(dl_ref, q_ref, k_ref, vt_ref, g_ref, o_ref, *, tq, lam_init):
    i = pl.program_id(1)
    lam = _diff_lambda(dl_ref, lam_init)
    hw = 2 * DH_A
    n_heads = q_ref.shape[1] // hw
    nm = 2 * n_heads
    lo = lax.broadcasted_iota(jnp.int32, (1, hw), 1) < DH_A
    qs = []
    for h in range(n_heads):
        q = q_ref[:, h * hw:(h + 1) * hw] * (DH_A ** -0.5 * math.log2(math.e))
        qs += [jnp.where(lo, q, 0.0).astype(BF16), jnp.where(lo, 0.0, q).astype(BF16)]
    head = lambda x: slice((x // 2) * hw, (x // 2 + 1) * hw)

    def block(j, carry, diagonal):
        ms, ls, accs = carry[:nm], carry[nm:2 * nm], carry[2 * nm:]
        kb = k_ref[pl.ds(pl.multiple_of(j * tq, tq), tq), :].astype(BF16)
        vt = vt_ref[j].astype(BF16)
        ss = [_dot_nt(kb[:, head(x)], qs[x]) for x in range(nm)]
        if diagonal:
            ok = (lax.broadcasted_iota(jnp.int32, (tq, 1), 0) <= lax.broadcasted_iota(jnp.int32, (1, tq), 1))
            ss = [jnp.where(ok, s, NEG) for s in ss]
        ns = [jnp.maximum(m, jnp.max(s, 0, keepdims=True)) for m, s in zip(ms, ss)]
        ps = [jnp.exp2(s - n) for s, n in zip(ss, ns)]
        cs = [jnp.exp2(m - n) for m, n in zip(ms, ns)]
        ls = [c * l + jnp.sum(p, 0, keepdims=True) for c, l, p in zip(cs, ls, ps)]
        accs = [cs[x] * accs[x] + _dot(vt[head(x), :], ps[x].astype(BF16)) for x in range(nm)]
        return tuple(ns) + tuple(ls) + tuple(accs)

    init = ((jnp.full((1, tq), NEG, F32),) * nm + (jnp.zeros((1, tq), F32),) * nm
            + (jnp.zeros((hw, tq), F32),) * nm)
    c = lax.fori_loop(0, i, lambda j, c: block(j, c, False), init)
    fin = block(i, c, True)
    ls, accs = fin[nm:2 * nm], fin[2 * nm:]
    outs = []
    for h in range(n_heads):
        o = (accs[2 * h] / ls[2 * h] - lam * (accs[2 * h + 1] / ls[2 * h + 1])).T
        outs.append(_head_rms(o, g_ref[...], lam_init))
    o_ref[...] = jnp.concatenate(outs, axis=-1)


def _diff_prompt(q, k, vt, dl, subln, b, t, lam_init):
    n, w = q.shape
    hw = 2 * DH_A
    tq = vt.shape[2]
    nq = t // tq
    return pl.pallas_call(
        functools.partial(_diff_prompt_kernel, tq=tq, lam_init=lam_init),
        grid=(b, nq),
        in_specs=[pl.BlockSpec(dl.shape, lambda bi, i: (0, 0)),
                  pl.BlockSpec((tq, w), lambda bi, i: (bi * nq + i, 0)),
                  pl.BlockSpec((t, w), lambda bi, i: (bi, 0)),
                  pl.BlockSpec((nq, w, tq), lambda bi, i: (bi, 0, 0)),
                  pl.BlockSpec((1, hw), lambda bi, i: (0, 0))],
        out_specs=pl.BlockSpec((tq, w), lambda bi, i: (bi * nq + i, 0)),
        out_shape=jax.ShapeDtypeStruct((n, w), F32),
        compiler_params=_params("parallel", "arbitrary"),
        name="diff_attn_prompt",
    )(dl, q, k, vt, subln.reshape(1, hw))


def _block_diag_queries(q8, groups, group_width):
    rows = groups * SUBLANES
    row_g = lax.broadcasted_iota(jnp.int32, (rows, 1), 0) // SUBLANES
    lane_g = lax.broadcasted_iota(jnp.int32, (1, q8.shape[1]), 1) // group_width
    return jnp.where(lane_g == row_g, jnp.concatenate([q8] * groups, axis=0), 0.0).astype(BF16)


def _diff_sample_kernel(pt_ref, dl_ref, q_ref, kn_ref, vn_ref, g_ref, *refs, n_pages, lam_init):
    del pt_ref
    kt_refs, v_refs, o_ref = refs[:n_pages], refs[n_pages:2 * n_pages], refs[2 * n_pages]
    width = q_ref.shape[2]
    groups = width // DH_A
    hw = 2 * DH_A
    qbd = _block_diag_queries(q_ref[0] * (DH_A ** -0.5), groups, DH_A)
    row_t = lax.broadcasted_iota(jnp.int32, (groups * SUBLANES, 1), 0) % SUBLANES
    col = lax.broadcasted_iota(jnp.int32, (1, SUBLANES), 1)
    s_pages = [_dot(qbd, kt[0].astype(BF16)) for kt in kt_refs]
    s_new = jnp.where(col <= row_t, _dot_nt(qbd, kn_ref[0].astype(BF16)), NEG)
    m = jnp.max(s_new, -1, keepdims=True)
    for s in s_pages:
        m = jnp.maximum(m, jnp.max(s, -1, keepdims=True))
    p_new = jnp.exp(s_new - m)
    den = jnp.sum(p_new, -1, keepdims=True)
    p_pages = []
    for s in s_pages:
        pr = jnp.exp(s - m)
        den = den + jnp.sum(pr, -1, keepdims=True)
        p_pages.append(pr.astype(BF16))
    p_new = p_new.astype(BF16)
    lam = _diff_lambda(dl_ref, lam_init)
    vn = vn_ref[0].astype(BF16)
    n_heads = width // hw
    page = v_refs[0].shape[1]
    spread = jnp.where(lax.broadcasted_iota(jnp.int32, (page, page * n_heads), 1) // n_heads
                       == lax.broadcasted_iota(jnp.int32, (page, page * n_heads), 0), 1.0, 0.0).astype(BF16)
    row_h = lax.broadcasted_iota(jnp.int32, (groups * SUBLANES, 1), 0) // (2 * SUBLANES)
    own = lax.broadcasted_iota(jnp.int32, (1, page * n_heads), 1) % n_heads == row_h
    p_wide = [jnp.where(own, _dot(pr, spread), 0.0).astype(BF16) for pr in p_pages]
    v_rows = [v_ref[0].reshape(page * n_heads, hw).astype(BF16) for v_ref in v_refs]
    acc = jnp.concatenate([_dot(p_new[2 * h * SUBLANES:2 * (h + 1) * SUBLANES], vn[:, h * hw:(h + 1) * hw])
                           for h in range(n_heads)], axis=0)
    for pw_, vr in zip(p_wide, v_rows):
        acc = acc + _dot(pw_, vr)
    a = acc / den
    outs = []
    for h in range(n_heads):
        r0 = 2 * h * SUBLANES
        outs.append(_head_rms(a[r0:r0 + SUBLANES] - lam * a[r0 + SUBLANES:r0 + 2 * SUBLANES], g_ref[...], lam_init))
    o_ref[0] = jnp.concatenate(outs, axis=-1)


def _diff_sample(q8, kn8, vn8, pool_kt, pool_v, page_table, dl, subln, lam_init):
    bs, _, w = q8.shape
    n_pages = page_table.shape[1]
    tok = pl.BlockSpec((1, SUBLANES, w), lambda b, pt: (b, 0, 0))
    kspecs = [pl.BlockSpec((1,) + pool_kt.shape[1:], lambda b, pt, j=j: (pt[b, j], 0, 0)) for j in range(n_pages)]
    vspecs = [pl.BlockSpec((1,) + pool_v.shape[1:], lambda b, pt, j=j: (pt[b, j], 0, 0, 0)) for j in range(n_pages)]
    return pl.pallas_call(
        functools.partial(_diff_sample_kernel, n_pages=n_pages, lam_init=lam_init),
        grid_spec=pltpu.PrefetchScalarGridSpec(
            num_scalar_prefetch=1,
            grid=(bs,),
            in_specs=[pl.BlockSpec(dl.shape, lambda b, pt: (0, 0)), tok, tok, tok,
                      pl.BlockSpec((1, 2 * DH_A), lambda b, pt: (0, 0))] + kspecs + vspecs,
            out_specs=tok),
        out_shape=jax.ShapeDtypeStruct((bs, SUBLANES, w), F32),
        compiler_params=_params("parallel"),
        name="diff_attn_sample",
    )(page_table, dl, q8, kn8, vn8, subln.reshape(1, 2 * DH_A), *([pool_kt] * n_pages), *([pool_v] * n_pages))


def _s5_prep_kernel(are_ref, aim_ref, ldt_ref, bre_ref, bim_ref, lr_ref, li_ref, bbr_ref, bbi_ref):
    a_re, a_im = are_ref[...], aim_ref[...]
    dt = jnp.exp(ldt_ref[...])
    mag = jnp.exp(a_re * dt)
    lam_re, lam_im = mag * jnp.cos(a_im * dt), mag * jnp.sin(a_im * dt)
    den = a_re * a_re + a_im * a_im
    nr = lam_re - 1.0
    f_re = (nr * a_re + lam_im * a_im) / den
    f_im = (lam_im * a_re - nr * a_im) / den
    lr_ref[...] = lam_re
    li_ref[...] = lam_im
    for g in range(a_re.shape[0]):
        fr, fi = f_re[g:g + 1, :], f_im[g:g + 1, :]
        br, bi = bre_ref[g], bim_ref[g]
        bbr_ref[g] = fr * br - fi * bi
        bbi_ref[g] = fr * bi + fi * br


def _s5_prep(a_re, a_im, log_dt, b_re, b_im):
    g, p = a_re.shape
    c = b_re.shape[-1]
    bt = lambda b: jnp.transpose(b, (0, 2, 1))
    sd = jax.ShapeDtypeStruct
    return pl.pallas_call(
        _s5_prep_kernel,
        out_shape=[sd((g, p), F32), sd((g, p), F32), sd((g, c, p), F32), sd((g, c, p), F32)],
        name="s5_prep",
    )(a_re, a_im, log_dt.reshape(g, 1), bt(b_re), bt(b_im))


S5_SPLIT = 2


def _dot_block_diag(x, w_ref):
    kx, kw = x.shape[1] // S5_SPLIT, w_ref.shape[1] // S5_SPLIT
    return jnp.concatenate([_dot(x[:, i * kx:(i + 1) * kx], w_ref[i * kx:(i + 1) * kx, i * kw:(i + 1) * kw])
                            for i in range(S5_SPLIT)], axis=-1)


def _s5_tail(u, hr, hi, ccr, cci, d, gw, gb):
    y = _dot_block_diag(hr.astype(BF16), ccr) - _dot_block_diag(hi.astype(BF16), cci) + d * u
    z = jax.nn.gelu(y)
    return z * _sigmoid(_dot(z.astype(BF16), gw) + gb)


def _s5_prompt_kernel(u_ref, bbr_ref, bbi_ref, lr_ref, li_ref, h0r_ref, h0i_ref, ccr_ref, cci_ref,
                      d_ref, gw_ref, gb_ref, y_ref, hr_out, hi_out, xr, xi, cr, ci):
    i = pl.program_id(1)
    tc = u_ref.shape[0]

    @pl.when(i == 0)
    def _():
        cr[...] = h0r_ref[0]
        ci[...] = h0i_ref[0]

    u = u_ref[...]
    ub = u.astype(BF16)
    xr[...] = _dot_block_diag(ub, bbr_ref)
    xi[...] = _dot_block_diag(ub, bbi_ref)
    lr, li = lr_ref[...], li_ref[...]

    def step(t, carry):
        hr, hi = carry
        nhr = lr * hr - li * hi + xr[pl.ds(t, 1), :]
        nhi = lr * hi + li * hr + xi[pl.ds(t, 1), :]
        xr[pl.ds(t, 1), :] = nhr
        xi[pl.ds(t, 1), :] = nhi
        return nhr, nhi

    hr, hi = lax.fori_loop(0, tc, step, (cr[...], ci[...]), unroll=8)
    cr[...] = hr
    ci[...] = hi
    y_ref[...] = _s5_tail(u, xr[...], xi[...], ccr_ref[...], cci_ref[...], d_ref[...], gw_ref[...], gb_ref[...])

    @pl.when(i == pl.num_programs(1) - 1)
    def _():
        hr_out[0] = hr
        hi_out[0] = hi


def _s5_prompt(u, b, t, h0r, h0i, prm):
    bbr, bbi, lr, li, ccr, cci, d, gw, gb = prm
    n, ch = u.shape
    ns = lr.shape[1]
    tc = min(512, t)
    nt = t // tc
    st = pl.BlockSpec((1, 1, ns), lambda bi, i: (bi, 0, 0))
    sd = jax.ShapeDtypeStruct
    return pl.pallas_call(
        _s5_prompt_kernel,
        grid=(b, nt),
        in_specs=[pl.BlockSpec((tc, ch), lambda bi, i: (bi * nt + i, 0)),
                  _const_spec(bbr.shape), _const_spec(bbi.shape), _const_spec(lr.shape), _const_spec(li.shape),
                  st, st, _const_spec(ccr.shape), _const_spec(cci.shape), _const_spec(d.shape),
                  _const_spec(gw.shape), _const_spec(gb.shape)],
        out_specs=[pl.BlockSpec((tc, ch), lambda bi, i: (bi * nt + i, 0)), st, st],
        out_shape=[sd((n, ch), F32), sd((b, 1, ns), F32), sd((b, 1, ns), F32)],
        scratch_shapes=[pltpu.VMEM((tc, ns), F32), pltpu.VMEM((tc, ns), F32),
                        pltpu.VMEM((1, ns), F32), pltpu.VMEM((1, ns), F32)],
        compiler_params=_params("parallel", "arbitrary"),
        name="s5_prompt",
    )(u, bbr, bbi, lr, li, h0r, h0i, ccr, cci, d, gw, gb)


def _s5_sample_kernel(u_ref, bbr_ref, bbi_ref, lr_ref, li_ref, h0r_ref, h0i_ref, ccr_ref, cci_ref,
                      d_ref, gw_ref, gb_ref, y_ref, hr_out, hi_out):
    lr, li = lr_ref[...], li_ref[...]
    hr, hi = h0r_ref[...], h0i_ref[...]
    for t in range(u_ref.shape[0]):
        u = u_ref[t]
        ub = u.astype(BF16)
        xr = _dot_block_diag(ub, bbr_ref)
        xi = _dot_block_diag(ub, bbi_ref)
        hr, hi = lr * hr - li * hi + xr, lr * hi + li * hr + xi
        y_ref[t] = _s5_tail(u, hr, hi, ccr_ref[...], cci_ref[...], d_ref[...], gw_ref[...], gb_ref[...])
    hr_out[...] = hr
    hi_out[...] = hi


def _s5_sample(u_tm, h0r, h0i, prm):
    bbr, bbi, lr, li, ccr, cci, d, gw, gb = prm
    sd = jax.ShapeDtypeStruct
    return pl.pallas_call(
        _s5_sample_kernel,
        out_shape=[sd(u_tm.shape, F32), sd(h0r.shape, F32), sd(h0i.shape, F32)],
        compiler_params=pltpu.CompilerParams(vmem_limit_bytes=VMEM_LIMIT_BYTES),
        name="s5_sample",
    )(u_tm, bbr, bbi, lr, li, h0r, h0i, ccr, cci, d, gw, gb)


def _dil_prompt_kernel(q_ref, kp_ref, kc_ref, vp_ref, vc_ref, o_ref, l_ref):
    n = pl.program_id(1)
    blk = q_ref.shape[1]
    q = q_ref[0] * (DH_C ** -0.5)
    kcat = jnp.concatenate([kp_ref[0], kc_ref[0]], axis=0).astype(BF16)
    vcat = jnp.concatenate([vp_ref[0], vc_ref[0]], axis=0).astype(BF16)
    qi = lax.broadcasted_iota(jnp.int32, (blk, 1), 0) + blk
    ki = lax.broadcasted_iota(jnp.int32, (1, 2 * blk), 1)
    dist = qi - ki
    lo_k = jnp.where(n > 0, 0, blk)
    ok = (dist >= 0) & (dist <= blk) & (ki >= lo_k)
    w = q.shape[1]
    pw = 2 * DH_C
    first = lax.broadcasted_iota(jnp.int32, (1, pw), 1) < DH_C
    ok2 = jnp.concatenate([ok, ok], axis=0)
    pairs = [slice(p * pw, (p + 1) * pw) for p in range(w // pw)]
    qbd = [jnp.concatenate([jnp.where(first, q[:, sl], 0.0), jnp.where(first, 0.0, q[:, sl])], axis=0).astype(BF16)
           for sl in pairs]
    s = [jnp.where(ok2, _dot_nt(qb, kcat[:, sl]), NEG) for qb, sl in zip(qbd, pairs)]
    m = [jnp.max(x, -1, keepdims=True) for x in s]
    pr = [jnp.exp(x - mm) for x, mm in zip(s, m)]
    den = [jnp.sum(x, -1, keepdims=True) for x in pr]
    oh = [_dot((x / d).astype(BF16), vcat[:, sl]) for x, d, sl in zip(pr, den, pairs)]
    lh = [mm + jnp.log(d) for mm, d in zip(m, den)]
    o_ref[0] = jnp.concatenate([jnp.where(first, x[:blk], x[blk:]) for x in oh], axis=-1)
    l_ref[0] = jnp.concatenate([jnp.where(first, x[:blk], x[blk:]) for x in lh], axis=-1)


def _dil_prompt(qd, kd, vd):
    bd, ns, w = qd.shape
    cur = pl.BlockSpec((1, C_BLK, w), lambda b, n: (b, n, 0))
    prev = pl.BlockSpec((1, C_BLK, w), lambda b, n: (b, jnp.maximum(n - 1, 0), 0))
    sd = jax.ShapeDtypeStruct((bd, ns, w), F32)
    return pl.pallas_call(
        _dil_prompt_kernel,
        grid=(bd, ns // C_BLK),
        in_specs=[cur, prev, cur, prev, cur],
        out_specs=[cur, cur],
        out_shape=[sd, sd],
        compiler_params=_params("parallel", "parallel"),
        name="dilated_attn_prompt",
    )(qd, kd, kd, vd, vd)


def _dil_sample_kernel(q_ref, kn_ref, vn_ref, kt_ref, vt_ref, o_ref):
    w = q_ref.shape[2]
    nh = w // DH_C
    buf = kt_ref.shape[2]
    rows = nh * SUBLANES
    qbd = _block_diag_queries(q_ref[0] * (DH_C ** -0.5), nh, DH_C)
    row_q = lax.broadcasted_iota(jnp.int32, (rows, 1), 0) % SUBLANES

    def reach(delta):
        mult = jnp.zeros(delta.shape, F32)
        for d in C_DILATIONS:
            ok = (delta >= 0) & (lax.rem(delta, d) == 0) & (delta <= d * C_BLK)
            mult = mult + jnp.where(ok, 1.0, 0.0)
        return mult

    mult = reach(buf + row_q - lax.broadcasted_iota(jnp.int32, (1, buf), 1))
    mult_n = reach(row_q - lax.broadcasted_iota(jnp.int32, (1, SUBLANES), 1))
    s = jnp.where(mult > 0, _dot(qbd, kt_ref[0].astype(BF16)), NEG)
    s_n = jnp.where(mult_n > 0, _dot_nt(qbd, kn_ref[0].astype(BF16)), NEG)
    m = jnp.maximum(jnp.max(s, -1, keepdims=True), jnp.max(s_n, -1, keepdims=True))
    p = mult * jnp.exp(s - m)
    p_n = mult_n * jnp.exp(s_n - m)
    inv = 1.0 / (jnp.sum(p, -1, keepdims=True) + jnp.sum(p_n, -1, keepdims=True))
    o = _dot_nt((p * inv).astype(BF16), vt_ref[0].astype(BF16)) + _dot((p_n * inv).astype(BF16), vn_ref[0].astype(BF16))
    lane_h = lax.broadcasted_iota(jnp.int32, (1, w), 1) // DH_C
    out = jnp.zeros((SUBLANES, w), F32)
    for h in range(nh):
        out = out + jnp.where(lane_h == h, o[h * SUBLANES:(h + 1) * SUBLANES, :], 0.0)
    o_ref[0] = out


def _dil_sample(q8, kn8, vn8, cache_kt, cache_vt):
    bs, w, buf = cache_kt.shape
    tok = pl.BlockSpec((1, SUBLANES, w), lambda b: (b, 0, 0))
    cache = pl.BlockSpec((1, w, buf), lambda b: (b, 0, 0))
    return pl.pallas_call(
        _dil_sample_kernel,
        grid=(bs,),
        in_specs=[tok, tok, tok, cache, cache],
        out_specs=tok,
        out_shape=jax.ShapeDtypeStruct((bs, SUBLANES, w), F32),
        compiler_params=_params("parallel"),
        name="dilated_attn_sample",
    )(q8, kn8, vn8, cache_kt, cache_vt)


def _softplus(x):
    return jnp.maximum(x, 0.0) + jnp.log1p(jnp.exp(-jnp.abs(x)))


def _rwkv_pre_kernel(pd_ref, pv_ref, mu_ref, w0_ref, w2_ref, a0_ref, a2_ref, g2_ref, kk_ref, ka_ref, rk_ref, seg_ref,
                     r_o, ld_o, k_o, v_o, kk_o, b_o, g_o, bonus_o):
    pd = pd_ref[...]
    xm = pd + (pv_ref[...] - pd) * mu_ref[...]
    o1, o2, o3 = D_W, 2 * D_W, 3 * D_W
    o5 = o3 + D_LORA_W + D_LORA_A
    r, k, v = xm[:, :o1], xm[:, o1:o2], xm[:, o2:o3]
    wa, gl = xm[:, o3:o5], xm[:, o5:]
    lw = _dot(jnp.tanh(wa).astype(BF16), w2_ref[...])
    la = _dot(wa.astype(BF16), a2_ref[...])
    g = _dot(_sigmoid(gl).astype(BF16), g2_ref[...])
    w_log = -_softplus(-(w0_ref[...] + lw)) - 0.5
    a = _sigmoid(a0_ref[...] + la)
    seg = seg_ref[...]
    kk = k * kk_ref[...]
    kk = kk / jnp.maximum(jnp.sqrt(_dot_exact_rhs(kk * kk, seg)), 1e-12)
    k2 = k * (1.0 + (a - 1.0) * ka_ref[...])
    r_o[...] = r
    ld_o[...] = -jnp.exp(w_log)
    k_o[...] = k2
    v_o[...] = v
    kk_o[...] = kk
    b_o[...] = kk * a
    g_o[...] = g
    bonus_o[...] = _dot_exact_rhs(r * k2 * rk_ref[...], seg) * v


def _rwkv_pre(pd, prev, prm):
    n, cols = pd.shape
    tm = min(TOKEN_TILE, n)
    tok = lambda wd: pl.BlockSpec((tm, wd), lambda i: (i, 0))
    return pl.pallas_call(
        _rwkv_pre_kernel,
        grid=(n // tm,),
        in_specs=[tok(cols), tok(cols)] + [_const_spec(p.shape) for p in prm],
        out_specs=[tok(D_W)] * 8,
        out_shape=[jax.ShapeDtypeStruct((n, D_W), F32)] * 8,
        compiler_params=_params("parallel"),
        name="rwkv_pre",
    )(pd, prev, *prm)


def _rwkv_chunk_kernel(r_ref, ld_ref, k_ref, v_ref, kk_ref, b_ref, y_ref, s_out, st):
    c = pl.program_id(1)
    nb, ch, w = r_ref.shape
    pw = 2 * N_D
    wide = lambda ref: jnp.concatenate([ref[i] for i in range(nb)], axis=-1)

    @pl.when(c == 0)
    def _():
        st[...] = jnp.zeros(st.shape, F32)

    ri = lax.broadcasted_iota(jnp.int32, (ch, ch), 0)
    ci = lax.broadcasted_iota(jnp.int32, (ch, ch), 1)
    tri_incl = ci <= ri
    tri_strict = ci < ri
    eye_c = jnp.where(ci == ri, 1.0, 0.0)
    ld = wide(ld_ref)
    r_w, k_w, b_w = wide(r_ref), wide(k_ref), wide(b_ref)
    cum = _dot_exact_rhs_left(jnp.where(tri_incl, 1.0, 0.0).astype(BF16), ld)
    cum_end = cum[ch - 1:ch, :]
    g_inc = jnp.exp(cum)
    g_inv = jnp.exp(-cum)
    g_end = jnp.exp(cum_end - cum)
    rho = r_w * g_inc
    kap = wide(kk_ref) * jnp.exp(cum - ld)
    kh = k_w * g_inv
    bh = b_w * g_inv
    khg = k_w * g_end
    bhg = b_w * g_end
    gam_end = jnp.exp(cum_end)
    v = wide(v_ref)

    pi = lax.broadcasted_iota(jnp.int32, (pw, pw), 0)
    pj = lax.broadcasted_iota(jnp.int32, (pw, pw), 1)
    same_head = (pi // N_D) == (pj // N_D)
    lane = lax.broadcasted_iota(jnp.int32, (1, pw), 1)
    first = lane < N_D

    npair = nb * w // pw
    heads = [(pr, hh) for pr in range(npair) for hh in range(2)]
    sls = [slice(pr * pw, (pr + 1) * pw) for pr in range(npair)]
    sel = (first, jnp.logical_not(first))
    bf = lambda x: x.astype(BF16)
    kap_p = [kap[:, s] for s in sls]
    rho_p = [rho[:, s] for s in sls]
    kap_b = [bf(x) for x in kap_p]
    kh_b = [bf(kh[:, s]) for s in sls]
    bh_b = [bf(bh[:, s]) for s in sls]
    v_b = [bf(v[:, s]) for s in sls]
    kap_m = [bf(jnp.where(sel[hh], kap_p[pr], 0.0)) for pr, hh in heads]
    rho_m = [bf(jnp.where(sel[hh], rho_p[pr], 0.0)) for pr, hh in heads]
    a_b = [jnp.where(tri_strict, _dot_nt(kap_m[i], bh_b[pr]), 0.0) for i, (pr, _) in enumerate(heads)]
    a_k = [bf(jnp.where(tri_strict, _dot_nt(kap_m[i], kh_b[pr]), 0.0)) for i, (pr, _) in enumerate(heads)]
    ap_b = [bf(jnp.where(tri_incl, _dot_nt(rho_m[i], bh_b[pr]), 0.0)) for i, (pr, _) in enumerate(heads)]
    ap_k = [bf(jnp.where(tri_incl, _dot_nt(rho_m[i], kh_b[pr]), 0.0)) for i, (pr, _) in enumerate(heads)]
    tl = [eye_c - a for a in a_b]
    pw2 = [_dot(bf(a), bf(a)) for a in a_b]
    span = 2
    while span < ch:
        tl = [t + _dot(bf(t), bf(p2)) for t, p2 in zip(tl, pw2)]
        span *= 2
        if span < ch:
            pw2 = [_dot(bf(p2), bf(p2)) for p2 in pw2]
    tl_b = [bf(t) for t in tl]
    kap2_h = [_dot(tl_b[i], kap_b[pr]) for i, (pr, _) in enumerate(heads)]
    akv = [bf(_dot(a_k[i], v_b[pr])) for i, (pr, _) in enumerate(heads)]
    wr_h = [_dot(tl_b[i], akv[i]) for i in range(len(heads))]
    rho2_h = [rho_p[pr] - _dot(ap_b[i], bf(kap2_h[i])) for i, (pr, _) in enumerate(heads)]
    yloc_h = [_dot(ap_k[i], v_b[pr]) - _dot(ap_b[i], bf(wr_h[i])) for i, (pr, _) in enumerate(heads)]
    pair = lambda xs: [jnp.where(first, xs[2 * pr], xs[2 * pr + 1]) for pr in range(npair)]
    kap2, wr, rho2, yloc = pair(kap2_h), pair(wr_h), pair(rho2_h), pair(yloc_h)
    bhg_b = [bf(bhg[:, s]) for s in sls]
    khg_b = [bf(khg[:, s]) for s in sls]
    diag = [jnp.where(pi == pj, jnp.broadcast_to(gam_end[:, s], (pw, pw)), 0.0) for s in sls]
    phi = [jnp.where(same_head, diag[pr] - _dot_tn(bhg_b[pr], bf(kap2[pr])), 0.0) for pr in range(npair)]
    gmat = [jnp.where(same_head, _dot_tn(khg_b[pr], v_b[pr]) - _dot_tn(bhg_b[pr], bf(wr[pr])), 0.0)
            for pr in range(npair)]
    s_b = [bf(st[pr]) for pr in range(npair)]
    per_b = w // pw
    for pr in range(npair):
        y_ref[pr // per_b, :, (pr % per_b) * pw:(pr % per_b + 1) * pw] = _dot(bf(rho2[pr]), s_b[pr]) + yloc[pr]
    for pr in range(npair):
        st[pr] = _dot(bf(phi[pr]), s_b[pr]) + gmat[pr]

    @pl.when(c == pl.num_programs(1) - 1)
    def _():
        for pr in range(npair):
            s_out[pr // per_b, pr % per_b] = st[pr]


def _dot_exact_rhs_left(m01, x):
    hi, mid, lo = _split3(x)
    return _dot(m01, hi) + _dot(m01, mid) + _dot(m01, lo)


def _rwkv_chunk(r, ld, k2, v, kk, bb, b, t):
    n, w = r.shape
    ch = min(RWKV_CHUNK, t)
    nc = t // ch
    npair = w // (2 * N_D)
    nb = 2 if b % 2 == 0 else 1
    tok = pl.BlockSpec((nb, ch, w), lambda bi, c: (bi, c, 0))
    seq = lambda z: z.reshape(b, t, w)
    y, s_pairs = pl.pallas_call(
        _rwkv_chunk_kernel,
        grid=(b // nb, nc),
        in_specs=[tok] * 6,
        out_specs=[tok, pl.BlockSpec((nb, npair, 2 * N_D, 2 * N_D), lambda bi, c: (bi, 0, 0, 0))],
        out_shape=[jax.ShapeDtypeStruct((b, t, w), F32), jax.ShapeDtypeStruct((b, npair, 2 * N_D, 2 * N_D), F32)],
        scratch_shapes=[pltpu.VMEM((nb * npair, 2 * N_D, 2 * N_D), F32)],
        compiler_params=_params("parallel", "arbitrary"),
        name="rwkv_chunk_scan",
    )(seq(r), seq(ld), seq(k2), seq(v), seq(kk), seq(bb))
    return y.reshape(n, w), s_pairs


def _rwkv_lane_kernel(r_ref, ld_ref, k_ref, v_ref, kk_ref, b_ref, s_ref, y_ref, s_out):
    steps = r_ref.shape[0]

    def body(vi, carry):
        s = s_ref[0, vi]
        for t in range(steps):
            sk = jnp.sum(s * kk_ref[t, 0], axis=0, keepdims=True)
            vv = v_ref[t, 0, pl.ds(vi, 1), :]
            s = s * jnp.exp(ld_ref[t, 0]) - sk * b_ref[t, 0] + vv * k_ref[t, 0]
            y_ref[t, 0, pl.ds(vi, 1), :] = jnp.sum(s * r_ref[t, 0], axis=0, keepdims=True)
        s_out[0, vi] = s
        return carry

    lax.fori_loop(0, s_ref.shape[1], body, 0)


def _rwkv_lane(rt, ldt, kt, vt, kkt, bt, s0):
    steps, nh, nd, bs = rt.shape
    tok = pl.BlockSpec((steps, 1, nd, bs), lambda h: (0, h, 0, 0))
    stt = pl.BlockSpec((1, nd, nd, bs), lambda h: (h, 0, 0, 0))
    return pl.pallas_call(
        _rwkv_lane_kernel,
        grid=(nh,),
        in_specs=[tok] * 6 + [stt],
        out_specs=[tok, stt],
        out_shape=[jax.ShapeDtypeStruct(rt.shape, F32), jax.ShapeDtypeStruct(s0.shape, F32)],
        compiler_params=_params("parallel"),
        name="rwkv_lane_scan",
    )(rt, ldt, kt, vt, kkt, bt, s0)


def _rwkv_post_kernel(y_ref, bonus_ref, g_ref, gw_ref, gb_ref, seg_ref, o_ref):
    y = y_ref[...]
    seg = seg_ref[...]
    mu = _dot_exact_rhs(y, seg) * (1.0 / N_D)
    yc = y - mu
    var = _dot_exact_rhs(yc * yc, seg) * (1.0 / N_D)
    yn = yc * lax.rsqrt(var + GN_EPS) * gw_ref[...] + gb_ref[...]
    o_ref[...] = (yn + bonus_ref[...]) * g_ref[...]


def _rwkv_post(y, bonus, g, gn_w, gn_b, seg):
    n, w = y.shape
    tm = min(TOKEN_TILE, n)
    tok = pl.BlockSpec((tm, w), lambda i: (i, 0))
    return pl.pallas_call(
        _rwkv_post_kernel,
        grid=(n // tm,),
        in_specs=[tok, tok, tok, _const_spec((1, w)), _const_spec((1, w)), _const_spec(seg.shape)],
        out_specs=tok,
        out_shape=jax.ShapeDtypeStruct((n, w), F32),
        compiler_params=_params("parallel"),
        name="rwkv_post",
    )(y, bonus, g, gn_w.reshape(1, w), gn_b.reshape(1, w), seg)


def _pad_tokens(x, bs, s_len):
    x = x.reshape(bs, s_len, x.shape[-1])
    return jnp.pad(x, ((0, 0), (0, SUBLANES - s_len), (0, 0)))


def _block_diag_in(bb):
    g, c, p = bb.shape
    return jnp.einsum('gcp,gh->gchp', bb, jnp.eye(g, dtype=bb.dtype)).reshape(g * c, g * p)


def _block_diag_out(cc):
    g, c, p = cc.shape
    return jnp.einsum('gcp,gh->gphc', cc, jnp.eye(g, dtype=cc.dtype)).reshape(g * p, g * c)


def _shifted(pd, shift0, b, t):
    pd3 = pd.reshape(b, t, pd.shape[-1])
    return jnp.concatenate([shift0[:, None, :], pd3[:, :-1]], axis=1).reshape(b * t, pd.shape[-1])


def _trunk(x, mod_all, row0, rows, per_token_mod, sample, st, p):
    b, t, d = x.shape
    n = b * t
    xt = x.reshape(n, d)
    tm = min(TOKEN_TILE, n)
    outs = {}
    for l in range(DEPTH):
        mod = mod_all[l, row0:row0 + rows].reshape(rows, 9, d)

        def mods(i):
            sel = [mod[:, 3 * i + j] for j in range(3)]
            if per_token_mod:
                return [jnp.repeat(m, t, axis=0).reshape(n // tm, tm, d) for m in sel], 1
            return [m.reshape(b, 1, d) for m in sel], t // tm

        m0, tpg = mods(0)
        xt = _ffn(xt, m0, tpg, p['wg'][l][0], p['wu'][l][0], p['wd'][l][0], p['ln_g'][l, 0], p['ln_b'][l, 0])
        m1, _ = mods(1)
        if l % 2 == 0:
            e = l // 2
            lam_init = 0.8 - 0.6 * math.exp(-0.3 * l)
            widths = (512, 512, 512, S5_CH)
            dl, subln = p['diff_lambda'][e], p['diff_subln'][e]
            if sample:
                q, k, v, u = _inproj(xt, m1, tpg, p['even_w_in'][e], widths)
            else:
                q, k, v, u, vt, kt = _inproj(xt, m1, tpg, p['even_w_in'][e], widths,
                                             t_weights=((p['even_wv_t'][e], True), (p['even_wk_t'][e], False)),
                                             t_block=min(256, t), seq_tiles=t // tm)
            if sample:
                att8 = _diff_sample(_pad_tokens(q, b, t), _pad_tokens(k, b, t), _pad_tokens(v, b, t),
                                    st['pool_k'][e], st['pool_v'][e], st['page_table'], dl, subln, lam_init)
                att = att8[:, :t].reshape(n, -1)
                u_tm = u.reshape(b, t, -1).transpose(1, 0, 2)
                y_tm, hr, hi = _s5_sample(u_tm, st['s5_re'][e].reshape(b, S5_N), st['s5_im'][e].reshape(b, S5_N),
                                          p['s5'][e])
                y5 = y_tm.transpose(1, 0, 2).reshape(n, -1)
            else:
                att = _diff_prompt(q, k, vt, dl, subln, b, t, lam_init)
                zero = jnp.zeros((b, 1, S5_N), F32)
                y5, hr, hi = _s5_prompt(u, b, t, zero, zero, p['s5'][e])
            if sample:
                k_out = k.reshape(b, t, H_A, 2, DH_A)
            else:
                k_out = kt.reshape(b, H_A, 2, DH_A, t).transpose(0, 4, 1, 2, 3)
            outs.setdefault('ak', []).append(k_out)
            outs.setdefault('av', []).append(v.reshape(b, t, H_A, 2 * DH_A))
            outs.setdefault('s5r', []).append(hr.reshape(b, S5_G, S5_P))
            outs.setdefault('s5i', []).append(hi.reshape(b, S5_G, S5_P))
            xt = _outproj(xt, m1[2], tpg, [att], y5, p['even_w_out'][e], p['ln_g'][l, 1], p['ln_b'][l, 1])
        else:
            o = l // 2
            widths = (512, 512, 512, pd_cols(p))
            rp = p['rwkv'][o]
            dil_of = None
            if sample:
                q, k, v, pd = _inproj(xt, m1, tpg, p['odd_w_in'][o], widths)
            else:
                dils = C_DILATIONS[1:]
                res = _inproj(xt, m1, tpg, p['odd_w_in'][o], widths,
                              t_weights=((p['odd_wk_t'][o], False), (p['odd_wv_t'][o], False)),
                              seq_tiles=t // tm, n_dil=3, dils=dils)
                q, k, v, pd = res[:4]
                split = {dd: res[4 + 3 * i:7 + 3 * i] for i, dd in enumerate(dils)}
                kt, vt = res[4 + 3 * len(dils):]
            if sample:
                att8 = _dil_sample(_pad_tokens(q, b, t), _pad_tokens(k, b, t), _pad_tokens(v, b, t),
                                   st['cache_c_k'][o], st['cache_c_v'][o])
                atts = [att8[:, :t].reshape(n, -1)]
                prev = _shifted(pd, st['d_shift'][o], b, t)
                r, ld, k2, vv, kk, bb, g, bonus = _rwkv_pre(pd, prev, rp['pre'])
                tl = lambda z: z.reshape(b, t, H_D, N_D).transpose(1, 2, 3, 0)
                s0 = st['d_wkv'][o].transpose(1, 2, 3, 0)
                y_l, s_new = _rwkv_lane(tl(r), tl(ld), tl(k2), tl(vv), tl(kk), tl(bb), s0)
                y = y_l.transpose(3, 0, 1, 2).reshape(n, D_W)
                s_new = s_new.transpose(3, 0, 1, 2)
                k_keep, v_keep = k.reshape(b, t, H_C, DH_C), v.reshape(b, t, H_C, DH_C)
            else:
                atts_o, atts_l = [], []
                for dil in C_DILATIONS:
                    if dil == 1:
                        ob, lb = _dil_prompt(q.reshape(b, t, -1), k.reshape(b, t, -1), v.reshape(b, t, -1))
                        ob, lb = ob.reshape(n, -1), lb.reshape(n, -1)
                    else:
                        ob, lb = _dil_prompt(*[z.reshape(b * dil, t // dil, -1) for z in split[dil]])
                        ob, lb = ob.reshape(b, dil, t // dil, -1), lb.reshape(b, dil, t // dil, -1)
                    atts_o.append(ob)
                    atts_l.append(lb)
                atts = atts_o + atts_l
                dil_of = C_DILATIONS + C_DILATIONS
                prev = _shifted(pd, jnp.zeros((b, pd.shape[-1]), F32), b, t)
                r, ld, k2, vv, kk, bb, g, bonus = _rwkv_pre(pd, prev, rp['pre'])
                y, s_pairs = _rwkv_chunk(r, ld, k2, vv, kk, bb, b, t)
                sp = s_pairs.reshape(b, H_D // 2, 2, N_D, 2, N_D)
                s_new = jnp.stack([sp[:, :, 0, :, 0, :], sp[:, :, 1, :, 1, :]], axis=2)
                s_new = s_new.reshape(b, H_D, N_D, N_D).transpose(0, 1, 3, 2)
                keep = min(C_BLK * C_DILATIONS[-1], t)
                keep_t = lambda zt: zt[:, :, t - keep:].reshape(b, H_C, DH_C, keep).transpose(0, 3, 1, 2)
                k_keep, v_keep = keep_t(kt), keep_t(vt)
            yd = _rwkv_post(y, bonus, g, rp['gn_w'], rp['gn_b'], rp['seg'])
            outs.setdefault('ck', []).append(k_keep)
            outs.setdefault('cv', []).append(v_keep)
            outs.setdefault('dw', []).append(s_new)
            outs.setdefault('ds', []).append(pd.reshape(b, t, -1)[:, -1])
            xt = _outproj(xt, m1[2], tpg, atts, yd, p['odd_w_out'][o], p['ln_g'][l, 1], p['ln_b'][l, 1],
                          dil_of=dil_of, seq_tiles=t // tm)
        m2, _ = mods(2)
        xt = _ffn(xt, m2, tpg, p['wg'][l][1], p['wu'][l][1], p['wd'][l][1], p['ln_g'][l, 2], p['ln_b'][l, 2])
    stacked = [jnp.stack(outs[key], 0) for key in ('ak', 'av', 's5r', 's5i', 'ck', 'cv', 'dw', 'ds')]
    return xt.reshape(b, t, d), stacked


def pd_cols(p):
    return p['odd_w_in'].shape[-1] - 3 * H_C * DH_C


def kernel(x_prompt, x_sample, cache_a_k, cache_a_v, state_s5_re, state_s5_im, cache_c_k, cache_c_v, state_d_wkv, state_d_shift, page_table, c_prompt, c_sample, ada_w, ada_b, ln_g, ln_b, ffn_w_gate, ffn_w_up, ffn_w_down, even_w_in, even_w_out, diff_lambda, diff_subln, s5_a_re, s5_a_im, s5_log_dt, s5_b_re, s5_b_im, s5_c_re, s5_c_im, s5_d, s5_glu_w, s5_glu_b, odd_w_in, odd_w_out, rwkv_mu, rwkv_w0, rwkv_w2, rwkv_a0, rwkv_a2, rwkv_g2, rwkv_k_k, rwkv_k_a, rwkv_r_k, rwkv_gn_w, rwkv_gn_b):
    bp, bs = x_prompt.shape[0], x_sample.shape[0]
    n_even, n_odd = even_w_in.shape[0], odd_w_in.shape[0]
    bf = lambda w: w.astype(BF16)

    seg = jnp.kron(jnp.eye(H_D, dtype=F32), jnp.ones((N_D, N_D), F32)).astype(BF16)
    s5 = []
    for e in range(n_even):
        lr, li, bbr, bbi = _s5_prep(s5_a_re[e], s5_a_im[e], s5_log_dt[e], s5_b_re[e], s5_b_im[e])
        s5.append((bf(_block_diag_in(bbr)), bf(_block_diag_in(bbi)), lr.reshape(1, S5_N), li.reshape(1, S5_N),
                   bf(_block_diag_out(s5_c_re[e])), bf(_block_diag_out(s5_c_im[e])),
                   s5_d[e].reshape(1, S5_CH), bf(s5_glu_w[e]), s5_glu_b[e].reshape(1, S5_CH)))
    rwkv = []
    for o in range(n_odd):
        row = lambda z: z.reshape(1, -1)
        w2p = jnp.concatenate([rwkv_w2[o], jnp.zeros_like(rwkv_a2[o])], axis=0)
        a2p = jnp.concatenate([jnp.zeros_like(rwkv_w2[o]), rwkv_a2[o]], axis=0)
        pre = (row(rwkv_mu[o]), row(rwkv_w0[o]), bf(w2p), row(rwkv_a0[o]), bf(a2p), bf(rwkv_g2[o]),
               row(rwkv_k_k[o]), row(rwkv_k_a[o]), row(rwkv_r_k[o]), seg)
        rwkv.append(dict(pre=pre, gn_w=rwkv_gn_w[o], gn_b=rwkv_gn_b[o], seg=seg))
    p = dict(wg=bf(ffn_w_gate), wu=bf(ffn_w_up), wd=bf(ffn_w_down), ln_g=ln_g, ln_b=ln_b,
             even_w_in=bf(even_w_in), even_w_out=bf(even_w_out), odd_w_in=bf(odd_w_in), odd_w_out=bf(odd_w_out),
             even_wv_t=bf(jnp.swapaxes(even_w_in[:, :, 2 * A_W:3 * A_W], 1, 2)),
             even_wk_t=bf(jnp.swapaxes(even_w_in[:, :, A_W:2 * A_W], 1, 2)),
             odd_wk_t=bf(jnp.swapaxes(odd_w_in[:, :, C_W:2 * C_W], 1, 2)),
             odd_wv_t=bf(jnp.swapaxes(odd_w_in[:, :, 2 * C_W:3 * C_W], 1, 2)),
             diff_lambda=diff_lambda, diff_subln=diff_subln, s5=s5, rwkv=rwkv)

    mod_all = _ada(jnp.concatenate([c_prompt, c_sample], axis=0), ada_w, ada_b)

    y_prompt, st_p = _trunk(x_prompt, mod_all, 0, bp, False, False, None, p)
    n_pool, page = cache_a_k.shape[1], cache_a_k.shape[2]
    win_buf = cache_c_k.shape[2]
    pos_minor = lambda c: jnp.transpose(c, (0, 1, 3, 4, 2)).reshape(n_odd, bs, -1, win_buf)
    st = dict(pool_k=jnp.transpose(cache_a_k, (0, 1, 3, 4, 5, 2)).reshape(n_even, n_pool, -1, page),
              pool_v=cache_a_v,
              page_table=page_table, s5_re=state_s5_re, s5_im=state_s5_im,
              cache_c_k=pos_minor(cache_c_k), cache_c_v=pos_minor(cache_c_v),
              d_wkv=state_d_wkv, d_shift=state_d_shift)
    y_sample, st_s = _trunk(x_sample, mod_all, bp, bs, True, True, st, p)
    a_k_p, a_v_p, s5_re_p, s5_im_p, c_k_p, c_v_p, d_wkv_p, d_shift_p = st_p
    a_k_s, a_v_s, s5_re_s, s5_im_s, c_k_s, c_v_s, d_wkv_s, d_shift_s = st_s
    return (y_prompt, y_sample, a_k_p, a_k_s, a_v_p, a_v_s, s5_re_p, s5_re_s, s5_im_p, s5_im_s,
            c_k_p, c_k_s, c_v_p, c_v_s, d_wkv_p, d_wkv_s, d_shift_p, d_shift_s)
```

```python
import functools
import math

import jax
import jax.numpy as jnp
from jax import lax
from jax.experimental import pallas as pl
from jax.experimental.pallas import tpu as pltpu

F32 = jnp.float32
BF16 = jnp.bfloat16

DEPTH = 2
H_A, DH_A = 4, 64
A_W = H_A * 2 * DH_A
S5_GROUP, S5_G, S5_P = 16, 32, 64
S5_CH = S5_GROUP * S5_G
S5_N = S5_G * S5_P
H_C, DH_C = 8, 64
C_W = H_C * DH_C
C_BLK = 128
C_DILATIONS = (1, 4, 16)
H_D, N_D = 8, 64
D_W = H_D * N_D
D_LORA_W, D_LORA_A, D_LORA_G = 64, 64, 128
GN_EPS = 64e-5
ALPHA = (2.0 * DEPTH) ** 0.25
LN_EPS = 1e-5
NEG = -1e30

LANES = 128
SUBLANES = 8
VMEM_LIMIT_BYTES = 56 * 1024 * 1024
TOKEN_TILE = 512
RWKV_CHUNK = 64


def _params(*sem):
    return pltpu.CompilerParams(dimension_semantics=sem, vmem_limit_bytes=VMEM_LIMIT_BYTES)


def _const_spec(shape):
    nd = len(shape)
    return pl.BlockSpec(shape, lambda *_: (0,) * nd, pipeline_mode=pl.Buffered(1))


def _dot(a, b):
    return jnp.dot(a, b, preferred_element_type=F32)


def _dot_nt(a, b):
    return lax.dot_general(a, b, (((1,), (1,)), ((), ())), preferred_element_type=F32)


def _dot_tn(a, b):
    return lax.dot_general(a, b, (((0,), (0,)), ((), ())), preferred_element_type=F32)


def _split3(x):
    hi = x.astype(BF16)
    r1 = x - hi.astype(F32)
    mid = r1.astype(BF16)
    lo = (r1 - mid.astype(F32)).astype(BF16)
    return hi, mid, lo


def _dot_exact_rhs(x, m01):
    hi, mid, lo = _split3(x)
    return _dot(hi, m01) + _dot(mid, m01) + _dot(lo, m01)


def _layer_norm(y, g, b):
    mu = jnp.mean(y, -1, keepdims=True)
    yc = y - mu
    var = jnp.mean(yc * yc, -1, keepdims=True)
    return yc * lax.rsqrt(var + LN_EPS) * g + b


def _sigmoid(x):
    return jax.nn.sigmoid(x)


class _Mod:
    def __init__(self, arr, spec):
        self.arr, self.spec = arr, spec


def _mod_spec(mod, tiles_per_group):
    del tiles_per_group
    return mod.spec


def _ada_kernel(c_ref, w_ref, b_ref, o_ref):
    c = c_ref[...]
    s = (c * _sigmoid(c)).astype(BF16)
    o_ref[0] = _dot(s, w_ref[0].astype(BF16)) + b_ref[0]


def _ada(c_all, ada_w, ada_b):
    nl, d, w = ada_w.shape
    r = c_all.shape[0]
    tn = 1152 if w % 1152 == 0 else w
    return pl.pallas_call(
        _ada_kernel,
        grid=(nl, w // tn),
        in_specs=[pl.BlockSpec((r, d), lambda l, j: (0, 0)),
                  pl.BlockSpec((1, d, tn), lambda l, j: (l, 0, j)),
                  pl.BlockSpec((1, 1, tn), lambda l, j: (l, 0, j))],
        out_specs=pl.BlockSpec((1, r, tn), lambda l, j: (l, 0, j)),
        out_shape=jax.ShapeDtypeStruct((nl, r, w), F32),
        compiler_params=_params("parallel", "parallel"),
        name="ada_mod",
    )(c_all, ada_w, ada_b.reshape(nl, 1, w))


def _ffn_kernel(x_ref, sh_ref, sc_ref, gt_ref, wg_ref, wu_ref, wd_ref, g_ref, b_ref, o_ref):
    x = x_ref[...]
    h = (x * (1.0 + sc_ref[0]) + sh_ref[0]).astype(BF16)
    g = _dot(h, wg_ref[...])
    u = _dot(h, wu_ref[...])
    a = (g * _sigmoid(g) * u).astype(BF16)
    f = _dot(a, wd_ref[...])
    y = ALPHA * x + 0.5 * (1.0 + gt_ref[0]) * f
    o_ref[...] = _layer_norm(y, g_ref[...], b_ref[...])


def _ffn(x, mods, tpg, wg, wu, wd, ln_g, ln_b):
    n, d = x.shape
    f = wg.shape[1]
    tm = min(TOKEN_TILE, n)
    sh, sc, gt = mods
    return pl.pallas_call(
        _ffn_kernel,
        grid=(n // tm,),
        in_specs=[pl.BlockSpec((tm, d), lambda i: (i, 0)),
                  _mod_spec(sh, tpg), _mod_spec(sc, tpg), _mod_spec(gt, tpg),
                  _const_spec((d, f)), _const_spec((d, f)), _const_spec((f, d)),
                  _const_spec((1, d)), _const_spec((1, d))],
        out_specs=pl.BlockSpec((tm, d), lambda i: (i, 0)),
        out_shape=jax.ShapeDtypeStruct((n, d), F32),
        compiler_params=_params("parallel"),
        name="ffn_ln",
    )(x, sh.arr, sc.arr, gt.arr, wg, wu, wd, ln_g.reshape(1, d), ln_b.reshape(1, d))


def _inproj_kernel(x_ref, sh_ref, sc_ref, w_ref, *refs, widths, t_blocked, dils, n_dil):
    n_t = len(t_blocked)
    n_split = n_dil * len(dils)
    wt_refs, outs = refs[:n_t], refs[n_t:]
    o_refs = outs[:len(widths)]
    dil_refs = outs[len(widths):len(widths) + n_split]
    t_refs = outs[len(widths) + n_split:len(widths) + n_split + n_t]
    tm = x_ref.shape[0]
    h = (x_ref[...] * (1.0 + sc_ref[0]) + sh_ref[0]).astype(BF16)
    p = _dot(h, w_ref[...])
    off = 0
    for o_ref, wd in zip(o_refs, widths):
        o_ref[...] = p[:, off:off + wd].astype(o_ref.dtype)
        off += wd
    if n_split:
        stage = refs[-1]
        off = 0
        for s in range(n_dil):
            for c in range(widths[s] // LANES):
                stage[s, c] = p[:, off + c * LANES:off + (c + 1) * LANES]
            off += widths[s]
        for di, d in enumerate(dils):
            for s in range(n_dil):
                for r in range(d):
                    for c in range(widths[s] // LANES):
                        dst = dil_refs[di * n_dil + s]
                        dst[0, r, :, c * LANES:(c + 1) * LANES] = (
                            stage[s, c, pl.ds(r, tm // d, stride=d), :].astype(dst.dtype))
    for wt_ref, t_ref, blocked in zip(wt_refs, t_refs, t_blocked):
        vt = _dot_nt(wt_ref[...], h)
        if blocked:
            blk = t_ref.shape[2]
            for c in range(t_ref.shape[0]):
                t_ref[c] = vt[:, c * blk:(c + 1) * blk]
        else:
            t_ref[0] = vt


def _inproj(x, mods, tpg, w, widths, *, t_weights=(), t_block=None, seq_tiles=None, n_dil=0, dils=(),
            dil_dtype=F32):
    n, d = x.shape
    tm = min(TOKEN_TILE, n)
    sh, sc, _ = mods
    in_specs = [pl.BlockSpec((tm, d), lambda i: (i, 0)), _mod_spec(sh, tpg), _mod_spec(sc, tpg), _const_spec(w.shape)]
    out_specs = [pl.BlockSpec((tm, wd), lambda i: (i, 0)) for wd in widths]
    out_shape = [jax.ShapeDtypeStruct((n, wd), dil_dtype if s < n_dil else F32) for s, wd in enumerate(widths)]
    args = [x, sh.arr, sc.arr, w]
    nb = n // (seq_tiles * tm) if seq_tiles else None
    for dd in dils:
        for s in range(n_dil):
            out_specs.append(pl.BlockSpec((1, dd, tm // dd, widths[s]),
                                          lambda i: (i // seq_tiles, 0, i % seq_tiles, 0)))
            out_shape.append(jax.ShapeDtypeStruct((nb, dd, seq_tiles * tm // dd, widths[s]), dil_dtype))
    for wt, blocked in t_weights:
        in_specs.append(_const_spec(wt.shape))
        args.append(wt)
        if blocked:
            out_specs.append(pl.BlockSpec((tm // t_block, wt.shape[0], t_block), lambda i: (i, 0, 0)))
            out_shape.append(jax.ShapeDtypeStruct((n // t_block, wt.shape[0], t_block), F32))
        else:
            out_specs.append(pl.BlockSpec((1, wt.shape[0], tm), lambda i: (i // seq_tiles, 0, i % seq_tiles)))
            out_shape.append(jax.ShapeDtypeStruct((nb, wt.shape[0], seq_tiles * tm), F32))
    return pl.pallas_call(
        functools.partial(_inproj_kernel, widths=widths, t_blocked=tuple(bl for _, bl in t_weights),
                          dils=tuple(dils), n_dil=n_dil),
        grid=(n // tm,),
        in_specs=in_specs,
        out_specs=out_specs,
        out_shape=out_shape,
        scratch_shapes=[pltpu.VMEM((n_dil, widths[0] // LANES, tm, LANES), F32)] if n_dil and dils else [],
        compiler_params=_params("parallel"),
        name="in_proj",
    )(*args)


def _outproj_kernel(x_ref, gt_ref, *refs, n_branch, dil_of):
    n_att = len(dil_of)
    att_refs = refs[:n_att]
    y_ref, w_ref, g_ref, b_ref, o_ref = refs[n_att:n_att + 5]
    scratch = list(refs[n_att + 5:])
    tm = x_ref.shape[0]
    vals = []
    for ref, d in zip(att_refs, dil_of):
        if d == 1:
            vals.append(ref[...])
        else:
            buf = scratch.pop(0)
            for r in range(d):
                for c in range(buf.shape[0]):
                    buf[c, pl.ds(r, tm // d, stride=d), :] = ref[0, r, :, c * LANES:(c + 1) * LANES]
            vals.append(jnp.concatenate([buf[c] for c in range(buf.shape[0])], axis=-1))
    if n_branch > 1:
        outs = vals[:n_branch]
        lses = vals[n_branch:]
        m = functools.reduce(jnp.maximum, lses)
        ws = [jnp.exp(l - m) for l in lses]
        den = functools.reduce(lambda a, b: a + b, ws)
        att = functools.reduce(lambda a, b: a + b, [w * o for w, o in zip(ws, outs)]) / den
    else:
        att = vals[0]
    half = att.shape[1]
    mix = _dot(att.astype(BF16), w_ref[:half, :]) + _dot(y_ref[...].astype(BF16), w_ref[half:, :])
    y = ALPHA * x_ref[...] + (1.0 + gt_ref[0]) * mix
    o_ref[...] = _layer_norm(y, g_ref[...], b_ref[...])


def _outproj(x, gate, tpg, atts, y, w, ln_g, ln_b, dil_of=None, seq_tiles=None):
    n, d = x.shape
    tm = min(TOKEN_TILE, n)
    half = y.shape[1]
    dil_of = tuple(dil_of) if dil_of else (1,) * len(atts)
    n_branch = len(atts) // 2 if len(atts) > 1 else 1
    tok = lambda wd: pl.BlockSpec((tm, wd), lambda i: (i, 0))
    split = lambda dd: pl.BlockSpec((1, dd, tm // dd, half), lambda i: (i // seq_tiles, 0, i % seq_tiles, 0))
    return pl.pallas_call(
        functools.partial(_outproj_kernel, n_branch=n_branch, dil_of=dil_of),
        grid=(n // tm,),
        in_specs=[tok(d), _mod_spec(gate, tpg)] + [tok(half) if dd == 1 else split(dd) for dd in dil_of]
        + [tok(half), _const_spec(w.shape), _const_spec((1, d)), _const_spec((1, d))],
        out_specs=tok(d),
        out_shape=jax.ShapeDtypeStruct((n, d), F32),
        scratch_shapes=[pltpu.VMEM((half // LANES, tm, LANES), F32) for dd in dil_of if dd > 1],
        compiler_params=_params("parallel"),
        name="out_proj_ln",
    )(x, gate.arr, *atts, y, w, ln_g.reshape(1, d), ln_b.reshape(1, d))


def _diff_lambda(dl_ref, lam_init):
    lp = dl_ref[...]
    a = jnp.sum(lp[0:1] * lp[1:2], axis=-1, keepdims=True)
    b = jnp.sum(lp[2:3] * lp[3:4], axis=-1, keepdims=True)
    return jnp.exp(a) - jnp.exp(b) + lam_init


def _head_rms(o, g, lam_init):
    return o * lax.rsqrt(jnp.mean(o * o, -1, keepdims=True) + LN_EPS) * g * (1.0 - lam_init)


def _diff_prompt_kernel(dl_ref, q_ref, k_ref, vt_ref, g_ref, o_ref, *, tq, lam_init):
    i = pl.program_id(1)
    lam = _diff_lambda(dl_ref, lam_init)
    hw = 2 * DH_A
    n_heads = q_ref.shape[1] // hw
    nm = 2 * n_heads
    lo = lax.broadcasted_iota(jnp.int32, (1, hw), 1) < DH_A
    qs = []
    for h in range(n_heads):
        q = q_ref[:, h * hw:(h + 1) * hw] * (DH_A ** -0.5 * math.log2(math.e))
        qs += [jnp.where(lo, q, 0.0).astype(BF16), jnp.where(lo, 0.0, q).astype(BF16)]
    head = lambda x: slice((x // 2) * hw, (x // 2 + 1) * hw)

    def block(j, carry, diagonal):
        ms, ls, accs = carry[:nm], carry[nm:2 * nm], carry[2 * nm:]
        kb = k_ref[pl.ds(pl.multiple_of(j * tq, tq), tq), :].astype(BF16)
        vt = vt_ref[j].astype(BF16)
        ss = [_dot_nt(kb[:, head(x)], qs[x]) for x in range(nm)]
        if diagonal:
            ok = (lax.broadcasted_iota(jnp.int32, (tq, 1), 0) <= lax.broadcasted_iota(jnp.int32, (1, tq), 1))
            ss = [jnp.where(ok, s, NEG) for s in ss]
        ns = [jnp.maximum(m, jnp.max(s, 0, keepdims=True)) for m, s in zip(ms, ss)]
        ps = [jnp.exp2(s - n) for s, n in zip(ss, ns)]
        cs = [jnp.exp2(m - n) for m, n in zip(ms, ns)]
        ls = [c * l + jnp.sum(p, 0, keepdims=True) for c, l, p in zip(cs, ls, ps)]
        accs = [cs[x] * accs[x] + _dot(vt[head(x), :], ps[x].astype(BF16)) for x in range(nm)]
        return tuple(ns) + tuple(ls) + tuple(accs)

    init = ((jnp.full((1, tq), NEG, F32),) * nm + (jnp.zeros((1, tq), F32),) * nm
            + (jnp.zeros((hw, tq), F32),) * nm)
    c = lax.fori_loop(0, i, lambda j, c: block(j, c, False), init)
    fin = block(i, c, True)
    ls, accs = fin[nm:2 * nm], fin[2 * nm:]
    outs = []
    for h in range(n_heads):
        o = (accs[2 * h] / ls[2 * h] - lam * (accs[2 * h + 1] / ls[2 * h + 1])).T
        outs.append(_head_rms(o, g_ref[...], lam_init))
    o_ref[...] = jnp.concatenate(outs, axis=-1)


def _diff_prompt(q, k, vt, dl, subln, b, t, lam_init):
    n, w = q.shape
    hw = 2 * DH_A
    tq = vt.shape[2]
    nq = t // tq
    return pl.pallas_call(
        functools.partial(_diff_prompt_kernel, tq=tq, lam_init=lam_init),
        grid=(b, nq),
        in_specs=[pl.BlockSpec(dl.shape, lambda bi, i: (0, 0)),
                  pl.BlockSpec((tq, w), lambda bi, i: (bi * nq + i, 0)),
                  pl.BlockSpec((t, w), lambda bi, i: (bi, 0)),
                  pl.BlockSpec((nq, w, tq), lambda bi, i: (bi, 0, 0)),
                  pl.BlockSpec((1, hw), lambda bi, i: (0, 0))],
        out_specs=pl.BlockSpec((tq, w), lambda bi, i: (bi * nq + i, 0)),
        out_shape=jax.ShapeDtypeStruct((n, w), F32),
        compiler_params=_params("parallel", "arbitrary"),
        name="diff_attn_prompt",
    )(dl, q, k, vt, subln.reshape(1, hw))


def _block_diag_queries(q8, groups, group_width):
    rows = groups * SUBLANES
    row_g = lax.broadcasted_iota(jnp.int32, (rows, 1), 0) // SUBLANES
    lane_g = lax.broadcasted_iota(jnp.int32, (1, q8.shape[1]), 1) // group_width
    return jnp.where(lane_g == row_g, jnp.concatenate([q8] * groups, axis=0), 0.0).astype(BF16)


def _diff_sample_kernel(pt_ref, dl_ref, q_ref, kn_ref, vn_ref, g_ref, *refs, n_pages, lam_init):
    del pt_ref
    kt_refs, v_refs, o_ref = refs[:n_pages], refs[n_pages:2 * n_pages], refs[2 * n_pages]
    width = q_ref.shape[2]
    groups = width // DH_A
    hw = 2 * DH_A
    qbd = _block_diag_queries(q_ref[0] * (DH_A ** -0.5), groups, DH_A)
    row_t = lax.broadcasted_iota(jnp.int32, (groups * SUBLANES, 1), 0) % SUBLANES
    col = lax.broadcasted_iota(jnp.int32, (1, SUBLANES), 1)
    s_pages = [_dot(qbd, kt[0].astype(BF16)) for kt in kt_refs]
    s_new = jnp.where(col <= row_t, _dot_nt(qbd, kn_ref[0].astype(BF16)), NEG)
    m = jnp.max(s_new, -1, keepdims=True)
    for s in s_pages:
        m = jnp.maximum(m, jnp.max(s, -1, keepdims=True))
    p_new = jnp.exp(s_new - m)
    den = jnp.sum(p_new, -1, keepdims=True)
    p_pages = []
    for s in s_pages:
        pr = jnp.exp(s - m)
        den = den + jnp.sum(pr, -1, keepdims=True)
        p_pages.append(pr.astype(BF16))
    p_new = p_new.astype(BF16)
    lam = _diff_lambda(dl_ref, lam_init)
    vn = vn_ref[0].astype(BF16)
    n_heads = width // hw
    page = v_refs[0].shape[1]
    spread = jnp.where(lax.broadcasted_iota(jnp.int32, (page, page * n_heads), 1) // n_heads
                       == lax.broadcasted_iota(jnp.int32, (page, page * n_heads), 0), 1.0, 0.0).astype(BF16)
    row_h = lax.broadcasted_iota(jnp.int32, (groups * SUBLANES, 1), 0) // (2 * SUBLANES)
    own = lax.broadcasted_iota(jnp.int32, (1, page * n_heads), 1) % n_heads == row_h
    p_wide = [jnp.where(own, _dot(pr, spread), 0.0).astype(BF16) for pr in p_pages]
    v_rows = [v_ref[0].reshape(page * n_heads, hw).astype(BF16) for v_ref in v_refs]
    acc = jnp.concatenate([_dot(p_new[2 * h * SUBLANES:2 * (h + 1) * SUBLANES], vn[:, h * hw:(h + 1) * hw])
                           for h in range(n_heads)], axis=0)
    for pw_, vr in zip(p_wide, v_rows):
        acc = acc + _dot(pw_, vr)
    a = acc / den
    outs = []
    for h in range(n_heads):
        r0 = 2 * h * SUBLANES
        outs.append(_head_rms(a[r0:r0 + SUBLANES] - lam * a[r0 + SUBLANES:r0 + 2 * SUBLANES], g_ref[...], lam_init))
    o_ref[0] = jnp.concatenate(outs, axis=-1)


def _diff_sample(q8, kn8, vn8, pool_kt, pool_v, page_table, dl, subln, lam_init):
    bs, _, w = q8.shape
    n_pages = page_table.shape[1]
    tok = pl.BlockSpec((1, SUBLANES, w), lambda b, pt: (b, 0, 0))
    kspecs = [pl.BlockSpec((1,) + pool_kt.shape[1:], lambda b, pt, j=j: (pt[b, j], 0, 0)) for j in range(n_pages)]
    vspecs = [pl.BlockSpec((1,) + pool_v.shape[1:], lambda b, pt, j=j: (pt[b, j], 0, 0, 0)) for j in range(n_pages)]
    return pl.pallas_call(
        functools.partial(_diff_sample_kernel, n_pages=n_pages, lam_init=lam_init),
        grid_spec=pltpu.PrefetchScalarGridSpec(
            num_scalar_prefetch=1,
            grid=(bs,),
            in_specs=[pl.BlockSpec(dl.shape, lambda b, pt: (0, 0)), tok, tok, tok,
                      pl.BlockSpec((1, 2 * DH_A), lambda b, pt: (0, 0))] + kspecs + vspecs,
            out_specs=tok),
        out_shape=jax.ShapeDtypeStruct((bs, SUBLANES, w), F32),
        compiler_params=_params("parallel"),
        name="diff_attn_sample",
    )(page_table, dl, q8, kn8, vn8, subln.reshape(1, 2 * DH_A), *([pool_kt] * n_pages), *([pool_v] * n_pages))


def _s5_prep_kernel(are_ref, aim_ref, ldt_ref, bre_ref, bim_ref, lr_ref, li_ref, bbr_ref, bbi_ref):
    a_re, a_im = are_ref[...], aim_ref[...]
    dt = jnp.exp(ldt_ref[...])
    mag = jnp.exp(a_re * dt)
    lam_re, lam_im = mag * jnp.cos(a_im * dt), mag * jnp.sin(a_im * dt)
    den = a_re * a_re + a_im * a_im
    nr = lam_re - 1.0
    f_re = (nr * a_re + lam_im * a_im) / den
    f_im = (lam_im * a_re - nr * a_im) / den
    lr_ref[...] = lam_re
    li_ref[...] = lam_im
    for g in range(a_re.shape[0]):
        fr, fi = f_re[g:g + 1, :], f_im[g:g + 1, :]
        br, bi = bre_ref[g], bim_ref[g]
        bbr_ref[g] = fr * br - fi * bi
        bbi_ref[g] = fr * bi + fi * br


def _s5_prep(a_re, a_im, log_dt, b_re, b_im):
    g, p = a_re.shape
    c = b_re.shape[-1]
    bt = lambda b: jnp.transpose(b, (0, 2, 1))
    sd = jax.ShapeDtypeStruct
    return pl.pallas_call(
        _s5_prep_kernel,
        out_shape=[sd((g, p), F32), sd((g, p), F32), sd((g, c, p), F32), sd((g, c, p), F32)],
        name="s5_prep",
    )(a_re, a_im, log_dt.reshape(g, 1), bt(b_re), bt(b_im))


S5_SPLIT = 2


def _dot_block_diag(x, w_ref):
    kx, kw = x.shape[1] // S5_SPLIT, w_ref.shape[1] // S5_SPLIT
    return jnp.concatenate([_dot(x[:, i * kx:(i + 1) * kx], w_ref[i * kx:(i + 1) * kx, i * kw:(i + 1) * kw])
                            for i in range(S5_SPLIT)], axis=-1)


def _s5_tail(u, hr, hi, ccr, cci, d, gw, gb):
    y = _dot_block_diag(hr.astype(BF16), ccr) - _dot_block_diag(hi.astype(BF16), cci) + d * u
    z = jax.nn.gelu(y)
    return z * _sigmoid(_dot(z.astype(BF16), gw) + gb)


def _s5_prompt_kernel(u_ref, bbr_ref, bbi_ref, lr_ref, li_ref, h0r_ref, h0i_ref, ccr_ref, cci_ref,
                      d_ref, gw_ref, gb_ref, y_ref, hr_out, hi_out, xr, xi, cr, ci):
    i = pl.program_id(1)
    tc = u_ref.shape[0]

    @pl.when(i == 0)
    def _():
        cr[...] = h0r_ref[0]
        ci[...] = h0i_ref[0]

    u = u_ref[...]
    ub = u.astype(BF16)
    xr[...] = _dot_block_diag(ub, bbr_ref)
    xi[...] = _dot_block_diag(ub, bbi_ref)
    lr, li = lr_ref[...], li_ref[...]

    def step(t, carry):
        hr, hi = carry
        nhr = lr * hr - li * hi + xr[pl.ds(t, 1), :]
        nhi = lr * hi + li * hr + xi[pl.ds(t, 1), :]
        xr[pl.ds(t, 1), :] = nhr
        xi[pl.ds(t, 1), :] = nhi
        return nhr, nhi

    hr, hi = lax.fori_loop(0, tc, step, (cr[...], ci[...]), unroll=8)
    cr[...] = hr
    ci[...] = hi
    y_ref[...] = _s5_tail(u, xr[...], xi[...], ccr_ref[...], cci_ref[...], d_ref[...], gw_ref[...], gb_ref[...])

    @pl.when(i == pl.num_programs(1) - 1)
    def _():
        hr_out[0] = hr
        hi_out[0] = hi


def _s5_prompt(u, b, t, h0r, h0i, prm):
    bbr, bbi, lr, li, ccr, cci, d, gw, gb = prm
    n, ch = u.shape
    ns = lr.shape[1]
    tc = min(512, t)
    nt = t // tc
    st = pl.BlockSpec((1, 1, ns), lambda bi, i: (bi, 0, 0))
    sd = jax.ShapeDtypeStruct
    return pl.pallas_call(
        _s5_prompt_kernel,
        grid=(b, nt),
        in_specs=[pl.BlockSpec((tc, ch), lambda bi, i: (bi * nt + i, 0)),
                  _const_spec(bbr.shape), _const_spec(bbi.shape), _const_spec(lr.shape), _const_spec(li.shape),
                  st, st, _const_spec(ccr.shape), _const_spec(cci.shape), _const_spec(d.shape),
                  _const_spec(gw.shape), _const_spec(gb.shape)],
        out_specs=[pl.BlockSpec((tc, ch), lambda bi, i: (bi * nt + i, 0)), st, st],
        out_shape=[sd((n, ch), F32), sd((b, 1, ns), F32), sd((b, 1, ns), F32)],
        scratch_shapes=[pltpu.VMEM((tc, ns), F32), pltpu.VMEM((tc, ns), F32),
                        pltpu.VMEM((1, ns), F32), pltpu.VMEM((1, ns), F32)],
        compiler_params=_params("parallel", "arbitrary"),
        name="s5_prompt",
    )(u, bbr, bbi, lr, li, h0r, h0i, ccr, cci, d, gw, gb)


def _s5_sample_kernel(u_ref, bbr_ref, bbi_ref, lr_ref, li_ref, h0r_ref, h0i_ref, ccr_ref, cci_ref,
                      d_ref, gw_ref, gb_ref, y_ref, hr_out, hi_out):
    lr, li = lr_ref[...], li_ref[...]
    hr, hi = h0r_ref[...], h0i_ref[...]
    for t in range(u_ref.shape[0]):
        u = u_ref[t]
        ub = u.astype(BF16)
        xr = _dot_block_diag(ub, bbr_ref)
        xi = _dot_block_diag(ub, bbi_ref)
        hr, hi = lr * hr - li * hi + xr, lr * hi + li * hr + xi
        y_ref[t] = _s5_tail(u, hr, hi, ccr_ref[...], cci_ref[...], d_ref[...], gw_ref[...], gb_ref[...])
    hr_out[...] = hr
    hi_out[...] = hi


def _s5_sample(u_tm, h0r, h0i, prm):
    bbr, bbi, lr, li, ccr, cci, d, gw, gb = prm
    sd = jax.ShapeDtypeStruct
    return pl.pallas_call(
        _s5_sample_kernel,
        out_shape=[sd(u_tm.shape, F32), sd(h0r.shape, F32), sd(h0i.shape, F32)],
        compiler_params=pltpu.CompilerParams(vmem_limit_bytes=VMEM_LIMIT_BYTES),
        name="s5_sample",
    )(u_tm, bbr, bbi, lr, li, h0r, h0i, ccr, cci, d, gw, gb)


def _dil_prompt_kernel(q_ref, kp_ref, kc_ref, vp_ref, vc_ref, o_ref, l_ref):
    n = pl.program_id(1)
    blk = q_ref.shape[1]
    q = q_ref[0] * (DH_C ** -0.5)
    kcat = jnp.concatenate([kp_ref[0], kc_ref[0]], axis=0).astype(BF16)
    vcat = jnp.concatenate([vp_ref[0], vc_ref[0]], axis=0).astype(BF16)
    qi = lax.broadcasted_iota(jnp.int32, (blk, 1), 0) + blk
    ki = lax.broadcasted_iota(jnp.int32, (1, 2 * blk), 1)
    dist = qi - ki
    lo_k = jnp.where(n > 0, 0, blk)
    ok = (dist >= 0) & (dist <= blk) & (ki >= lo_k)
    w = q.shape[1]
    pw = 2 * DH_C
    first = lax.broadcasted_iota(jnp.int32, (1, pw), 1) < DH_C
    ok2 = jnp.concatenate([ok, ok], axis=0)
    pairs = [slice(p * pw, (p + 1) * pw) for p in range(w // pw)]
    qbd = [jnp.concatenate([jnp.where(first, q[:, sl], 0.0), jnp.where(first, 0.0, q[:, sl])], axis=0).astype(BF16)
           for sl in pairs]
    s = [jnp.where(ok2, _dot_nt(qb, kcat[:, sl]), NEG) for qb, sl in zip(qbd, pairs)]
    m = [jnp.max(x, -1, keepdims=True) for x in s]
    pr = [jnp.exp(x - mm) for x, mm in zip(s, m)]
    den = [jnp.sum(x, -1, keepdims=True) for x in pr]
    oh = [_dot((x / d).astype(BF16), vcat[:, sl]) for x, d, sl in zip(pr, den, pairs)]
    lh = [mm + jnp.log(d) for mm, d in zip(m, den)]
    o_ref[0] = jnp.concatenate([jnp.where(first, x[:blk], x[blk:]) for x in oh], axis=-1)
    l_ref[0] = jnp.concatenate([jnp.where(first, x[:blk], x[blk:]) for x in lh], axis=-1)


def _dil_prompt(qd, kd, vd):
    bd, ns, w = qd.shape
    cur = pl.BlockSpec((1, C_BLK, w), lambda b, n: (b, n, 0))
    prev = pl.BlockSpec((1, C_BLK, w), lambda b, n: (b, jnp.maximum(n - 1, 0), 0))
    sd = jax.ShapeDtypeStruct((bd, ns, w), F32)
    return pl.pallas_call(
        _dil_prompt_kernel,
        grid=(bd, ns // C_BLK),
        in_specs=[cur, prev, cur, prev, cur],
        out_specs=[cur, cur],
        out_shape=[sd, sd],
        compiler_params=_params("parallel", "parallel"),
        name="dilated_attn_prompt",
    )(qd, kd, kd, vd, vd)


def _dil_sample_kernel(q_ref, kn_ref, vn_ref, kt_ref, vt_ref, o_ref):
    w = q_ref.shape[2]
    nh = w // DH_C
    buf = kt_ref.shape[2]
    rows = nh * SUBLANES
    qbd = _block_diag_queries(q_ref[0] * (DH_C ** -0.5), nh, DH_C)
    row_q = lax.broadcasted_iota(jnp.int32, (rows, 1), 0) % SUBLANES

    def reach(delta):
        mult = jnp.zeros(delta.shape, F32)
        for d in C_DILATIONS:
            ok = (delta >= 0) & (lax.rem(delta, d) == 0) & (delta <= d * C_BLK)
            mult = mult + jnp.where(ok, 1.0, 0.0)
        return mult

    mult = reach(buf + row_q - lax.broadcasted_iota(jnp.int32, (1, buf), 1))
    mult_n = reach(row_q - lax.broadcasted_iota(jnp.int32, (1, SUBLANES), 1))
    s = jnp.where(mult > 0, _dot(qbd, kt_ref[0].astype(BF16)), NEG)
    s_n = jnp.where(mult_n > 0, _dot_nt(qbd, kn_ref[0].astype(BF16)), NEG)
    m = jnp.maximum(jnp.max(s, -1, keepdims=True), jnp.max(s_n, -1, keepdims=True))
    p = mult * jnp.exp(s - m)
    p_n = mult_n * jnp.exp(s_n - m)
    inv = 1.0 / (jnp.sum(p, -1, keepdims=True) + jnp.sum(p_n, -1, keepdims=True))
    o = _dot_nt((p * inv).astype(BF16), vt_ref[0].astype(BF16)) + _dot((p_n * inv).astype(BF16), vn_ref[0].astype(BF16))
    lane_h = lax.broadcasted_iota(jnp.int32, (1, w), 1) // DH_C
    out = jnp.zeros((SUBLANES, w), F32)
    for h in range(nh):
        out = out + jnp.where(lane_h == h, o[h * SUBLANES:(h + 1) * SUBLANES, :], 0.0)
    o_ref[0] = out


def _dil_sample(q8, kn8, vn8, cache_kt, cache_vt):
    bs, w, buf = cache_kt.shape
    tok = pl.BlockSpec((1, SUBLANES, w), lambda b: (b, 0, 0))
    cache = pl.BlockSpec((1, w, buf), lambda b: (b, 0, 0))
    return pl.pallas_call(
        _dil_sample_kernel,
        grid=(bs,),
        in_specs=[tok, tok, tok, cache, cache],
        out_specs=tok,
        out_shape=jax.ShapeDtypeStruct((bs, SUBLANES, w), F32),
        compiler_params=_params("parallel"),
        name="dilated_attn_sample",
    )(q8, kn8, vn8, cache_kt, cache_vt)


def _softplus(x):
    return jnp.maximum(x, 0.0) + jnp.log1p(jnp.exp(-jnp.abs(x)))


def _rwkv_pre_kernel(pd_ref, pv_ref, mu_ref, w0_ref, w2_ref, a0_ref, a2_ref, g2_ref, kk_ref, ka_ref, rk_ref, seg_ref,
                     r_o, ld_o, k_o, v_o, kk_o, b_o, g_o, bonus_o, *, seq_tiles):
    pd = pd_ref[...]
    if seq_tiles is None:
        prev = pv_ref[...]
    else:
        first = pl.program_id(0) % seq_tiles == 0
        before = jnp.where(first, 0.0, pv_ref[SUBLANES - 1:SUBLANES, :])
        row = lax.broadcasted_iota(jnp.int32, (pd.shape[0], 1), 0)
        prev = jnp.where(row == 0, before, pltpu.roll(pd, 1, 0))
    xm = pd + (prev - pd) * mu_ref[...]
    o1, o2, o3 = D_W, 2 * D_W, 3 * D_W
    o5 = o3 + D_LORA_W + D_LORA_A
    r, k, v = xm[:, :o1], xm[:, o1:o2], xm[:, o2:o3]
    wa, gl = xm[:, o3:o5], xm[:, o5:]
    lw = _dot(jnp.tanh(wa).astype(BF16), w2_ref[...])
    la = _dot(wa.astype(BF16), a2_ref[...])
    g = _dot(_sigmoid(gl).astype(BF16), g2_ref[...])
    w_log = -_softplus(-(w0_ref[...] + lw)) - 0.5
    a = _sigmoid(a0_ref[...] + la)
    seg = seg_ref[...]
    kk = k * kk_ref[...]
    kk = kk / jnp.maximum(jnp.sqrt(_dot_exact_rhs(kk * kk, seg)), 1e-12)
    k2 = k * (1.0 + (a - 1.0) * ka_ref[...])
    r_o[...] = r
    ld_o[...] = -jnp.exp(w_log)
    k_o[...] = k2
    v_o[...] = v
    kk_o[...] = kk
    b_o[...] = kk * a
    g_o[...] = g
    bonus_o[...] = _dot_exact_rhs(r * k2 * rk_ref[...], seg) * v


def _rwkv_pre(pd, prev, prm, seq_tiles=None):
    n, cols = pd.shape
    tm = min(TOKEN_TILE, n)
    tok = lambda wd: pl.BlockSpec((tm, wd), lambda i: (i, 0))
    if prev is None:
        prev = pd
        prev_spec = pl.BlockSpec((SUBLANES, cols), lambda i: (jnp.maximum(i * (tm // SUBLANES) - 1, 0), 0))
    else:
        prev_spec = tok(cols)
    return pl.pallas_call(
        functools.partial(_rwkv_pre_kernel, seq_tiles=seq_tiles),
        grid=(n // tm,),
        in_specs=[tok(cols), prev_spec] + [_const_spec(p.shape) for p in prm],
        out_specs=[tok(D_W)] * 8,
        out_shape=[jax.ShapeDtypeStruct((n, D_W), F32)] * 8,
        compiler_params=_params("parallel"),
        name="rwkv_pre",
    )(pd, prev, *prm)


def _rwkv_chunk_kernel(r_ref, ld_ref, k_ref, v_ref, kk_ref, b_ref, y_ref, s_out, st):
    c = pl.program_id(1)
    nb, ch, w = r_ref.shape
    pw = 2 * N_D
    wide = lambda ref: jnp.concatenate([ref[i] for i in range(nb)], axis=-1)

    @pl.when(c == 0)
    def _():
        st[...] = jnp.zeros(st.shape, F32)

    ri = lax.broadcasted_iota(jnp.int32, (ch, ch), 0)
    ci = lax.broadcasted_iota(jnp.int32, (ch, ch), 1)
    tri_incl = ci <= ri
    tri_strict = ci < ri
    eye_c = jnp.where(ci == ri, 1.0, 0.0)
    ld = wide(ld_ref)
    r_w, k_w, b_w = wide(r_ref), wide(k_ref), wide(b_ref)
    cum = _dot_exact_rhs_left(jnp.where(tri_incl, 1.0, 0.0).astype(BF16), ld)
    cum_end = cum[ch - 1:ch, :]
    g_inc = jnp.exp(cum)
    g_inv = jnp.exp(-cum)
    g_end = jnp.exp(cum_end - cum)
    rho = r_w * g_inc
    kap = wide(kk_ref) * jnp.exp(cum - ld)
    kh = k_w * g_inv
    bh = b_w * g_inv
    khg = k_w * g_end
    bhg = b_w * g_end
    gam_end = jnp.exp(cum_end)
    v = wide(v_ref)

    pi = lax.broadcasted_iota(jnp.int32, (pw, pw), 0)
    pj = lax.broadcasted_iota(jnp.int32, (pw, pw), 1)
    same_head = (pi // N_D) == (pj // N_D)
    lane = lax.broadcasted_iota(jnp.int32, (1, pw), 1)
    first = lane < N_D

    npair = nb * w // pw
    heads = [(pr, hh) for pr in range(npair) for hh in range(2)]
    sls = [slice(pr * pw, (pr + 1) * pw) for pr in range(npair)]
    sel = (first, jnp.logical_not(first))
    bf = lambda x: x.astype(BF16)
    kap_p = [kap[:, s] for s in sls]
    rho_p = [rho[:, s] for s in sls]
    kap_b = [bf(x) for x in kap_p]
    kh_b = [bf(kh[:, s]) for s in sls]
    bh_b = [bf(bh[:, s]) for s in sls]
    v_b = [bf(v[:, s]) for s in sls]
    kap_m = [bf(jnp.where(sel[hh], kap_p[pr], 0.0)) for pr, hh in heads]
    rho_m = [bf(jnp.where(sel[hh], rho_p[pr], 0.0)) for pr, hh in heads]
    a_b = [jnp.where(tri_strict, _dot_nt(kap_m[i], bh_b[pr]), 0.0) for i, (pr, _) in enumerate(heads)]
    a_k = [bf(jnp.where(tri_strict, _dot_nt(kap_m[i], kh_b[pr]), 0.0)) for i, (pr, _) in enumerate(heads)]
    ap_b = [bf(jnp.where(tri_incl, _dot_nt(rho_m[i], bh_b[pr]), 0.0)) for i, (pr, _) in enumerate(heads)]
    ap_k = [bf(jnp.where(tri_incl, _dot_nt(rho_m[i], kh_b[pr]), 0.0)) for i, (pr, _) in enumerate(heads)]
    tl = [eye_c - a for a in a_b]
    pw2 = [_dot(bf(a), bf(a)) for a in a_b]
    span = 2
    while span < ch:
        tl = [t + _dot(bf(t), bf(p2)) for t, p2 in zip(tl, pw2)]
        span *= 2
        if span < ch:
            pw2 = [_dot(bf(p2), bf(p2)) for p2 in pw2]
    tl_b = [bf(t) for t in tl]
    kap2_h = [_dot(tl_b[i], kap_b[pr]) for i, (pr, _) in enumerate(heads)]
    akv = [bf(_dot(a_k[i], v_b[pr])) for i, (pr, _) in enumerate(heads)]
    wr_h = [_dot(tl_b[i], akv[i]) for i in range(len(heads))]
    rho2_h = [rho_p[pr] - _dot(ap_b[i], bf(kap2_h[i])) for i, (pr, _) in enumerate(heads)]
    yloc_h = [_dot(ap_k[i], v_b[pr]) - _dot(ap_b[i], bf(wr_h[i])) for i, (pr, _) in enumerate(heads)]
    pair = lambda xs: [jnp.where(first, xs[2 * pr], xs[2 * pr + 1]) for pr in range(npair)]
    kap2, wr, rho2, yloc = pair(kap2_h), pair(wr_h), pair(rho2_h), pair(yloc_h)
    bhg_b = [bf(bhg[:, s]) for s in sls]
    khg_b = [bf(khg[:, s]) for s in sls]
    diag = [jnp.where(pi == pj, jnp.broadcast_to(gam_end[:, s], (pw, pw)), 0.0) for s in sls]
    phi = [jnp.where(same_head, diag[pr] - _dot_tn(bhg_b[pr], bf(kap2[pr])), 0.0) for pr in range(npair)]
    gmat = [jnp.where(same_head, _dot_tn(khg_b[pr], v_b[pr]) - _dot_tn(bhg_b[pr], bf(wr[pr])), 0.0)
            for pr in range(npair)]
    s_b = [bf(st[pr]) for pr in range(npair)]
    per_b = w // pw
    for pr in range(npair):
        y_ref[pr // per_b, :, (pr % per_b) * pw:(pr % per_b + 1) * pw] = _dot(bf(rho2[pr]), s_b[pr]) + yloc[pr]
    for pr in range(npair):
        st[pr] = _dot(bf(phi[pr]), s_b[pr]) + gmat[pr]

    @pl.when(c == pl.num_programs(1) - 1)
    def _():
        for pr in range(npair):
            s_out[pr // per_b, pr % per_b] = st[pr]


def _dot_exact_rhs_left(m01, x):
    hi, mid, lo = _split3(x)
    return _dot(m01, hi) + _dot(m01, mid) + _dot(m01, lo)


def _rwkv_chunk(r, ld, k2, v, kk, bb, b, t):
    n, w = r.shape
    ch = min(RWKV_CHUNK, t)
    nc = t // ch
    npair = w // (2 * N_D)
    nb = next(d for d in (4, 2, 1) if b % d == 0)
    tok = pl.BlockSpec((nb, ch, w), lambda bi, c: (bi, c, 0))
    seq = lambda z: z.reshape(b, t, w)
    y, s_pairs = pl.pallas_call(
        _rwkv_chunk_kernel,
        grid=(b // nb, nc),
        in_specs=[tok] * 6,
        out_specs=[tok, pl.BlockSpec((nb, npair, 2 * N_D, 2 * N_D), lambda bi, c: (bi, 0, 0, 0))],
        out_shape=[jax.ShapeDtypeStruct((b, t, w), F32), jax.ShapeDtypeStruct((b, npair, 2 * N_D, 2 * N_D), F32)],
        scratch_shapes=[pltpu.VMEM((nb * npair, 2 * N_D, 2 * N_D), F32)],
        compiler_params=_params("parallel", "arbitrary"),
        name="rwkv_chunk_scan",
    )(seq(r), seq(ld), seq(k2), seq(v), seq(kk), seq(bb))
    return y.reshape(n, w), s_pairs


def _rwkv_lane_kernel(r_ref, ld_ref, k_ref, v_ref, kk_ref, b_ref, s_ref, y_ref, s_out):
    steps = r_ref.shape[0]

    def body(vi, carry):
        s = s_ref[0, vi]
        for t in range(steps):
            sk = jnp.sum(s * kk_ref[t, 0], axis=0, keepdims=True)
            vv = v_ref[t, 0, pl.ds(vi, 1), :]
            s = s * jnp.exp(ld_ref[t, 0]) - sk * b_ref[t, 0] + vv * k_ref[t, 0]
            y_ref[t, 0, pl.ds(vi, 1), :] = jnp.sum(s * r_ref[t, 0], axis=0, keepdims=True)
        s_out[0, vi] = s
        return carry

    lax.fori_loop(0, s_ref.shape[1], body, 0)


def _rwkv_lane(rt, ldt, kt, vt, kkt, bt, s0):
    steps, nh, nd, bs = rt.shape
    tok = pl.BlockSpec((steps, 1, nd, bs), lambda h: (0, h, 0, 0))
    stt = pl.BlockSpec((1, nd, nd, bs), lambda h: (h, 0, 0, 0))
    return pl.pallas_call(
        _rwkv_lane_kernel,
        grid=(nh,),
        in_specs=[tok] * 6 + [stt],
        out_specs=[tok, stt],
        out_shape=[jax.ShapeDtypeStruct(rt.shape, F32), jax.ShapeDtypeStruct(s0.shape, F32)],
        compiler_params=_params("parallel"),
        name="rwkv_lane_scan",
    )(rt, ldt, kt, vt, kkt, bt, s0)


def _rwkv_post_kernel(y_ref, bonus_ref, g_ref, gw_ref, gb_ref, seg_ref, o_ref):
    y = y_ref[...]
    seg = seg_ref[...]
    mu = _dot_exact_rhs(y, seg) * (1.0 / N_D)
    yc = y - mu
    var = _dot_exact_rhs(yc * yc, seg) * (1.0 / N_D)
    yn = yc * lax.rsqrt(var + GN_EPS) * gw_ref[...] + gb_ref[...]
    o_ref[...] = (yn + bonus_ref[...]) * g_ref[...]


def _rwkv_post(y, bonus, g, gn_w, gn_b, seg):
    n, w = y.shape
    tm = min(TOKEN_TILE, n)
    tok = pl.BlockSpec((tm, w), lambda i: (i, 0))
    return pl.pallas_call(
        _rwkv_post_kernel,
        grid=(n // tm,),
        in_specs=[tok, tok, tok, _const_spec((1, w)), _const_spec((1, w)), _const_spec(seg.shape)],
        out_specs=tok,
        out_shape=jax.ShapeDtypeStruct((n, w), F32),
        compiler_params=_params("parallel"),
        name="rwkv_post",
    )(y, bonus, g, gn_w.reshape(1, w), gn_b.reshape(1, w), seg)


def _pad_tokens(x, bs, s_len):
    x = x.reshape(bs, s_len, x.shape[-1])
    return jnp.pad(x, ((0, 0), (0, SUBLANES - s_len), (0, 0)))


def _block_diag_in(bb):
    g, c, p = bb.shape
    return jnp.einsum('gcp,gh->gchp', bb, jnp.eye(g, dtype=bb.dtype)).reshape(g * c, g * p)


def _block_diag_out(cc):
    g, c, p = cc.shape
    return jnp.einsum('gcp,gh->gphc', cc, jnp.eye(g, dtype=cc.dtype)).reshape(g * p, g * c)


def _shifted(pd, shift0, b, t):
    pd3 = pd.reshape(b, t, pd.shape[-1])
    return jnp.concatenate([shift0[:, None, :], pd3[:, :-1]], axis=1).reshape(b * t, pd.shape[-1])


def _trunk(x, mod_all, row0, rows, per_token_mod, sample, st, p):
    b, t, d = x.shape
    n = b * t
    xt = x.reshape(n, d)
    tm = min(TOKEN_TILE, n)
    outs = {}
    for l in range(DEPTH):
        mod = mod_all[l, row0:row0 + rows]
        if per_token_mod:
            mod_arr = jnp.repeat(mod, t, axis=0).reshape(n // tm, tm, 9 * d)
        else:
            mod_arr = mod.reshape(rows * 9, 1, d)

        def mods(i, mod_arr=mod_arr):
            def one(c):
                if per_token_mod:
                    return _Mod(mod_arr, pl.BlockSpec((1, tm, d), lambda ti: (ti, 0, c)))
                return _Mod(mod_arr, pl.BlockSpec((1, 1, d), lambda ti: ((ti // (t // tm)) * 9 + c, 0, 0)))
            return [one(3 * i + j) for j in range(3)], t // tm

        m0, tpg = mods(0)
        xt = _ffn(xt, m0, tpg, p['wg'][l][0], p['wu'][l][0], p['wd'][l][0], p['ln_g'][l, 0], p['ln_b'][l, 0])
        m1, _ = mods(1)
        if l % 2 == 0:
            e = l // 2
            lam_init = 0.8 - 0.6 * math.exp(-0.3 * l)
            widths = (512, 512, 512, S5_CH)
            dl, subln = p['diff_lambda'][e], p['diff_subln'][e]
            if sample:
                q, k, v, u = _inproj(xt, m1, tpg, p['even_w_in'][e], widths)
            else:
                q, k, v, u, vt, kt = _inproj(xt, m1, tpg, p['even_w_in'][e], widths,
                                             t_weights=((p['even_wv_t'][e], True), (p['even_wk_t'][e], False)),
                                             t_block=min(256, t), seq_tiles=t // tm)
            if sample:
                att8 = _diff_sample(_pad_tokens(q, b, t), _pad_tokens(k, b, t), _pad_tokens(v, b, t),
                                    st['pool_k'][e], st['pool_v'][e], st['page_table'], dl, subln, lam_init)
                att = att8[:, :t].reshape(n, -1)
                u_tm = u.reshape(b, t, -1).transpose(1, 0, 2)
                y_tm, hr, hi = _s5_sample(u_tm, st['s5_re'][e].reshape(b, S5_N), st['s5_im'][e].reshape(b, S5_N),
                                          p['s5'][e])
                y5 = y_tm.transpose(1, 0, 2).reshape(n, -1)
            else:
                att = _diff_prompt(q, k, vt, dl, subln, b, t, lam_init)
                zero = jnp.zeros((b, 1, S5_N), F32)
                y5, hr, hi = _s5_prompt(u, b, t, zero, zero, p['s5'][e])
            if sample:
                k_out = k.reshape(b, t, H_A, 2, DH_A)
            else:
                k_out = kt.reshape(b, H_A, 2, DH_A, t).transpose(0, 4, 1, 2, 3)
            outs.setdefault('ak', []).append(k_out)
            outs.setdefault('av', []).append(v.reshape(b, t, H_A, 2 * DH_A))
            outs.setdefault('s5r', []).append(hr.reshape(b, S5_G, S5_P))
            outs.setdefault('s5i', []).append(hi.reshape(b, S5_G, S5_P))
            xt = _outproj(xt, m1[2], tpg, [att], y5, p['even_w_out'][e], p['ln_g'][l, 1], p['ln_b'][l, 1])
        else:
            o = l // 2
            widths = (512, 512, 512, pd_cols(p))
            rp = p['rwkv'][o]
            dil_of = None
            if sample:
                q, k, v, pd = _inproj(xt, m1, tpg, p['odd_w_in'][o], widths)
            else:
                dils = C_DILATIONS[1:]
                res = _inproj(xt, m1, tpg, p['odd_w_in'][o], widths,
                              t_weights=((p['odd_wk_t'][o], False), (p['odd_wv_t'][o], False)),
                              seq_tiles=t // tm, n_dil=3, dils=dils, dil_dtype=BF16)
                q, k, v, pd = res[:4]
                split = {dd: res[4 + 3 * i:7 + 3 * i] for i, dd in enumerate(dils)}
                kt, vt = res[4 + 3 * len(dils):]
            if sample:
                att8 = _dil_sample(_pad_tokens(q, b, t), _pad_tokens(k, b, t), _pad_tokens(v, b, t),
                                   st['cache_c_k'][o], st['cache_c_v'][o])
                atts = [att8[:, :t].reshape(n, -1)]
                prev = _shifted(pd, st['d_shift'][o], b, t)
                r, ld, k2, vv, kk, bb, g, bonus = _rwkv_pre(pd, prev, rp['pre'])
                tl = lambda z: z.reshape(b, t, H_D, N_D).transpose(1, 2, 3, 0)
                s0 = st['d_wkv'][o].transpose(1, 2, 3, 0)
                y_l, s_new = _rwkv_lane(tl(r), tl(ld), tl(k2), tl(vv), tl(kk), tl(bb), s0)
                y = y_l.transpose(3, 0, 1, 2).reshape(n, D_W)
                s_new = s_new.transpose(3, 0, 1, 2)
                k_keep, v_keep = k.reshape(b, t, H_C, DH_C), v.reshape(b, t, H_C, DH_C)
            else:
                atts_o, atts_l = [], []
                for dil in C_DILATIONS:
                    if dil == 1:
                        ob, lb = _dil_prompt(q.reshape(b, t, -1), k.reshape(b, t, -1), v.reshape(b, t, -1))
                        ob, lb = ob.reshape(n, -1), lb.reshape(n, -1)
                    else:
                        ob, lb = _dil_prompt(*[z.reshape(b * dil, t // dil, -1) for z in split[dil]])
                        ob, lb = ob.reshape(b, dil, t // dil, -1), lb.reshape(b, dil, t // dil, -1)
                    atts_o.append(ob)
                    atts_l.append(lb)
                atts = atts_o + atts_l
                dil_of = C_DILATIONS + C_DILATIONS
                r, ld, k2, vv, kk, bb, g, bonus = _rwkv_pre(pd, None, rp['pre'], seq_tiles=t // tm)
                y, s_pairs = _rwkv_chunk(r, ld, k2, vv, kk, bb, b, t)
                sp = s_pairs.reshape(b, H_D // 2, 2, N_D, 2, N_D)
                s_new = jnp.stack([sp[:, :, 0, :, 0, :], sp[:, :, 1, :, 1, :]], axis=2)
                s_new = s_new.reshape(b, H_D, N_D, N_D).transpose(0, 1, 3, 2)
                keep = min(C_BLK * C_DILATIONS[-1], t)
                keep_t = lambda zt: zt[:, :, t - keep:].reshape(b, H_C, DH_C, keep).transpose(0, 3, 1, 2)
                k_keep, v_keep = keep_t(kt), keep_t(vt)
            yd = _rwkv_post(y, bonus, g, rp['gn_w'], rp['gn_b'], rp['seg'])
            outs.setdefault('ck', []).append(k_keep)
            outs.setdefault('cv', []).append(v_keep)
            outs.setdefault('dw', []).append(s_new)
            outs.setdefault('ds', []).append(pd.reshape(b, t, -1)[:, -1])
            xt = _outproj(xt, m1[2], tpg, atts, yd, p['odd_w_out'][o], p['ln_g'][l, 1], p['ln_b'][l, 1],
                          dil_of=dil_of, seq_tiles=t // tm)
        m2, _ = mods(2)
        xt = _ffn(xt, m2, tpg, p['wg'][l][1], p['wu'][l][1], p['wd'][l][1], p['ln_g'][l, 2], p['ln_b'][l, 2])
    stacked = [jnp.stack(outs[key], 0) for key in ('ak', 'av', 's5r', 's5i', 'ck', 'cv', 'dw', 'ds')]
    return xt.reshape(b, t, d), stacked


def pd_cols(p):
    return p['odd_w_in'].shape[-1] - 3 * H_C * DH_C


def kernel(x_prompt, x_sample, cache_a_k, cache_a_v, state_s5_re, state_s5_im, cache_c_k, cache_c_v, state_d_wkv, state_d_shift, page_table, c_prompt, c_sample, ada_w, ada_b, ln_g, ln_b, ffn_w_gate, ffn_w_up, ffn_w_down, even_w_in, even_w_out, diff_lambda, diff_subln, s5_a_re, s5_a_im, s5_log_dt, s5_b_re, s5_b_im, s5_c_re, s5_c_im, s5_d, s5_glu_w, s5_glu_b, odd_w_in, odd_w_out, rwkv_mu, rwkv_w0, rwkv_w2, rwkv_a0, rwkv_a2, rwkv_g2, rwkv_k_k, rwkv_k_a, rwkv_r_k, rwkv_gn_w, rwkv_gn_b):
    bp, bs = x_prompt.shape[0], x_sample.shape[0]
    n_even, n_odd = even_w_in.shape[0], odd_w_in.shape[0]
    bf = lambda w: w.astype(BF16)

    seg = jnp.kron(jnp.eye(H_D, dtype=F32), jnp.ones((N_D, N_D), F32)).astype(BF16)
    s5 = []
    for e in range(n_even):
        lr, li, bbr, bbi = _s5_prep(s5_a_re[e], s5_a_im[e], s5_log_dt[e], s5_b_re[e], s5_b_im[e])
        s5.append((bf(_block_diag_in(bbr)), bf(_block_diag_in(bbi)), lr.reshape(1, S5_N), li.reshape(1, S5_N),
                   bf(_block_diag_out(s5_c_re[e])), bf(_block_diag_out(s5_c_im[e])),
                   s5_d[e].reshape(1, S5_CH), bf(s5_glu_w[e]), s5_glu_b[e].reshape(1, S5_CH)))
    rwkv = []
    for o in range(n_odd):
        row = lambda z: z.reshape(1, -1)
        w2p = jnp.concatenate([rwkv_w2[o], jnp.zeros_like(rwkv_a2[o])], axis=0)
        a2p = jnp.concatenate([jnp.zeros_like(rwkv_w2[o]), rwkv_a2[o]], axis=0)
        pre = (row(rwkv_mu[o]), row(rwkv_w0[o]), bf(w2p), row(rwkv_a0[o]), bf(a2p), bf(rwkv_g2[o]),
               row(rwkv_k_k[o]), row(rwkv_k_a[o]), row(rwkv_r_k[o]), seg)
        rwkv.append(dict(pre=pre, gn_w=rwkv_gn_w[o], gn_b=rwkv_gn_b[o], seg=seg))
    p = dict(wg=bf(ffn_w_gate), wu=bf(ffn_w_up), wd=bf(ffn_w_down), ln_g=ln_g, ln_b=ln_b,
             even_w_in=bf(even_w_in), even_w_out=bf(even_w_out), odd_w_in=bf(odd_w_in), odd_w_out=bf(odd_w_out),
             even_wv_t=bf(jnp.swapaxes(even_w_in[:, :, 2 * A_W:3 * A_W], 1, 2)),
             even_wk_t=bf(jnp.swapaxes(even_w_in[:, :, A_W:2 * A_W], 1, 2)),
             odd_wk_t=bf(jnp.swapaxes(odd_w_in[:, :, C_W:2 * C_W], 1, 2)),
             odd_wv_t=bf(jnp.swapaxes(odd_w_in[:, :, 2 * C_W:3 * C_W], 1, 2)),
             diff_lambda=diff_lambda, diff_subln=diff_subln, s5=s5, rwkv=rwkv)

    mod_all = _ada(jnp.concatenate([c_prompt, c_sample], axis=0), ada_w, ada_b)

    y_prompt, st_p = _trunk(x_prompt, mod_all, 0, bp, False, False, None, p)
    n_pool, page = cache_a_k.shape[1], cache_a_k.shape[2]
    win_buf = cache_c_k.shape[2]
    pos_minor = lambda c: jnp.transpose(c, (0, 1, 3, 4, 2)).reshape(n_odd, bs, -1, win_buf)
    st = dict(pool_k=jnp.transpose(cache_a_k, (0, 1, 3, 4, 5, 2)).reshape(n_even, n_pool, -1, page),
              pool_v=cache_a_v,
              page_table=page_table, s5_re=state_s5_re, s5_im=state_s5_im,
              cache_c_k=pos_minor(cache_c_k), cache_c_v=pos_minor(cache_c_v),
              d_wkv=state_d_wkv, d_shift=state_d_shift)
    y_sample, st_s = _trunk(x_sample, mod_all, bp, bs, True, True, st, p)
    a_k_p, a_v_p, s5_re_p, s5_im_p, c_k_p, c_v_p, d_wkv_p, d_shift_p = st_p
    a_k_s, a_v_s, s5_re_s, s5_im_s, c_k_s, c_v_s, d_wkv_s, d_shift_s = st_s
    return (y_prompt, y_sample, a_k_p, a_k_s, a_v_p, a_v_s, s5_re_p, s5_re_s, s5_im_p, s5_im_s,
            c_k_p, c_k_s, c_v_p, c_v_s, d_wkv_p, d_wkv_s, d_shift_p, d_shift_s)
```

```python
import functools
import math

import jax
import jax.numpy as jnp
from jax import lax
from jax.experimental import pallas as pl
from jax.experimental.pallas import tpu as pltpu

F32 = jnp.float32
BF16 = jnp.bfloat16

DEPTH = 2
H_A, DH_A = 4, 64
A_W = H_A * 2 * DH_A
S5_GROUP, S5_G, S5_P = 16, 32, 64
S5_CH = S5_GROUP * S5_G
S5_N = S5_G * S5_P
H_C, DH_C = 8, 64
C_W = H_C * DH_C
C_BLK = 128
C_DILATIONS = (1, 4, 16)
H_D, N_D = 8, 64
D_W = H_D * N_D
D_LORA_W, D_LORA_A, D_LORA_G = 64, 64, 128
GN_EPS = 64e-5
ALPHA = (2.0 * DEPTH) ** 0.25
LN_EPS = 1e-5
NEG = -1e30

LANES = 128
SUBLANES = 8
VMEM_LIMIT_BYTES = 56 * 1024 * 1024
TOKEN_TILE = 512
RWKV_CHUNK = 64
ATTN_Q_TILE = 256
ATTN_K_BLOCK = 512


def _params(*sem):
    return pltpu.CompilerParams(dimension_semantics=sem, vmem_limit_bytes=VMEM_LIMIT_BYTES)


def _const_spec(shape):
    nd = len(shape)
    return pl.BlockSpec(shape, lambda *_: (0,) * nd, pipeline_mode=pl.Buffered(1))


def _dot(a, b):
    return jnp.dot(a, b, preferred_element_type=F32)


def _dot_nt(a, b):
    return lax.dot_general(a, b, (((1,), (1,)), ((), ())), preferred_element_type=F32)


def _dot_tn(a, b):
    return lax.dot_general(a, b, (((0,), (0,)), ((), ())), preferred_element_type=F32)


def _split3(x):
    hi = x.astype(BF16)
    r1 = x - hi.astype(F32)
    mid = r1.astype(BF16)
    lo = (r1 - mid.astype(F32)).astype(BF16)
    return hi, mid, lo


def _dot_exact_rhs(x, m01):
    hi, mid, lo = _split3(x)
    return _dot(hi, m01) + _dot(mid, m01) + _dot(lo, m01)


def _layer_norm(y, g, b):
    mu = jnp.mean(y, -1, keepdims=True)
    yc = y - mu
    var = jnp.mean(yc * yc, -1, keepdims=True)
    return yc * lax.rsqrt(var + LN_EPS) * g + b


def _sigmoid(x):
    return jax.nn.sigmoid(x)


class _Mod:
    def __init__(self, arr, spec):
        self.arr, self.spec = arr, spec


def _mod_spec(mod, tiles_per_group):
    del tiles_per_group
    return mod.spec


def _ada_kernel(c_ref, w_ref, b_ref, o_ref):
    c = c_ref[...]
    s = (c * _sigmoid(c)).astype(BF16)
    o_ref[0] = _dot(s, w_ref[0].astype(BF16)) + b_ref[0]


def _ada(c_all, ada_w, ada_b):
    nl, d, w = ada_w.shape
    r = c_all.shape[0]
    tn = 1152 if w % 1152 == 0 else w
    return pl.pallas_call(
        _ada_kernel,
        grid=(nl, w // tn),
        in_specs=[pl.BlockSpec((r, d), lambda l, j: (0, 0)),
                  pl.BlockSpec((1, d, tn), lambda l, j: (l, 0, j)),
                  pl.BlockSpec((1, 1, tn), lambda l, j: (l, 0, j))],
        out_specs=pl.BlockSpec((1, r, tn), lambda l, j: (l, 0, j)),
        out_shape=jax.ShapeDtypeStruct((nl, r, w), F32),
        compiler_params=_params("parallel", "parallel"),
        name="ada_mod",
    )(c_all, ada_w, ada_b.reshape(nl, 1, w))


def _ffn_kernel(x_ref, sh_ref, sc_ref, gt_ref, wg_ref, wu_ref, wd_ref, g_ref, b_ref, o_ref):
    x = x_ref[...]
    h = (x * (1.0 + sc_ref[0]) + sh_ref[0]).astype(BF16)
    g = _dot(h, wg_ref[...])
    u = _dot(h, wu_ref[...])
    a = (g * _sigmoid(g) * u).astype(BF16)
    f = _dot(a, wd_ref[...])
    y = ALPHA * x + 0.5 * (1.0 + gt_ref[0]) * f
    o_ref[...] = _layer_norm(y, g_ref[...], b_ref[...])


def _ffn(x, mods, tpg, wg, wu, wd, ln_g, ln_b):
    n, d = x.shape
    f = wg.shape[1]
    tm = min(TOKEN_TILE, n)
    sh, sc, gt = mods
    return pl.pallas_call(
        _ffn_kernel,
        grid=(n // tm,),
        in_specs=[pl.BlockSpec((tm, d), lambda i: (i, 0)),
                  _mod_spec(sh, tpg), _mod_spec(sc, tpg), _mod_spec(gt, tpg),
                  _const_spec((d, f)), _const_spec((d, f)), _const_spec((f, d)),
                  _const_spec((1, d)), _const_spec((1, d))],
        out_specs=pl.BlockSpec((tm, d), lambda i: (i, 0)),
        out_shape=jax.ShapeDtypeStruct((n, d), F32),
        compiler_params=_params("parallel"),
        name="ffn_ln",
    )(x, sh.arr, sc.arr, gt.arr, wg, wu, wd, ln_g.reshape(1, d), ln_b.reshape(1, d))


def _inproj_kernel(x_ref, sh_ref, sc_ref, w_ref, *refs, widths, t_blocked, dils, n_dil):
    n_t = len(t_blocked)
    n_split = n_dil * len(dils)
    wt_refs, outs = refs[:n_t], refs[n_t:]
    o_refs = outs[:len(widths)]
    dil_refs = outs[len(widths):len(widths) + n_split]
    t_refs = outs[len(widths) + n_split:len(widths) + n_split + n_t]
    tm = x_ref.shape[0]
    h = (x_ref[...] * (1.0 + sc_ref[0]) + sh_ref[0]).astype(BF16)
    p = _dot(h, w_ref[...])
    off = 0
    for o_ref, wd in zip(o_refs, widths):
        o_ref[...] = p[:, off:off + wd].astype(o_ref.dtype)
        off += wd
    if n_split:
        stage = refs[-1]
        off = 0
        for s in range(n_dil):
            for c in range(widths[s] // LANES):
                stage[s, c] = p[:, off + c * LANES:off + (c + 1) * LANES]
            off += widths[s]
        for di, d in enumerate(dils):
            for s in range(n_dil):
                for r in range(d):
                    for c in range(widths[s] // LANES):
                        dst = dil_refs[di * n_dil + s]
                        dst[0, r, :, c * LANES:(c + 1) * LANES] = (
                            stage[s, c, pl.ds(r, tm // d, stride=d), :].astype(dst.dtype))
    for wt_ref, t_ref, blocked in zip(wt_refs, t_refs, t_blocked):
        vt = _dot_nt(wt_ref[...], h)
        if blocked:
            blk = t_ref.shape[2]
            for c in range(t_ref.shape[0]):
                t_ref[c] = vt[:, c * blk:(c + 1) * blk]
        else:
            t_ref[0] = vt


def _inproj(x, mods, tpg, w, widths, *, t_weights=(), t_block=None, seq_tiles=None, n_dil=0, dils=(),
            dil_dtype=F32):
    n, d = x.shape
    tm = min(TOKEN_TILE, n)
    sh, sc, _ = mods
    in_specs = [pl.BlockSpec((tm, d), lambda i: (i, 0)), _mod_spec(sh, tpg), _mod_spec(sc, tpg), _const_spec(w.shape)]
    out_specs = [pl.BlockSpec((tm, wd), lambda i: (i, 0)) for wd in widths]
    out_shape = [jax.ShapeDtypeStruct((n, wd), dil_dtype if s < n_dil else F32) for s, wd in enumerate(widths)]
    args = [x, sh.arr, sc.arr, w]
    nb = n // (seq_tiles * tm) if seq_tiles else None
    for dd in dils:
        for s in range(n_dil):
            out_specs.append(pl.BlockSpec((1, dd, tm // dd, widths[s]),
                                          lambda i: (i // seq_tiles, 0, i % seq_tiles, 0)))
            out_shape.append(jax.ShapeDtypeStruct((nb, dd, seq_tiles * tm // dd, widths[s]), dil_dtype))
    for wt, blocked in t_weights:
        in_specs.append(_const_spec(wt.shape))
        args.append(wt)
        if blocked:
            out_specs.append(pl.BlockSpec((tm // t_block, wt.shape[0], t_block), lambda i: (i, 0, 0)))
            out_shape.append(jax.ShapeDtypeStruct((n // t_block, wt.shape[0], t_block), F32))
        else:
            out_specs.append(pl.BlockSpec((1, wt.shape[0], tm), lambda i: (i // seq_tiles, 0, i % seq_tiles)))
            out_shape.append(jax.ShapeDtypeStruct((nb, wt.shape[0], seq_tiles * tm), F32))
    return pl.pallas_call(
        functools.partial(_inproj_kernel, widths=widths, t_blocked=tuple(bl for _, bl in t_weights),
                          dils=tuple(dils), n_dil=n_dil),
        grid=(n // tm,),
        in_specs=in_specs,
        out_specs=out_specs,
        out_shape=out_shape,
        scratch_shapes=[pltpu.VMEM((n_dil, widths[0] // LANES, tm, LANES), F32)] if n_dil and dils else [],
        compiler_params=_params("parallel"),
        name="in_proj",
    )(*args)


def _outproj_kernel(x_ref, gt_ref, *refs, n_branch, dil_of):
    n_att = len(dil_of)
    att_refs = refs[:n_att]
    y_ref, w_ref, g_ref, b_ref, o_ref = refs[n_att:n_att + 5]
    scratch = list(refs[n_att + 5:])
    tm = x_ref.shape[0]
    vals = []
    for ref, d in zip(att_refs, dil_of):
        if d == 1:
            vals.append(ref[...])
        else:
            buf = scratch.pop(0)
            for r in range(d):
                for c in range(buf.shape[0]):
                    buf[c, pl.ds(r, tm // d, stride=d), :] = ref[0, r, :, c * LANES:(c + 1) * LANES]
            vals.append(jnp.concatenate([buf[c] for c in range(buf.shape[0])], axis=-1))
    if n_branch > 1:
        outs = vals[:n_branch]
        lses = vals[n_branch:]
        m = functools.reduce(jnp.maximum, lses)
        ws = [jnp.exp(l - m) for l in lses]
        den = functools.reduce(lambda a, b: a + b, ws)
        att = functools.reduce(lambda a, b: a + b, [w * o for w, o in zip(ws, outs)]) / den
    else:
        att = vals[0]
    half = att.shape[1]
    mix = _dot(att.astype(BF16), w_ref[:half, :]) + _dot(y_ref[...].astype(BF16), w_ref[half:, :])
    y = ALPHA * x_ref[...] + (1.0 + gt_ref[0]) * mix
    o_ref[...] = _layer_norm(y, g_ref[...], b_ref[...])


def _outproj(x, gate, tpg, atts, y, w, ln_g, ln_b, dil_of=None, seq_tiles=None):
    n, d = x.shape
    tm = min(TOKEN_TILE, n)
    half = y.shape[1]
    dil_of = tuple(dil_of) if dil_of else (1,) * len(atts)
    n_branch = len(atts) // 2 if len(atts) > 1 else 1
    tok = lambda wd: pl.BlockSpec((tm, wd), lambda i: (i, 0))
    split = lambda dd: pl.BlockSpec((1, dd, tm // dd, half), lambda i: (i // seq_tiles, 0, i % seq_tiles, 0))
    return pl.pallas_call(
        functools.partial(_outproj_kernel, n_branch=n_branch, dil_of=dil_of),
        grid=(n // tm,),
        in_specs=[tok(d), _mod_spec(gate, tpg)] + [tok(half) if dd == 1 else split(dd) for dd in dil_of]
        + [tok(half), _const_spec(w.shape), _const_spec((1, d)), _const_spec((1, d))],
        out_specs=tok(d),
        out_shape=jax.ShapeDtypeStruct((n, d), F32),
        scratch_shapes=[pltpu.VMEM((half // LANES, tm, LANES), F32) for dd in dil_of if dd > 1],
        compiler_params=_params("parallel"),
        name="out_proj_ln",
    )(x, gate.arr, *atts, y, w, ln_g.reshape(1, d), ln_b.reshape(1, d))


def _diff_lambda(dl_ref, lam_init):
    lp = dl_ref[...]
    a = jnp.sum(lp[0:1] * lp[1:2], axis=-1, keepdims=True)
    b = jnp.sum(lp[2:3] * lp[3:4], axis=-1, keepdims=True)
    return jnp.exp(a) - jnp.exp(b) + lam_init


def _head_rms(o, g, lam_init):
    return o * lax.rsqrt(jnp.mean(o * o, -1, keepdims=True) + LN_EPS) * g * (1.0 - lam_init)


def _diff_prompt_kernel(dl_ref, q_ref, k_ref, vt_ref, g_ref, o_ref, *, tq, tk, lam_init):
    i = pl.program_id(1)
    lam = _diff_lambda(dl_ref, lam_init)
    hw = 2 * DH_A
    n_heads = q_ref.shape[1] // hw
    nm = 2 * n_heads
    lo = lax.broadcasted_iota(jnp.int32, (1, hw), 1) < DH_A
    qs = []
    for h in range(n_heads):
        q = q_ref[:, h * hw:(h + 1) * hw] * (DH_A ** -0.5 * math.log2(math.e))
        qs += [jnp.where(lo, q, 0.0).astype(BF16), jnp.where(lo, 0.0, q).astype(BF16)]
    head = lambda x: slice((x // 2) * hw, (x // 2 + 1) * hw)

    def block(j, carry, masked):
        ms, ls, accs = carry[:nm], carry[nm:2 * nm], carry[2 * nm:]
        kb = k_ref[pl.ds(pl.multiple_of(j * tk, tk), tk), :].astype(BF16)
        vt = vt_ref[j].astype(BF16)
        ss = [_dot_nt(kb[:, head(x)], qs[x]) for x in range(nm)]
        if masked:
            kpos = j * tk + lax.broadcasted_iota(jnp.int32, (tk, 1), 0)
            qpos = i * tq + lax.broadcasted_iota(jnp.int32, (1, tq), 1)
            ok = kpos <= qpos
            ss = [jnp.where(ok, s, NEG) for s in ss]
        ns = [jnp.maximum(m, jnp.max(s, 0, keepdims=True)) for m, s in zip(ms, ss)]
        ps = [jnp.exp2(s - n) for s, n in zip(ss, ns)]
        cs = [jnp.exp2(m - n) for m, n in zip(ms, ns)]
        ls = [c * l + jnp.sum(p, 0, keepdims=True) for c, l, p in zip(cs, ls, ps)]
        accs = [cs[x] * accs[x] + _dot(vt[head(x), :], ps[x].astype(BF16)) for x in range(nm)]
        return tuple(ns) + tuple(ls) + tuple(accs)

    init = ((jnp.full((1, tq), NEG, F32),) * nm + (jnp.zeros((1, tq), F32),) * nm
            + (jnp.zeros((hw, tq), F32),) * nm)
    n_full = (i * tq) // tk
    c = lax.fori_loop(0, n_full, lambda j, c: block(j, c, False), init)
    fin = block(n_full, c, True)
    ls, accs = fin[nm:2 * nm], fin[2 * nm:]
    outs = []
    for h in range(n_heads):
        o = (accs[2 * h] / ls[2 * h] - lam * (accs[2 * h + 1] / ls[2 * h + 1])).T
        outs.append(_head_rms(o, g_ref[...], lam_init))
    o_ref[...] = jnp.concatenate(outs, axis=-1)


def _diff_prompt(q, k, vt, dl, subln, b, t, lam_init):
    n, w = q.shape
    hw = 2 * DH_A
    tk = vt.shape[2]
    tq = min(ATTN_Q_TILE, tk)
    assert tk % tq == 0 and t % tk == 0
    nq = t // tq
    return pl.pallas_call(
        functools.partial(_diff_prompt_kernel, tq=tq, tk=tk, lam_init=lam_init),
        grid=(b, nq),
        in_specs=[pl.BlockSpec(dl.shape, lambda bi, i: (0, 0)),
                  pl.BlockSpec((tq, w), lambda bi, i: (bi * nq + i, 0)),
                  pl.BlockSpec((t, w), lambda bi, i: (bi, 0)),
                  pl.BlockSpec((t // tk, w, tk), lambda bi, i: (bi, 0, 0)),
                  pl.BlockSpec((1, hw), lambda bi, i: (0, 0))],
        out_specs=pl.BlockSpec((tq, w), lambda bi, i: (bi * nq + i, 0)),
        out_shape=jax.ShapeDtypeStruct((n, w), F32),
        compiler_params=_params("parallel", "arbitrary"),
        name="diff_attn_prompt",
    )(dl, q, k, vt, subln.reshape(1, hw))


def _block_diag_queries(q8, groups, group_width):
    rows = groups * SUBLANES
    row_g = lax.broadcasted_iota(jnp.int32, (rows, 1), 0) // SUBLANES
    lane_g = lax.broadcasted_iota(jnp.int32, (1, q8.shape[1]), 1) // group_width
    return jnp.where(lane_g == row_g, jnp.concatenate([q8] * groups, axis=0), 0.0).astype(BF16)


def _diff_sample_kernel(pt_ref, dl_ref, q_ref, kn_ref, vn_ref, g_ref, *refs, n_pages, lam_init):
    del pt_ref
    kt_refs, v_refs, o_ref = refs[:n_pages], refs[n_pages:2 * n_pages], refs[2 * n_pages]
    width = q_ref.shape[2]
    groups = width // DH_A
    hw = 2 * DH_A
    qbd = _block_diag_queries(q_ref[0] * (DH_A ** -0.5), groups, DH_A)
    row_t = lax.broadcasted_iota(jnp.int32, (groups * SUBLANES, 1), 0) % SUBLANES
    col = lax.broadcasted_iota(jnp.int32, (1, SUBLANES), 1)
    s_pages = [_dot(qbd, kt[0].astype(BF16)) for kt in kt_refs]
    s_new = jnp.where(col <= row_t, _dot_nt(qbd, kn_ref[0].astype(BF16)), NEG)
    m = jnp.max(s_new, -1, keepdims=True)
    for s in s_pages:
        m = jnp.maximum(m, jnp.max(s, -1, keepdims=True))
    p_new = jnp.exp(s_new - m)
    den = jnp.sum(p_new, -1, keepdims=True)
    p_pages = []
    for s in s_pages:
        pr = jnp.exp(s - m)
        den = den + jnp.sum(pr, -1, keepdims=True)
        p_pages.append(pr.astype(BF16))
    p_new = p_new.astype(BF16)
    lam = _diff_lambda(dl_ref, lam_init)
    vn = vn_ref[0].astype(BF16)
    n_heads = width // hw
    page = v_refs[0].shape[1]
    spread = jnp.where(lax.broadcasted_iota(jnp.int32, (page, page * n_heads), 1) // n_heads
                       == lax.broadcasted_iota(jnp.int32, (page, page * n_heads), 0), 1.0, 0.0).astype(BF16)
    row_h = lax.broadcasted_iota(jnp.int32, (groups * SUBLANES, 1), 0) // (2 * SUBLANES)
    own = lax.broadcasted_iota(jnp.int32, (1, page * n_heads), 1) % n_heads == row_h
    p_wide = [jnp.where(own, _dot(pr, spread), 0.0).astype(BF16) for pr in p_pages]
    v_rows = [v_ref[0].reshape(page * n_heads, hw).astype(BF16) for v_ref in v_refs]
    acc = jnp.concatenate([_dot(p_new[2 * h * SUBLANES:2 * (h + 1) * SUBLANES], vn[:, h * hw:(h + 1) * hw])
                           for h in range(n_heads)], axis=0)
    for pw_, vr in zip(p_wide, v_rows):
        acc = acc + _dot(pw_, vr)
    a = acc / den
    outs = []
    for h in range(n_heads):
        r0 = 2 * h * SUBLANES
        outs.append(_head_rms(a[r0:r0 + SUBLANES] - lam * a[r0 + SUBLANES:r0 + 2 * SUBLANES], g_ref[...], lam_init))
    o_ref[0] = jnp.concatenate(outs, axis=-1)


def _diff_sample(q8, kn8, vn8, pool_kt, pool_v, page_table, dl, subln, lam_init):
    bs, _, w = q8.shape
    n_pages = page_table.shape[1]
    tok = pl.BlockSpec((1, SUBLANES, w), lambda b, pt: (b, 0, 0))
    kspecs = [pl.BlockSpec((1,) + pool_kt.shape[1:], lambda b, pt, j=j: (pt[b, j], 0, 0)) for j in range(n_pages)]
    vspecs = [pl.BlockSpec((1,) + pool_v.shape[1:], lambda b, pt, j=j: (pt[b, j], 0, 0, 0)) for j in range(n_pages)]
    return pl.pallas_call(
        functools.partial(_diff_sample_kernel, n_pages=n_pages, lam_init=lam_init),
        grid_spec=pltpu.PrefetchScalarGridSpec(
            num_scalar_prefetch=1,
            grid=(bs,),
            in_specs=[pl.BlockSpec(dl.shape, lambda b, pt: (0, 0)), tok, tok, tok,
                      pl.BlockSpec((1, 2 * DH_A), lambda b, pt: (0, 0))] + kspecs + vspecs,
            out_specs=tok),
        out_shape=jax.ShapeDtypeStruct((bs, SUBLANES, w), F32),
        compiler_params=_params("parallel"),
        name="diff_attn_sample",
    )(page_table, dl, q8, kn8, vn8, subln.reshape(1, 2 * DH_A), *([pool_kt] * n_pages), *([pool_v] * n_pages))


def _s5_prep_kernel(are_ref, aim_ref, ldt_ref, bre_ref, bim_ref, lr_ref, li_ref, bbr_ref, bbi_ref):
    a_re, a_im = are_ref[...], aim_ref[...]
    dt = jnp.exp(ldt_ref[...])
    mag = jnp.exp(a_re * dt)
    lam_re, lam_im = mag * jnp.cos(a_im * dt), mag * jnp.sin(a_im * dt)
    den = a_re * a_re + a_im * a_im
    nr = lam_re - 1.0
    f_re = (nr * a_re + lam_im * a_im) / den
    f_im = (lam_im * a_re - nr * a_im) / den
    lr_ref[...] = lam_re
    li_ref[...] = lam_im
    for g in range(a_re.shape[0]):
        fr, fi = f_re[g:g + 1, :], f_im[g:g + 1, :]
        br, bi = bre_ref[g], bim_ref[g]
        bbr_ref[g] = fr * br - fi * bi
        bbi_ref[g] = fr * bi + fi * br


def _s5_prep(a_re, a_im, log_dt, b_re, b_im):
    g, p = a_re.shape
    c = b_re.shape[-1]
    bt = lambda b: jnp.transpose(b, (0, 2, 1))
    sd = jax.ShapeDtypeStruct
    return pl.pallas_call(
        _s5_prep_kernel,
        out_shape=[sd((g, p), F32), sd((g, p), F32), sd((g, c, p), F32), sd((g, c, p), F32)],
        name="s5_prep",
    )(a_re, a_im, log_dt.reshape(g, 1), bt(b_re), bt(b_im))


S5_SPLIT = 2


def _dot_block_diag(x, w_ref):
    kx, kw = x.shape[1] // S5_SPLIT, w_ref.shape[1] // S5_SPLIT
    return jnp.concatenate([_dot(x[:, i * kx:(i + 1) * kx], w_ref[i * kx:(i + 1) * kx, i * kw:(i + 1) * kw])
                            for i in range(S5_SPLIT)], axis=-1)


def _s5_tail(u, hr, hi, ccr, cci, d, gw, gb):
    y = _dot_block_diag(hr.astype(BF16), ccr) - _dot_block_diag(hi.astype(BF16), cci) + d * u
    z = jax.nn.gelu(y)
    return z * _sigmoid(_dot(z.astype(BF16), gw) + gb)


def _s5_prompt_kernel(u_ref, bbr_ref, bbi_ref, lr_ref, li_ref, h0r_ref, h0i_ref, ccr_ref, cci_ref,
                      d_ref, gw_ref, gb_ref, y_ref, hr_out, hi_out, xr, xi, cr, ci):
    i = pl.program_id(1)
    tc = u_ref.shape[0]

    @pl.when(i == 0)
    def _():
        cr[...] = h0r_ref[0]
        ci[...] = h0i_ref[0]

    u = u_ref[...]
    ub = u.astype(BF16)
    xr[...] = _dot_block_diag(ub, bbr_ref)
    xi[...] = _dot_block_diag(ub, bbi_ref)
    lr, li = lr_ref[...], li_ref[...]

    def step(t, carry):
        hr, hi = carry
        nhr = lr * hr - li * hi + xr[pl.ds(t, 1), :]
        nhi = lr * hi + li * hr + xi[pl.ds(t, 1), :]
        xr[pl.ds(t, 1), :] = nhr
        xi[pl.ds(t, 1), :] = nhi
        return nhr, nhi

    hr, hi = lax.fori_loop(0, tc, step, (cr[...], ci[...]), unroll=8)
    cr[...] = hr
    ci[...] = hi
    y_ref[...] = _s5_tail(u, xr[...], xi[...], ccr_ref[...], cci_ref[...], d_ref[...], gw_ref[...], gb_ref[...])

    @pl.when(i == pl.num_programs(1) - 1)
    def _():
        hr_out[0] = hr
        hi_out[0] = hi


def _s5_prompt(u, b, t, h0r, h0i, prm):
    bbr, bbi, lr, li, ccr, cci, d, gw, gb = prm
    n, ch = u.shape
    ns = lr.shape[1]
    tc = min(512, t)
    nt = t // tc
    st = pl.BlockSpec((1, 1, ns), lambda bi, i: (bi, 0, 0))
    sd = jax.ShapeDtypeStruct
    return pl.pallas_call(
        _s5_prompt_kernel,
        grid=(b, nt),
        in_specs=[pl.BlockSpec((tc, ch), lambda bi, i: (bi * nt + i, 0)),
                  _const_spec(bbr.shape), _const_spec(bbi.shape), _const_spec(lr.shape), _const_spec(li.shape),
                  st, st, _const_spec(ccr.shape), _const_spec(cci.shape), _const_spec(d.shape),
                  _const_spec(gw.shape), _const_spec(gb.shape)],
        out_specs=[pl.BlockSpec((tc, ch), lambda bi, i: (bi * nt + i, 0)), st, st],
        out_shape=[sd((n, ch), F32), sd((b, 1, ns), F32), sd((b, 1, ns), F32)],
        scratch_shapes=[pltpu.VMEM((tc, ns), F32), pltpu.VMEM((tc, ns), F32),
                        pltpu.VMEM((1, ns), F32), pltpu.VMEM((1, ns), F32)],
        compiler_params=_params("parallel", "arbitrary"),
        name="s5_prompt",
    )(u, bbr, bbi, lr, li, h0r, h0i, ccr, cci, d, gw, gb)


def _s5_sample_kernel(u_ref, bbr_ref, bbi_ref, lr_ref, li_ref, h0r_ref, h0i_ref, ccr_ref, cci_ref,
                      d_ref, gw_ref, gb_ref, y_ref, hr_out, hi_out):
    lr, li = lr_ref[...], li_ref[...]
    hr, hi = h0r_ref[...], h0i_ref[...]
    for t in range(u_ref.shape[0]):
        u = u_ref[t]
        ub = u.astype(BF16)
        xr = _dot_block_diag(ub, bbr_ref)
        xi = _dot_block_diag(ub, bbi_ref)
        hr, hi = lr * hr - li * hi + xr, lr * hi + li * hr + xi
        y_ref[t] = _s5_tail(u, hr, hi, ccr_ref[...], cci_ref[...], d_ref[...], gw_ref[...], gb_ref[...])
    hr_out[...] = hr
    hi_out[...] = hi


def _s5_sample(u_tm, h0r, h0i, prm):
    bbr, bbi, lr, li, ccr, cci, d, gw, gb = prm
    sd = jax.ShapeDtypeStruct
    return pl.pallas_call(
        _s5_sample_kernel,
        out_shape=[sd(u_tm.shape, F32), sd(h0r.shape, F32), sd(h0i.shape, F32)],
        compiler_params=pltpu.CompilerParams(vmem_limit_bytes=VMEM_LIMIT_BYTES),
        name="s5_sample",
    )(u_tm, bbr, bbi, lr, li, h0r, h0i, ccr, cci, d, gw, gb)


def _dil_prompt_kernel(q_ref, kp_ref, kc_ref, vp_ref, vc_ref, o_ref, l_ref):
    n = pl.program_id(1)
    blk = C_BLK
    nsub = q_ref.shape[1] // blk
    w = q_ref.shape[2]
    pw = 2 * DH_C
    kall = jnp.concatenate([kp_ref[0], kc_ref[0]], axis=0).astype(BF16)
    vall = jnp.concatenate([vp_ref[0], vc_ref[0]], axis=0).astype(BF16)
    qi = lax.broadcasted_iota(jnp.int32, (blk, 1), 0) + blk
    ki = lax.broadcasted_iota(jnp.int32, (1, 2 * blk), 1)
    dist = qi - ki
    band = (dist >= 0) & (dist <= blk)
    lo_k = jnp.where(n > 0, 0, blk)
    oks = [band & (ki >= lo_k)] + [band] * (nsub - 1)
    oks = [jnp.concatenate([ok, ok], axis=0) for ok in oks]
    first = lax.broadcasted_iota(jnp.int32, (1, pw), 1) < DH_C
    units = [(j, slice(p * pw, (p + 1) * pw)) for j in range(nsub) for p in range(w // pw)]
    rows = lambda j: slice(j * blk, (j + 2) * blk)
    qbd = []
    for j, sl in units:
        q = q_ref[0, j * blk:(j + 1) * blk, sl] * (DH_C ** -0.5)
        qbd.append(jnp.concatenate([jnp.where(first, q, 0.0), jnp.where(first, 0.0, q)], axis=0).astype(BF16))
    s = [jnp.where(oks[j], _dot_nt(qb, kall[rows(j), sl]), NEG) for qb, (j, sl) in zip(qbd, units)]
    m = [jnp.max(x, -1, keepdims=True) for x in s]
    pr = [jnp.exp(x - mm) for x, mm in zip(s, m)]
    den = [jnp.sum(x, -1, keepdims=True) for x in pr]
    oh = [_dot((x / d).astype(BF16), vall[rows(j), sl]) for x, d, (j, sl) in zip(pr, den, units)]
    lh = [mm + jnp.log(d) for mm, d in zip(m, den)]
    npair = w // pw
    for j in range(nsub):
        o_ref[0, j * blk:(j + 1) * blk, :] = jnp.concatenate(
            [jnp.where(first, x[:blk], x[blk:]) for x in oh[j * npair:(j + 1) * npair]], axis=-1)
        l_ref[0, j * blk:(j + 1) * blk, :] = jnp.concatenate(
            [jnp.where(first, x[:blk], x[blk:]) for x in lh[j * npair:(j + 1) * npair]], axis=-1)


def _dil_prompt(qd, kd, vd):
    bd, ns, w = qd.shape
    nsub = 2 if (ns // C_BLK) % 2 == 0 else 1
    cur = pl.BlockSpec((1, nsub * C_BLK, w), lambda b, n: (b, n, 0))
    prev = pl.BlockSpec((1, C_BLK, w), lambda b, n: (b, jnp.maximum(n * nsub - 1, 0), 0))
    sd = jax.ShapeDtypeStruct((bd, ns, w), F32)
    return pl.pallas_call(
        _dil_prompt_kernel,
        grid=(bd, ns // (nsub * C_BLK)),
        in_specs=[cur, prev, cur, prev, cur],
        out_specs=[cur, cur],
        out_shape=[sd, sd],
        compiler_params=_params("parallel", "parallel"),
        name="dilated_attn_prompt",
    )(qd, kd, kd, vd, vd)


def _dil_sample_kernel(q_ref, kn_ref, vn_ref, kt_ref, vt_ref, o_ref):
    w = q_ref.shape[2]
    nh = w // DH_C
    buf = kt_ref.shape[2]
    rows = nh * SUBLANES
    qbd = _block_diag_queries(q_ref[0] * (DH_C ** -0.5), nh, DH_C)
    row_q = lax.broadcasted_iota(jnp.int32, (rows, 1), 0) % SUBLANES

    def reach(delta):
        mult = jnp.zeros(delta.shape, F32)
        for d in C_DILATIONS:
            ok = (delta >= 0) & (lax.rem(delta, d) == 0) & (delta <= d * C_BLK)
            mult = mult + jnp.where(ok, 1.0, 0.0)
        return mult

    mult = reach(buf + row_q - lax.broadcasted_iota(jnp.int32, (1, buf), 1))
    mult_n = reach(row_q - lax.broadcasted_iota(jnp.int32, (1, SUBLANES), 1))
    s = jnp.where(mult > 0, _dot(qbd, kt_ref[0].astype(BF16)), NEG)
    s_n = jnp.where(mult_n > 0, _dot_nt(qbd, kn_ref[0].astype(BF16)), NEG)
    m = jnp.maximum(jnp.max(s, -1, keepdims=True), jnp.max(s_n, -1, keepdims=True))
    p = mult * jnp.exp(s - m)
    p_n = mult_n * jnp.exp(s_n - m)
    inv = 1.0 / (jnp.sum(p, -1, keepdims=True) + jnp.sum(p_n, -1, keepdims=True))
    o = _dot_nt((p * inv).astype(BF16), vt_ref[0].astype(BF16)) + _dot((p_n * inv).astype(BF16), vn_ref[0].astype(BF16))
    lane_h = lax.broadcasted_iota(jnp.int32, (1, w), 1) // DH_C
    out = jnp.zeros((SUBLANES, w), F32)
    for h in range(nh):
        out = out + jnp.where(lane_h == h, o[h * SUBLANES:(h + 1) * SUBLANES, :], 0.0)
    o_ref[0] = out


def _dil_sample(q8, kn8, vn8, cache_kt, cache_vt):
    bs, w, buf = cache_kt.shape
    tok = pl.BlockSpec((1, SUBLANES, w), lambda b: (b, 0, 0))
    cache = pl.BlockSpec((1, w, buf), lambda b: (b, 0, 0))
    return pl.pallas_call(
        _dil_sample_kernel,
        grid=(bs,),
        in_specs=[tok, tok, tok, cache, cache],
        out_specs=tok,
        out_shape=jax.ShapeDtypeStruct((bs, SUBLANES, w), F32),
        compiler_params=_params("parallel"),
        name="dilated_attn_sample",
    )(q8, kn8, vn8, cache_kt, cache_vt)


def _softplus(x):
    return jnp.maximum(x, 0.0) + jnp.log1p(jnp.exp(-jnp.abs(x)))


def _rwkv_pre_kernel(pd_ref, pv_ref, mu_ref, w0_ref, w2_ref, a0_ref, a2_ref, g2_ref, kk_ref, ka_ref, rk_ref, seg_ref,
                     r_o, ld_o, k_o, v_o, kk_o, b_o, g_o, bonus_o, *, seq_tiles):
    pd = pd_ref[...]
    if seq_tiles is None:
        prev = pv_ref[...]
    else:
        first = pl.program_id(0) % seq_tiles == 0
        before = jnp.where(first, 0.0, pv_ref[SUBLANES - 1:SUBLANES, :])
        row = lax.broadcasted_iota(jnp.int32, (pd.shape[0], 1), 0)
        prev = jnp.where(row == 0, before, pltpu.roll(pd, 1, 0))
    xm = pd + (prev - pd) * mu_ref[...]
    o1, o2, o3 = D_W, 2 * D_W, 3 * D_W
    o5 = o3 + D_LORA_W + D_LORA_A
    r, k, v = xm[:, :o1], xm[:, o1:o2], xm[:, o2:o3]
    wa, gl = xm[:, o3:o5], xm[:, o5:]
    lw = _dot(jnp.tanh(wa).astype(BF16), w2_ref[...])
    la = _dot(wa.astype(BF16), a2_ref[...])
    g = _dot(_sigmoid(gl).astype(BF16), g2_ref[...])
    w_log = -_softplus(-(w0_ref[...] + lw)) - 0.5
    a = _sigmoid(a0_ref[...] + la)
    seg = seg_ref[...]
    kk = k * kk_ref[...]
    kk = kk / jnp.maximum(jnp.sqrt(_dot_exact_rhs(kk * kk, seg)), 1e-12)
    k2 = k * (1.0 + (a - 1.0) * ka_ref[...])
    r_o[...] = r
    ld_o[...] = -jnp.exp(w_log)
    k_o[...] = k2
    v_o[...] = v
    kk_o[...] = kk
    b_o[...] = kk * a
    g_o[...] = g
    bonus_o[...] = _dot_exact_rhs(r * k2 * rk_ref[...], seg) * v


def _rwkv_pre(pd, prev, prm, seq_tiles=None):
    n, cols = pd.shape
    tm = min(TOKEN_TILE, n)
    tok = lambda wd: pl.BlockSpec((tm, wd), lambda i: (i, 0))
    if prev is None:
        prev = pd
        prev_spec = pl.BlockSpec((SUBLANES, cols), lambda i: (jnp.maximum(i * (tm // SUBLANES) - 1, 0), 0))
    else:
        prev_spec = tok(cols)
    return pl.pallas_call(
        functools.partial(_rwkv_pre_kernel, seq_tiles=seq_tiles),
        grid=(n // tm,),
        in_specs=[tok(cols), prev_spec] + [_const_spec(p.shape) for p in prm],
        out_specs=[tok(D_W)] * 8,
        out_shape=[jax.ShapeDtypeStruct((n, D_W), F32)] * 8,
        compiler_params=_params("parallel"),
        name="rwkv_pre",
    )(pd, prev, *prm)


def _rwkv_chunk_kernel(r_ref, ld_ref, k_ref, v_ref, kk_ref, b_ref, y_ref, s_out, st):
    c = pl.program_id(1)
    nb, ch, w = r_ref.shape
    pw = 2 * N_D
    wide = lambda ref: jnp.concatenate([ref[i] for i in range(nb)], axis=-1)

    @pl.when(c == 0)
    def _():
        st[...] = jnp.zeros(st.shape, F32)

    ri = lax.broadcasted_iota(jnp.int32, (ch, ch), 0)
    ci = lax.broadcasted_iota(jnp.int32, (ch, ch), 1)
    tri_incl = ci <= ri
    tri_strict = ci < ri
    eye_c = jnp.where(ci == ri, 1.0, 0.0)
    ld = wide(ld_ref)
    r_w, k_w, b_w = wide(r_ref), wide(k_ref), wide(b_ref)
    cum = _dot_exact_rhs_left(jnp.where(tri_incl, 1.0, 0.0).astype(BF16), ld)
    cum_end = cum[ch - 1:ch, :]
    g_inc = jnp.exp(cum)
    g_inv = jnp.exp(-cum)
    g_end = jnp.exp(cum_end - cum)
    rho = r_w * g_inc
    kap = wide(kk_ref) * jnp.exp(cum - ld)
    kh = k_w * g_inv
    bh = b_w * g_inv
    khg = k_w * g_end
    bhg = b_w * g_end
    gam_end = jnp.exp(cum_end)
    v = wide(v_ref)

    pi = lax.broadcasted_iota(jnp.int32, (pw, pw), 0)
    pj = lax.broadcasted_iota(jnp.int32, (pw, pw), 1)
    same_head = (pi // N_D) == (pj // N_D)
    lane = lax.broadcasted_iota(jnp.int32, (1, pw), 1)
    first = lane < N_D

    npair = nb * w // pw
    heads = [(pr, hh) for pr in range(npair) for hh in range(2)]
    sls = [slice(pr * pw, (pr + 1) * pw) for pr in range(npair)]
    sel = (first, jnp.logical_not(first))
    bf = lambda x: x.astype(BF16)
    kap_p = [kap[:, s] for s in sls]
    rho_p = [rho[:, s] for s in sls]
    kap_b = [bf(x) for x in kap_p]
    kh_b = [bf(kh[:, s]) for s in sls]
    bh_b = [bf(bh[:, s]) for s in sls]
    v_b = [bf(v[:, s]) for s in sls]
    kap_m = [bf(jnp.where(sel[hh], kap_p[pr], 0.0)) for pr, hh in heads]
    rho_m = [bf(jnp.where(sel[hh], rho_p[pr], 0.0)) for pr, hh in heads]
    a_b = [jnp.where(tri_strict, _dot_nt(kap_m[i], bh_b[pr]), 0.0) for i, (pr, _) in enumerate(heads)]
    a_k = [bf(jnp.where(tri_strict, _dot_nt(kap_m[i], kh_b[pr]), 0.0)) for i, (pr, _) in enumerate(heads)]
    ap_b = [bf(jnp.where(tri_incl, _dot_nt(rho_m[i], bh_b[pr]), 0.0)) for i, (pr, _) in enumerate(heads)]
    ap_k = [bf(jnp.where(tri_incl, _dot_nt(rho_m[i], kh_b[pr]), 0.0)) for i, (pr, _) in enumerate(heads)]
    tl = [eye_c - a for a in a_b]
    pw2 = [_dot(bf(a), bf(a)) for a in a_b]
    span = 2
    while span < ch:
        tl = [t + _dot(bf(t), bf(p2)) for t, p2 in zip(tl, pw2)]
        span *= 2
        if span < ch:
            pw2 = [_dot(bf(p2), bf(p2)) for p2 in pw2]
    tl_b = [bf(t) for t in tl]
    kap2_h = [_dot(tl_b[i], kap_b[pr]) for i, (pr, _) in enumerate(heads)]
    akv = [bf(_dot(a_k[i], v_b[pr])) for i, (pr, _) in enumerate(heads)]
    wr_h = [_dot(tl_b[i], akv[i]) for i in range(len(heads))]
    rho2_h = [rho_p[pr] - _dot(ap_b[i], bf(kap2_h[i])) for i, (pr, _) in enumerate(heads)]
    yloc_h = [_dot(ap_k[i], v_b[pr]) - _dot(ap_b[i], bf(wr_h[i])) for i, (pr, _) in enumerate(heads)]
    pair = lambda xs: [jnp.where(first, xs[2 * pr], xs[2 * pr + 1]) for pr in range(npair)]
    kap2, wr, rho2, yloc = pair(kap2_h), pair(wr_h), pair(rho2_h), pair(yloc_h)
    bhg_b = [bf(bhg[:, s]) for s in sls]
    khg_b = [bf(khg[:, s]) for s in sls]
    diag = [jnp.where(pi == pj, jnp.broadcast_to(gam_end[:, s], (pw, pw)), 0.0) for s in sls]
    phi = [jnp.where(same_head, diag[pr] - _dot_tn(bhg_b[pr], bf(kap2[pr])), 0.0) for pr in range(npair)]
    gmat = [jnp.where(same_head, _dot_tn(khg_b[pr], v_b[pr]) - _dot_tn(bhg_b[pr], bf(wr[pr])), 0.0)
            for pr in range(npair)]
    s_b = [bf(st[pr]) for pr in range(npair)]
    per_b = w // pw
    for pr in range(npair):
        y_ref[pr // per_b, :, (pr % per_b) * pw:(pr % per_b + 1) * pw] = _dot(bf(rho2[pr]), s_b[pr]) + yloc[pr]
    for pr in range(npair):
        st[pr] = _dot(bf(phi[pr]), s_b[pr]) + gmat[pr]

    @pl.when(c == pl.num_programs(1) - 1)
    def _():
        for pr in range(npair):
            s_out[pr // per_b, pr % per_b] = st[pr]


def _dot_exact_rhs_left(m01, x):
    hi, mid, lo = _split3(x)
    return _dot(m01, hi) + _dot(m01, mid) + _dot(m01, lo)


def _rwkv_chunk(r, ld, k2, v, kk, bb, b, t):
    n, w = r.shape
    ch = min(RWKV_CHUNK, t)
    nc = t // ch
    npair = w // (2 * N_D)
    nb = next(d for d in (4, 2, 1) if b % d == 0)
    tok = pl.BlockSpec((nb, ch, w), lambda bi, c: (bi, c, 0))
    seq = lambda z: z.reshape(b, t, w)
    y, s_pairs = pl.pallas_call(
        _rwkv_chunk_kernel,
        grid=(b // nb, nc),
        in_specs=[tok] * 6,
        out_specs=[tok, pl.BlockSpec((nb, npair, 2 * N_D, 2 * N_D), lambda bi, c: (bi, 0, 0, 0))],
        out_shape=[jax.ShapeDtypeStruct((b, t, w), F32), jax.ShapeDtypeStruct((b, npair, 2 * N_D, 2 * N_D), F32)],
        scratch_shapes=[pltpu.VMEM((nb * npair, 2 * N_D, 2 * N_D), F32)],
        compiler_params=_params("parallel", "arbitrary"),
        name="rwkv_chunk_scan",
    )(seq(r), seq(ld), seq(k2), seq(v), seq(kk), seq(bb))
    return y.reshape(n, w), s_pairs


def _rwkv_lane_kernel(r_ref, ld_ref, k_ref, v_ref, kk_ref, b_ref, s_ref, y_ref, s_out):
    steps = r_ref.shape[0]

    def body(vi, carry):
        s = s_ref[0, vi]
        for t in range(steps):
            sk = jnp.sum(s * kk_ref[t, 0], axis=0, keepdims=True)
            vv = v_ref[t, 0, pl.ds(vi, 1), :]
            s = s * jnp.exp(ld_ref[t, 0]) - sk * b_ref[t, 0] + vv * k_ref[t, 0]
            y_ref[t, 0, pl.ds(vi, 1), :] = jnp.sum(s * r_ref[t, 0], axis=0, keepdims=True)
        s_out[0, vi] = s
        return carry

    lax.fori_loop(0, s_ref.shape[1], body, 0)


def _rwkv_lane(rt, ldt, kt, vt, kkt, bt, s0):
    steps, nh, nd, bs = rt.shape
    tok = pl.BlockSpec((steps, 1, nd, bs), lambda h: (0, h, 0, 0))
    stt = pl.BlockSpec((1, nd, nd, bs), lambda h: (h, 0, 0, 0))
    return pl.pallas_call(
        _rwkv_lane_kernel,
        grid=(nh,),
        in_specs=[tok] * 6 + [stt],
        out_specs=[tok, stt],
        out_shape=[jax.ShapeDtypeStruct(rt.shape, F32), jax.ShapeDtypeStruct(s0.shape, F32)],
        compiler_params=_params("parallel"),
        name="rwkv_lane_scan",
    )(rt, ldt, kt, vt, kkt, bt, s0)


def _rwkv_post_kernel(y_ref, bonus_ref, g_ref, gw_ref, gb_ref, seg_ref, o_ref):
    y = y_ref[...]
    seg = seg_ref[...]
    mu = _dot_exact_rhs(y, seg) * (1.0 / N_D)
    yc = y - mu
    var = _dot_exact_rhs(yc * yc, seg) * (1.0 / N_D)
    yn = yc * lax.rsqrt(var + GN_EPS) * gw_ref[...] + gb_ref[...]
    o_ref[...] = (yn + bonus_ref[...]) * g_ref[...]


def _rwkv_post(y, bonus, g, gn_w, gn_b, seg):
    n, w = y.shape
    tm = min(TOKEN_TILE, n)
    tok = pl.BlockSpec((tm, w), lambda i: (i, 0))
    return pl.pallas_call(
        _rwkv_post_kernel,
        grid=(n // tm,),
        in_specs=[tok, tok, tok, _const_spec((1, w)), _const_spec((1, w)), _const_spec(seg.shape)],
        out_specs=tok,
        out_shape=jax.ShapeDtypeStruct((n, w), F32),
        compiler_params=_params("parallel"),
        name="rwkv_post",
    )(y, bonus, g, gn_w.reshape(1, w), gn_b.reshape(1, w), seg)


def _pad_tokens(x, bs, s_len):
    x = x.reshape(bs, s_len, x.shape[-1])
    return jnp.pad(x, ((0, 0), (0, SUBLANES - s_len), (0, 0)))


def _block_diag_in(bb):
    g, c, p = bb.shape
    return jnp.einsum('gcp,gh->gchp', bb, jnp.eye(g, dtype=bb.dtype)).reshape(g * c, g * p)


def _block_diag_out(cc):
    g, c, p = cc.shape
    return jnp.einsum('gcp,gh->gphc', cc, jnp.eye(g, dtype=cc.dtype)).reshape(g * p, g * c)


def _shifted(pd, shift0, b, t):
    pd3 = pd.reshape(b, t, pd.shape[-1])
    return jnp.concatenate([shift0[:, None, :], pd3[:, :-1]], axis=1).reshape(b * t, pd.shape[-1])


def _trunk(x, mod_all, row0, rows, per_token_mod, sample, st, p):
    b, t, d = x.shape
    n = b * t
    xt = x.reshape(n, d)
    tm = min(TOKEN_TILE, n)
    outs = {}
    for l in range(DEPTH):
        mod = mod_all[l, row0:row0 + rows]
        if per_token_mod:
            mod_arr = jnp.repeat(mod, t, axis=0).reshape(n // tm, tm, 9 * d)
        else:
            mod_arr = mod.reshape(rows * 9, 1, d)

        def mods(i, mod_arr=mod_arr):
            def one(c):
                if per_token_mod:
                    return _Mod(mod_arr, pl.BlockSpec((1, tm, d), lambda ti: (ti, 0, c)))
                return _Mod(mod_arr, pl.BlockSpec((1, 1, d), lambda ti: ((ti // (t // tm)) * 9 + c, 0, 0)))
            return [one(3 * i + j) for j in range(3)], t // tm

        m0, tpg = mods(0)
        xt = _ffn(xt, m0, tpg, p['wg'][l][0], p['wu'][l][0], p['wd'][l][0], p['ln_g'][l, 0], p['ln_b'][l, 0])
        m1, _ = mods(1)
        if l % 2 == 0:
            e = l // 2
            lam_init = 0.8 - 0.6 * math.exp(-0.3 * l)
            widths = (512, 512, 512, S5_CH)
            dl, subln = p['diff_lambda'][e], p['diff_subln'][e]
            if sample:
                q, k, v, u = _inproj(xt, m1, tpg, p['even_w_in'][e], widths)
            else:
                q, k, v, u, vt, kt = _inproj(xt, m1, tpg, p['even_w_in'][e], widths,
                                             t_weights=((p['even_wv_t'][e], True), (p['even_wk_t'][e], False)),
                                             t_block=min(ATTN_K_BLOCK, t), seq_tiles=t // tm)
            if sample:
                att8 = _diff_sample(_pad_tokens(q, b, t), _pad_tokens(k, b, t), _pad_tokens(v, b, t),
                                    st['pool_k'][e], st['pool_v'][e], st['page_table'], dl, subln, lam_init)
                att = att8[:, :t].reshape(n, -1)
                u_tm = u.reshape(b, t, -1).transpose(1, 0, 2)
                y_tm, hr, hi = _s5_sample(u_tm, st['s5_re'][e].reshape(b, S5_N), st['s5_im'][e].reshape(b, S5_N),
                                          p['s5'][e])
                y5 = y_tm.transpose(1, 0, 2).reshape(n, -1)
            else:
                att = _diff_prompt(q, k, vt, dl, subln, b, t, lam_init)
                zero = jnp.zeros((b, 1, S5_N), F32)
                y5, hr, hi = _s5_prompt(u, b, t, zero, zero, p['s5'][e])
            if sample:
                k_out = k.reshape(b, t, H_A, 2, DH_A)
            else:
                k_out = kt.reshape(b, H_A, 2, DH_A, t).transpose(0, 4, 1, 2, 3)
            outs.setdefault('ak', []).append(k_out)
            outs.setdefault('av', []).append(v.reshape(b, t, H_A, 2 * DH_A))
            outs.setdefault('s5r', []).append(hr.reshape(b, S5_G, S5_P))
            outs.setdefault('s5i', []).append(hi.reshape(b, S5_G, S5_P))
            xt = _outproj(xt, m1[2], tpg, [att], y5, p['even_w_out'][e], p['ln_g'][l, 1], p['ln_b'][l, 1])
        else:
            o = l // 2
            widths = (512, 512, 512, pd_cols(p))
            rp = p['rwkv'][o]
            dil_of = None
            if sample:
                q, k, v, pd = _inproj(xt, m1, tpg, p['odd_w_in'][o], widths)
            else:
                dils = C_DILATIONS[1:]
                res = _inproj(xt, m1, tpg, p['odd_w_in'][o], widths,
                              t_weights=((p['odd_wk_t'][o], False), (p['odd_wv_t'][o], False)),
                              seq_tiles=t // tm, n_dil=3, dils=dils, dil_dtype=BF16)
                q, k, v, pd = res[:4]
                split = {dd: res[4 + 3 * i:7 + 3 * i] for i, dd in enumerate(dils)}
                kt, vt = res[4 + 3 * len(dils):]
            if sample:
                att8 = _dil_sample(_pad_tokens(q, b, t), _pad_tokens(k, b, t), _pad_tokens(v, b, t),
                                   st['cache_c_k'][o], st['cache_c_v'][o])
                atts = [att8[:, :t].reshape(n, -1)]
                prev = _shifted(pd, st['d_shift'][o], b, t)
                r, ld, k2, vv, kk, bb, g, bonus = _rwkv_pre(pd, prev, rp['pre'])
                tl = lambda z: z.reshape(b, t, H_D, N_D).transpose(1, 2, 3, 0)
                s0 = st['d_wkv'][o].transpose(1, 2, 3, 0)
                y_l, s_new = _rwkv_lane(tl(r), tl(ld), tl(k2), tl(vv), tl(kk), tl(bb), s0)
                y = y_l.transpose(3, 0, 1, 2).reshape(n, D_W)
                s_new = s_new.transpose(3, 0, 1, 2)
                k_keep, v_keep = k.reshape(b, t, H_C, DH_C), v.reshape(b, t, H_C, DH_C)
            else:
                atts_o, atts_l = [], []
                for dil in C_DILATIONS:
                    if dil == 1:
                        ob, lb = _dil_prompt(q.reshape(b, t, -1), k.reshape(b, t, -1), v.reshape(b, t, -1))
                        ob, lb = ob.reshape(n, -1), lb.reshape(n, -1)
                    else:
                        ob, lb = _dil_prompt(*[z.reshape(b * dil, t // dil, -1) for z in split[dil]])
                        ob, lb = ob.reshape(b, dil, t // dil, -1), lb.reshape(b, dil, t // dil, -1)
                    atts_o.append(ob)
                    atts_l.append(lb)
                atts = atts_o + atts_l
                dil_of = C_DILATIONS + C_DILATIONS
                r, ld, k2, vv, kk, bb, g, bonus = _rwkv_pre(pd, None, rp['pre'], seq_tiles=t // tm)
                y, s_pairs = _rwkv_chunk(r, ld, k2, vv, kk, bb, b, t)
                sp = s_pairs.reshape(b, H_D // 2, 2, N_D, 2, N_D)
                s_new = jnp.stack([sp[:, :, 0, :, 0, :], sp[:, :, 1, :, 1, :]], axis=2)
                s_new = s_new.reshape(b, H_D, N_D, N_D).transpose(0, 1, 3, 2)
                keep = min(C_BLK * C_DILATIONS[-1], t)
                keep_t = lambda zt: zt[:, :, t - keep:].reshape(b, H_C, DH_C, keep).transpose(0, 3, 1, 2)
                k_keep, v_keep = keep_t(kt), keep_t(vt)
            yd = _rwkv_post(y, bonus, g, rp['gn_w'], rp['gn_b'], rp['seg'])
            outs.setdefault('ck', []).append(k_keep)
            outs.setdefault('cv', []).append(v_keep)
            outs.setdefault('dw', []).append(s_new)
            outs.setdefault('ds', []).append(pd.reshape(b, t, -1)[:, -1])
            xt = _outproj(xt, m1[2], tpg, atts, yd, p['odd_w_out'][o], p['ln_g'][l, 1], p['ln_b'][l, 1],
                          dil_of=dil_of, seq_tiles=t // tm)
        m2, _ = mods(2)
        xt = _ffn(xt, m2, tpg, p['wg'][l][1], p['wu'][l][1], p['wd'][l][1], p['ln_g'][l, 2], p['ln_b'][l, 2])
    stacked = [jnp.stack(outs[key], 0) for key in ('ak', 'av', 's5r', 's5i', 'ck', 'cv', 'dw', 'ds')]
    return xt.reshape(b, t, d), stacked


def pd_cols(p):
    return p['odd_w_in'].shape[-1] - 3 * H_C * DH_C


def kernel(x_prompt, x_sample, cache_a_k, cache_a_v, state_s5_re, state_s5_im, cache_c_k, cache_c_v, state_d_wkv, state_d_shift, page_table, c_prompt, c_sample, ada_w, ada_b, ln_g, ln_b, ffn_w_gate, ffn_w_up, ffn_w_down, even_w_in, even_w_out, diff_lambda, diff_subln, s5_a_re, s5_a_im, s5_log_dt, s5_b_re, s5_b_im, s5_c_re, s5_c_im, s5_d, s5_glu_w, s5_glu_b, odd_w_in, odd_w_out, rwkv_mu, rwkv_w0, rwkv_w2, rwkv_a0, rwkv_a2, rwkv_g2, rwkv_k_k, rwkv_k_a, rwkv_r_k, rwkv_gn_w, rwkv_gn_b):
    bp, bs = x_prompt.shape[0], x_sample.shape[0]
    n_even, n_odd = even_w_in.shape[0], odd_w_in.shape[0]
    bf = lambda w: w.astype(BF16)

    seg = jnp.kron(jnp.eye(H_D, dtype=F32), jnp.ones((N_D, N_D), F32)).astype(BF16)
    s5 = []
    for e in range(n_even):
        lr, li, bbr, bbi = _s5_prep(s5_a_re[e], s5_a_im[e], s5_log_dt[e], s5_b_re[e], s5_b_im[e])
        s5.append((bf(_block_diag_in(bbr)), bf(_block_diag_in(bbi)), lr.reshape(1, S5_N), li.reshape(1, S5_N),
                   bf(_block_diag_out(s5_c_re[e])), bf(_block_diag_out(s5_c_im[e])),
                   s5_d[e].reshape(1, S5_CH), bf(s5_glu_w[e]), s5_glu_b[e].reshape(1, S5_CH)))
    rwkv = []
    for o in range(n_odd):
        row = lambda z: z.reshape(1, -1)
        w2p = jnp.concatenate([rwkv_w2[o], jnp.zeros_like(rwkv_a2[o])], axis=0)
        a2p = jnp.concatenate([jnp.zeros_like(rwkv_w2[o]), rwkv_a2[o]], axis=0)
        pre = (row(rwkv_mu[o]), row(rwkv_w0[o]), bf(w2p), row(rwkv_a0[o]), bf(a2p), bf(rwkv_g2[o]),
               row(rwkv_k_k[o]), row(rwkv_k_a[o]), row(rwkv_r_k[o]), seg)
        rwkv.append(dict(pre=pre, gn_w=rwkv_gn_w[o], gn_b=rwkv_gn_b[o], seg=seg))
    p = dict(wg=bf(ffn_w_gate), wu=bf(ffn_w_up), wd=bf(ffn_w_down), ln_g=ln_g, ln_b=ln_b,
             even_w_in=bf(even_w_in), even_w_out=bf(even_w_out), odd_w_in=bf(odd_w_in), odd_w_out=bf(odd_w_out),
             even_wv_t=bf(jnp.swapaxes(even_w_in[:, :, 2 * A_W:3 * A_W], 1, 2)),
             even_wk_t=bf(jnp.swapaxes(even_w_in[:, :, A_W:2 * A_W], 1, 2)),
             odd_wk_t=bf(jnp.swapaxes(odd_w_in[:, :, C_W:2 * C_W], 1, 2)),
             odd_wv_t=bf(jnp.swapaxes(odd_w_in[:, :, 2 * C_W:3 * C_W], 1, 2)),
             diff_lambda=diff_lambda, diff_subln=diff_subln, s5=s5, rwkv=rwkv)

    mod_all = _ada(jnp.concatenate([c_prompt, c_sample], axis=0), ada_w, ada_b)

    y_prompt, st_p = _trunk(x_prompt, mod_all, 0, bp, False, False, None, p)
    n_pool, page = cache_a_k.shape[1], cache_a_k.shape[2]
    win_buf = cache_c_k.shape[2]
    pos_minor = lambda c: jnp.transpose(c, (0, 1, 3, 4, 2)).reshape(n_odd, bs, -1, win_buf)
    st = dict(pool_k=jnp.transpose(cache_a_k, (0, 1, 3, 4, 5, 2)).reshape(n_even, n_pool, -1, page),
              pool_v=cache_a_v,
              page_table=page_table, s5_re=state_s5_re, s5_im=state_s5_im,
              cache_c_k=pos_minor(cache_c_k), cache_c_v=pos_minor(cache_c_v),
              d_wkv=state_d_wkv, d_shift=state_d_shift)
    y_sample, st_s = _trunk(x_sample, mod_all, bp, bs, True, True, st, p)
    a_k_p, a_v_p, s5_re_p, s5_im_p, c_k_p, c_v_p, d_wkv_p, d_shift_p = st_p
    a_k_s, a_v_s, s5_re_s, s5_im_s, c_k_s, c_v_s, d_wkv_s, d_shift_s = st_s
    return (y_prompt, y_sample, a_k_p, a_k_s, a_v_p, a_v_s, s5_re_p, s5_re_s, s5_im_p, s5_im_s,
            c_k_p, c_k_s, c_v_p, c_v_s, d_wkv_p, d_wkv_s, d_shift_p, d_shift_s)
```

```python
import functools
import math

import jax
import jax.numpy as jnp
from jax import lax
from jax.experimental import pallas as pl
from jax.experimental.pallas import tpu as pltpu

F32 = jnp.float32
BF16 = jnp.bfloat16

DEPTH = 2
H_A, DH_A = 4, 64
A_W = H_A * 2 * DH_A
S5_GROUP, S5_G, S5_P = 16, 32, 64
S5_CH = S5_GROUP * S5_G
S5_N = S5_G * S5_P
H_C, DH_C = 8, 64
C_W = H_C * DH_C
C_BLK = 128
C_DILATIONS = (1, 4, 16)
H_D, N_D = 8, 64
D_W = H_D * N_D
D_LORA_W, D_LORA_A, D_LORA_G = 64, 64, 128
GN_EPS = 64e-5
ALPHA = (2.0 * DEPTH) ** 0.25
LN_EPS = 1e-5
NEG = -1e30

LANES = 128
SUBLANES = 8
VMEM_LIMIT_BYTES = 56 * 1024 * 1024
TOKEN_TILE = 512
RWKV_CHUNK = 64
ATTN_Q_TILE = 256
ATTN_K_BLOCK = 512


def _params(*sem):
    return pltpu.CompilerParams(dimension_semantics=sem, vmem_limit_bytes=VMEM_LIMIT_BYTES)


def _const_spec(shape):
    nd = len(shape)
    return pl.BlockSpec(shape, lambda *_: (0,) * nd, pipeline_mode=pl.Buffered(1))


def _dot(a, b):
    return jnp.dot(a, b, preferred_element_type=F32)


def _dot_nt(a, b):
    return lax.dot_general(a, b, (((1,), (1,)), ((), ())), preferred_element_type=F32)


def _dot_tn(a, b):
    return lax.dot_general(a, b, (((0,), (0,)), ((), ())), preferred_element_type=F32)


def _split3(x):
    hi = x.astype(BF16)
    r1 = x - hi.astype(F32)
    mid = r1.astype(BF16)
    lo = (r1 - mid.astype(F32)).astype(BF16)
    return hi, mid, lo


def _dot_exact_rhs(x, m01):
    hi, mid, lo = _split3(x)
    return _dot(hi, m01) + _dot(mid, m01) + _dot(lo, m01)


def _layer_norm(y, g, b):
    mu = jnp.mean(y, -1, keepdims=True)
    yc = y - mu
    var = jnp.mean(yc * yc, -1, keepdims=True)
    return yc * lax.rsqrt(var + LN_EPS) * g + b


def _sigmoid(x):
    return jax.nn.sigmoid(x)


class _Mod:
    def __init__(self, arr, spec):
        self.arr, self.spec = arr, spec


def _mod_spec(mod, tiles_per_group):
    del tiles_per_group
    return mod.spec


def _ada_kernel(c_ref, w_ref, b_ref, o_ref):
    c = c_ref[...]
    s = (c * _sigmoid(c)).astype(BF16)
    o_ref[0] = _dot(s, w_ref[0].astype(BF16)) + b_ref[0]


def _ada(c_all, ada_w, ada_b):
    nl, d, w = ada_w.shape
    r = c_all.shape[0]
    tn = 1152 if w % 1152 == 0 else w
    return pl.pallas_call(
        _ada_kernel,
        grid=(nl, w // tn),
        in_specs=[pl.BlockSpec((r, d), lambda l, j: (0, 0)),
                  pl.BlockSpec((1, d, tn), lambda l, j: (l, 0, j)),
                  pl.BlockSpec((1, 1, tn), lambda l, j: (l, 0, j))],
        out_specs=pl.BlockSpec((1, r, tn), lambda l, j: (l, 0, j)),
        out_shape=jax.ShapeDtypeStruct((nl, r, w), F32),
        compiler_params=_params("parallel", "parallel"),
        name="ada_mod",
    )(c_all, ada_w, ada_b.reshape(nl, 1, w))


def _ffn_kernel(x_ref, sh_ref, sc_ref, gt_ref, wg_ref, wu_ref, wd_ref, g_ref, b_ref, o_ref):
    x = x_ref[...]
    h = (x * (1.0 + sc_ref[0]) + sh_ref[0]).astype(BF16)
    g = _dot(h, wg_ref[...])
    u = _dot(h, wu_ref[...])
    a = (g * _sigmoid(g) * u).astype(BF16)
    f = _dot(a, wd_ref[...])
    y = ALPHA * x + 0.5 * (1.0 + gt_ref[0]) * f
    o_ref[...] = _layer_norm(y, g_ref[...], b_ref[...])


def _ffn(x, mods, tpg, wg, wu, wd, ln_g, ln_b):
    n, d = x.shape
    f = wg.shape[1]
    tm = min(TOKEN_TILE, n)
    sh, sc, gt = mods
    return pl.pallas_call(
        _ffn_kernel,
        grid=(n // tm,),
        in_specs=[pl.BlockSpec((tm, d), lambda i: (i, 0)),
                  _mod_spec(sh, tpg), _mod_spec(sc, tpg), _mod_spec(gt, tpg),
                  _const_spec((d, f)), _const_spec((d, f)), _const_spec((f, d)),
                  _const_spec((1, d)), _const_spec((1, d))],
        out_specs=pl.BlockSpec((tm, d), lambda i: (i, 0)),
        out_shape=jax.ShapeDtypeStruct((n, d), F32),
        compiler_params=_params("parallel"),
        name="ffn_ln",
    )(x, sh.arr, sc.arr, gt.arr, wg, wu, wd, ln_g.reshape(1, d), ln_b.reshape(1, d))


def _inproj_kernel(x_ref, sh_ref, sc_ref, w_ref, *refs, widths, t_blocked, dils, n_dil):
    n_t = len(t_blocked)
    n_split = n_dil * len(dils)
    wt_refs, outs = refs[:n_t], refs[n_t:]
    o_refs = outs[:len(widths)]
    dil_refs = outs[len(widths):len(widths) + n_split]
    t_refs = outs[len(widths) + n_split:len(widths) + n_split + n_t]
    tm = x_ref.shape[0]
    h = (x_ref[...] * (1.0 + sc_ref[0]) + sh_ref[0]).astype(BF16)
    p = _dot(h, w_ref[...])
    off = 0
    for o_ref, wd in zip(o_refs, widths):
        o_ref[...] = p[:, off:off + wd].astype(o_ref.dtype)
        off += wd
    if n_split:
        stage = refs[-1]
        off = 0
        for s in range(n_dil):
            for c in range(widths[s] // LANES):
                stage[s, c] = p[:, off + c * LANES:off + (c + 1) * LANES]
            off += widths[s]
        for di, d in enumerate(dils):
            for s in range(n_dil):
                for r in range(d):
                    for c in range(widths[s] // LANES):
                        dst = dil_refs[di * n_dil + s]
                        dst[0, r, :, c * LANES:(c + 1) * LANES] = (
                            stage[s, c, pl.ds(r, tm // d, stride=d), :].astype(dst.dtype))
    for wt_ref, t_ref, blocked in zip(wt_refs, t_refs, t_blocked):
        vt = _dot_nt(wt_ref[...], h)
        if blocked:
            blk = t_ref.shape[2]
            for c in range(t_ref.shape[0]):
                t_ref[c] = vt[:, c * blk:(c + 1) * blk]
        else:
            t_ref[0] = vt


def _inproj(x, mods, tpg, w, widths, *, t_weights=(), t_block=None, seq_tiles=None, n_dil=0, dils=(),
            dil_dtype=F32):
    n, d = x.shape
    tm = min(TOKEN_TILE, n)
    sh, sc, _ = mods
    in_specs = [pl.BlockSpec((tm, d), lambda i: (i, 0)), _mod_spec(sh, tpg), _mod_spec(sc, tpg), _const_spec(w.shape)]
    out_specs = [pl.BlockSpec((tm, wd), lambda i: (i, 0)) for wd in widths]
    out_shape = [jax.ShapeDtypeStruct((n, wd), dil_dtype if s < n_dil else F32) for s, wd in enumerate(widths)]
    args = [x, sh.arr, sc.arr, w]
    nb = n // (seq_tiles * tm) if seq_tiles else None
    for dd in dils:
        for s in range(n_dil):
            out_specs.append(pl.BlockSpec((1, dd, tm // dd, widths[s]),
                                          lambda i: (i // seq_tiles, 0, i % seq_tiles, 0)))
            out_shape.append(jax.ShapeDtypeStruct((nb, dd, seq_tiles * tm // dd, widths[s]), dil_dtype))
    for wt, blocked in t_weights:
        in_specs.append(_const_spec(wt.shape))
        args.append(wt)
        if blocked:
            out_specs.append(pl.BlockSpec((tm // t_block, wt.shape[0], t_block), lambda i: (i, 0, 0)))
            out_shape.append(jax.ShapeDtypeStruct((n // t_block, wt.shape[0], t_block), F32))
        else:
            out_specs.append(pl.BlockSpec((1, wt.shape[0], tm), lambda i: (i // seq_tiles, 0, i % seq_tiles)))
            out_shape.append(jax.ShapeDtypeStruct((nb, wt.shape[0], seq_tiles * tm), F32))
    return pl.pallas_call(
        functools.partial(_inproj_kernel, widths=widths, t_blocked=tuple(bl for _, bl in t_weights),
                          dils=tuple(dils), n_dil=n_dil),
        grid=(n // tm,),
        in_specs=in_specs,
        out_specs=out_specs,
        out_shape=out_shape,
        scratch_shapes=[pltpu.VMEM((n_dil, widths[0] // LANES, tm, LANES), F32)] if n_dil and dils else [],
        compiler_params=_params("parallel"),
        name="in_proj",
    )(*args)


def _outproj_kernel(x_ref, gt_ref, *refs, n_branch, dil_of, post):
    n_att = len(dil_of)
    att_refs = refs[:n_att]
    n_post = 5 if post else 0
    y_ref = refs[n_att]
    post_refs = refs[n_att + 1:n_att + 1 + n_post]
    w_ref, g_ref, b_ref, o_ref = refs[n_att + 1 + n_post:n_att + 5 + n_post]
    scratch = list(refs[n_att + 5 + n_post:])
    tm = x_ref.shape[0]
    vals = []
    for ref, d in zip(att_refs, dil_of):
        if d == 1:
            vals.append(ref[...])
        else:
            buf = scratch.pop(0)
            for r in range(d):
                for c in range(buf.shape[0]):
                    buf[c, pl.ds(r, tm // d, stride=d), :] = ref[0, r, :, c * LANES:(c + 1) * LANES]
            vals.append(jnp.concatenate([buf[c] for c in range(buf.shape[0])], axis=-1))
    if n_branch > 1:
        outs = vals[:n_branch]
        lses = vals[n_branch:]
        m = functools.reduce(jnp.maximum, lses)
        ws = [jnp.exp(l - m) for l in lses]
        den = functools.reduce(lambda a, b: a + b, ws)
        att = functools.reduce(lambda a, b: a + b, [w * o for w, o in zip(ws, outs)]) / den
    else:
        att = vals[0]
    half = att.shape[1]
    yv = y_ref[...]
    if post:
        yv = _rwkv_post_value(yv, *[r[...] for r in post_refs])
    mix = _dot(att.astype(BF16), w_ref[:half, :]) + _dot(yv.astype(BF16), w_ref[half:, :])
    y = ALPHA * x_ref[...] + (1.0 + gt_ref[0]) * mix
    o_ref[...] = _layer_norm(y, g_ref[...], b_ref[...])


def _outproj(x, gate, tpg, atts, y, w, ln_g, ln_b, dil_of=None, seq_tiles=None, post=None):
    n, d = x.shape
    tm = min(TOKEN_TILE, n)
    half = y.shape[1]
    dil_of = tuple(dil_of) if dil_of else (1,) * len(atts)
    n_branch = len(atts) // 2 if len(atts) > 1 else 1
    tok = lambda wd: pl.BlockSpec((tm, wd), lambda i: (i, 0))
    split = lambda dd: pl.BlockSpec((1, dd, tm // dd, half), lambda i: (i // seq_tiles, 0, i % seq_tiles, 0))
    post_specs, post_args = [], []
    if post is not None:
        bonus, pgate, gn_w, gn_b, seg = post
        post_specs = [tok(half), tok(half), _const_spec((1, half)), _const_spec((1, half)), _const_spec(seg.shape)]
        post_args = [bonus, pgate, gn_w.reshape(1, half), gn_b.reshape(1, half), seg]
    return pl.pallas_call(
        functools.partial(_outproj_kernel, n_branch=n_branch, dil_of=dil_of, post=post is not None),
        grid=(n // tm,),
        in_specs=[tok(d), _mod_spec(gate, tpg)] + [tok(half) if dd == 1 else split(dd) for dd in dil_of]
        + [tok(half)] + post_specs + [_const_spec(w.shape), _const_spec((1, d)), _const_spec((1, d))],
        out_specs=tok(d),
        out_shape=jax.ShapeDtypeStruct((n, d), F32),
        scratch_shapes=[pltpu.VMEM((half // LANES, tm, LANES), F32) for dd in dil_of if dd > 1],
        compiler_params=_params("parallel"),
        name="out_proj_ln",
    )(x, gate.arr, *atts, y, *post_args, w, ln_g.reshape(1, d), ln_b.reshape(1, d))


def _diff_lambda(dl_ref, lam_init):
    lp = dl_ref[...]
    a = jnp.sum(lp[0:1] * lp[1:2], axis=-1, keepdims=True)
    b = jnp.sum(lp[2:3] * lp[3:4], axis=-1, keepdims=True)
    return jnp.exp(a) - jnp.exp(b) + lam_init


def _head_rms(o, g, lam_init):
    return o * lax.rsqrt(jnp.mean(o * o, -1, keepdims=True) + LN_EPS) * g * (1.0 - lam_init)


def _diff_prompt_kernel(dl_ref, q_ref, k_ref, vt_ref, g_ref, o_ref, *, tq, tk, lam_init):
    i = pl.program_id(1)
    lam = _diff_lambda(dl_ref, lam_init)
    hw = 2 * DH_A
    n_heads = q_ref.shape[1] // hw
    nm = 2 * n_heads
    lo = lax.broadcasted_iota(jnp.int32, (1, hw), 1) < DH_A
    qs = []
    for h in range(n_heads):
        q = q_ref[:, h * hw:(h + 1) * hw] * (DH_A ** -0.5 * math.log2(math.e))
        qs += [jnp.where(lo, q, 0.0).astype(BF16), jnp.where(lo, 0.0, q).astype(BF16)]
    head = lambda x: slice((x // 2) * hw, (x // 2 + 1) * hw)

    def block(j, carry, masked):
        ms, ls, accs = carry[:nm], carry[nm:2 * nm], carry[2 * nm:]
        kb = k_ref[pl.ds(pl.multiple_of(j * tk, tk), tk), :].astype(BF16)
        vt = vt_ref[j].astype(BF16)
        ss = [_dot_nt(kb[:, head(x)], qs[x]) for x in range(nm)]
        if masked:
            kpos = j * tk + lax.broadcasted_iota(jnp.int32, (tk, 1), 0)
            qpos = i * tq + lax.broadcasted_iota(jnp.int32, (1, tq), 1)
            ok = kpos <= qpos
            ss = [jnp.where(ok, s, NEG) for s in ss]
        ns = [jnp.maximum(m, jnp.max(s, 0, keepdims=True)) for m, s in zip(ms, ss)]
        ps = [jnp.exp2(s - n) for s, n in zip(ss, ns)]
        cs = [jnp.exp2(m - n) for m, n in zip(ms, ns)]
        ls = [c * l + jnp.sum(p, 0, keepdims=True) for c, l, p in zip(cs, ls, ps)]
        accs = [cs[x] * accs[x] + _dot(vt[head(x), :], ps[x].astype(BF16)) for x in range(nm)]
        return tuple(ns) + tuple(ls) + tuple(accs)

    init = ((jnp.full((1, tq), NEG, F32),) * nm + (jnp.zeros((1, tq), F32),) * nm
            + (jnp.zeros((hw, tq), F32),) * nm)
    n_full = (i * tq) // tk
    c = lax.fori_loop(0, n_full, lambda j, c: block(j, c, False), init)
    fin = block(n_full, c, True)
    ls, accs = fin[nm:2 * nm], fin[2 * nm:]
    outs = []
    for h in range(n_heads):
        o = (accs[2 * h] / ls[2 * h] - lam * (accs[2 * h + 1] / ls[2 * h + 1])).T
        outs.append(_head_rms(o, g_ref[...], lam_init))
    o_ref[...] = jnp.concatenate(outs, axis=-1)


def _diff_prompt(q, k, vt, dl, subln, b, t, lam_init):
    n, w = q.shape
    hw = 2 * DH_A
    tk = vt.shape[2]
    tq = min(ATTN_Q_TILE, tk)
    assert tk % tq == 0 and t % tk == 0
    nq = t // tq
    return pl.pallas_call(
        functools.partial(_diff_prompt_kernel, tq=tq, tk=tk, lam_init=lam_init),
        grid=(b, nq),
        in_specs=[pl.BlockSpec(dl.shape, lambda bi, i: (0, 0)),
                  pl.BlockSpec((tq, w), lambda bi, i: (bi * nq + i, 0)),
                  pl.BlockSpec((t, w), lambda bi, i: (bi, 0)),
                  pl.BlockSpec((t // tk, w, tk), lambda bi, i: (bi, 0, 0)),
                  pl.BlockSpec((1, hw), lambda bi, i: (0, 0))],
        out_specs=pl.BlockSpec((tq, w), lambda bi, i: (bi * nq + i, 0)),
        out_shape=jax.ShapeDtypeStruct((n, w), F32),
        compiler_params=_params("parallel", "arbitrary"),
        name="diff_attn_prompt",
    )(dl, q, k, vt, subln.reshape(1, hw))


def _block_diag_queries(q8, groups, group_width):
    rows = groups * SUBLANES
    row_g = lax.broadcasted_iota(jnp.int32, (rows, 1), 0) // SUBLANES
    lane_g = lax.broadcasted_iota(jnp.int32, (1, q8.shape[1]), 1) // group_width
    return jnp.where(lane_g == row_g, jnp.concatenate([q8] * groups, axis=0), 0.0).astype(BF16)


def _diff_sample_kernel(pt_ref, dl_ref, q_ref, kn_ref, vn_ref, g_ref, *refs, n_pages, lam_init):
    del pt_ref
    kt_refs, v_refs, o_ref = refs[:n_pages], refs[n_pages:2 * n_pages], refs[2 * n_pages]
    width = q_ref.shape[2]
    groups = width // DH_A
    hw = 2 * DH_A
    qbd = _block_diag_queries(q_ref[0] * (DH_A ** -0.5), groups, DH_A)
    row_t = lax.broadcasted_iota(jnp.int32, (groups * SUBLANES, 1), 0) % SUBLANES
    col = lax.broadcasted_iota(jnp.int32, (1, SUBLANES), 1)
    s_pages = [_dot(qbd, kt[0].astype(BF16)) for kt in kt_refs]
    s_new = jnp.where(col <= row_t, _dot_nt(qbd, kn_ref[0].astype(BF16)), NEG)
    m = jnp.max(s_new, -1, keepdims=True)
    for s in s_pages:
        m = jnp.maximum(m, jnp.max(s, -1, keepdims=True))
    p_new = jnp.exp(s_new - m)
    den = jnp.sum(p_new, -1, keepdims=True)
    p_pages = []
    for s in s_pages:
        pr = jnp.exp(s - m)
        den = den + jnp.sum(pr, -1, keepdims=True)
        p_pages.append(pr.astype(BF16))
    p_new = p_new.astype(BF16)
    lam = _diff_lambda(dl_ref, lam_init)
    vn = vn_ref[0].astype(BF16)
    n_heads = width // hw
    page = v_refs[0].shape[1]
    spread = jnp.where(lax.broadcasted_iota(jnp.int32, (page, page * n_heads), 1) // n_heads
                       == lax.broadcasted_iota(jnp.int32, (page, page * n_heads), 0), 1.0, 0.0).astype(BF16)
    row_h = lax.broadcasted_iota(jnp.int32, (groups * SUBLANES, 1), 0) // (2 * SUBLANES)
    own = lax.broadcasted_iota(jnp.int32, (1, page * n_heads), 1) % n_heads == row_h
    p_wide = [jnp.where(own, _dot(pr, spread), 0.0).astype(BF16) for pr in p_pages]
    v_rows = [v_ref[0].reshape(page * n_heads, hw).astype(BF16) for v_ref in v_refs]
    acc = jnp.concatenate([_dot(p_new[2 * h * SUBLANES:2 * (h + 1) * SUBLANES], vn[:, h * hw:(h + 1) * hw])
                           for h in range(n_heads)], axis=0)
    for pw_, vr in zip(p_wide, v_rows):
        acc = acc + _dot(pw_, vr)
    a = acc / den
    outs = []
    for h in range(n_heads):
        r0 = 2 * h * SUBLANES
        outs.append(_head_rms(a[r0:r0 + SUBLANES] - lam * a[r0 + SUBLANES:r0 + 2 * SUBLANES], g_ref[...], lam_init))
    o_ref[0] = jnp.concatenate(outs, axis=-1)


def _diff_sample(q8, kn8, vn8, pool_kt, pool_v, page_table, dl, subln, lam_init):
    bs, _, w = q8.shape
    n_pages = page_table.shape[1]
    tok = pl.BlockSpec((1, SUBLANES, w), lambda b, pt: (b, 0, 0))
    kspecs = [pl.BlockSpec((1,) + pool_kt.shape[1:], lambda b, pt, j=j: (pt[b, j], 0, 0)) for j in range(n_pages)]
    vspecs = [pl.BlockSpec((1,) + pool_v.shape[1:], lambda b, pt, j=j: (pt[b, j], 0, 0, 0)) for j in range(n_pages)]
    return pl.pallas_call(
        functools.partial(_diff_sample_kernel, n_pages=n_pages, lam_init=lam_init),
        grid_spec=pltpu.PrefetchScalarGridSpec(
            num_scalar_prefetch=1,
            grid=(bs,),
            in_specs=[pl.BlockSpec(dl.shape, lambda b, pt: (0, 0)), tok, tok, tok,
                      pl.BlockSpec((1, 2 * DH_A), lambda b, pt: (0, 0))] + kspecs + vspecs,
            out_specs=tok),
        out_shape=jax.ShapeDtypeStruct((bs, SUBLANES, w), F32),
        compiler_params=_params("parallel"),
        name="diff_attn_sample",
    )(page_table, dl, q8, kn8, vn8, subln.reshape(1, 2 * DH_A), *([pool_kt] * n_pages), *([pool_v] * n_pages))


def _s5_prep_kernel(are_ref, aim_ref, ldt_ref, bre_ref, bim_ref, lr_ref, li_ref, bbr_ref, bbi_ref):
    a_re, a_im = are_ref[...], aim_ref[...]
    dt = jnp.exp(ldt_ref[...])
    mag = jnp.exp(a_re * dt)
    lam_re, lam_im = mag * jnp.cos(a_im * dt), mag * jnp.sin(a_im * dt)
    den = a_re * a_re + a_im * a_im
    nr = lam_re - 1.0
    f_re = (nr * a_re + lam_im * a_im) / den
    f_im = (lam_im * a_re - nr * a_im) / den
    lr_ref[...] = lam_re
    li_ref[...] = lam_im
    for g in range(a_re.shape[0]):
        fr, fi = f_re[g:g + 1, :], f_im[g:g + 1, :]
        br, bi = bre_ref[g], bim_ref[g]
        bbr_ref[g] = fr * br - fi * bi
        bbi_ref[g] = fr * bi + fi * br


def _s5_prep(a_re, a_im, log_dt, b_re, b_im):
    g, p = a_re.shape
    c = b_re.shape[-1]
    bt = lambda b: jnp.transpose(b, (0, 2, 1))
    sd = jax.ShapeDtypeStruct
    return pl.pallas_call(
        _s5_prep_kernel,
        out_shape=[sd((g, p), F32), sd((g, p), F32), sd((g, c, p), F32), sd((g, c, p), F32)],
        name="s5_prep",
    )(a_re, a_im, log_dt.reshape(g, 1), bt(b_re), bt(b_im))


S5_SPLIT = 2


def _dot_block_diag(x, w_ref):
    kx, kw = x.shape[1] // S5_SPLIT, w_ref.shape[1] // S5_SPLIT
    return jnp.concatenate([_dot(x[:, i * kx:(i + 1) * kx], w_ref[i * kx:(i + 1) * kx, i * kw:(i + 1) * kw])
                            for i in range(S5_SPLIT)], axis=-1)


def _s5_tail(u, hr, hi, ccr, cci, d, gw, gb):
    y = _dot_block_diag(hr.astype(BF16), ccr) - _dot_block_diag(hi.astype(BF16), cci) + d * u
    z = jax.nn.gelu(y)
    return z * _sigmoid(_dot(z.astype(BF16), gw) + gb)


def _s5_prompt_kernel(u_ref, bbr_ref, bbi_ref, lr_ref, li_ref, h0r_ref, h0i_ref, ccr_ref, cci_ref,
                      d_ref, gw_ref, gb_ref, y_ref, hr_out, hi_out, xr, xi, cr, ci):
    i = pl.program_id(1)
    tc = u_ref.shape[0]

    @pl.when(i == 0)
    def _():
        cr[...] = h0r_ref[0]
        ci[...] = h0i_ref[0]

    u = u_ref[...]
    ub = u.astype(BF16)
    xr[...] = _dot_block_diag(ub, bbr_ref)
    xi[...] = _dot_block_diag(ub, bbi_ref)
    lr, li = lr_ref[...], li_ref[...]

    def step(t, carry):
        hr, hi = carry
        nhr = lr * hr - li * hi + xr[pl.ds(t, 1), :]
        nhi = lr * hi + li * hr + xi[pl.ds(t, 1), :]
        xr[pl.ds(t, 1), :] = nhr
        xi[pl.ds(t, 1), :] = nhi
        return nhr, nhi

    hr, hi = lax.fori_loop(0, tc, step, (cr[...], ci[...]), unroll=8)
    cr[...] = hr
    ci[...] = hi
    y_ref[...] = _s5_tail(u, xr[...], xi[...], ccr_ref[...], cci_ref[...], d_ref[...], gw_ref[...], gb_ref[...])

    @pl.when(i == pl.num_programs(1) - 1)
    def _():
        hr_out[0] = hr
        hi_out[0] = hi


def _s5_prompt(u, b, t, h0r, h0i, prm):
    bbr, bbi, lr, li, ccr, cci, d, gw, gb = prm
    n, ch = u.shape
    ns = lr.shape[1]
    tc = min(512, t)
    nt = t // tc
    st = pl.BlockSpec((1, 1, ns), lambda bi, i: (bi, 0, 0))
    sd = jax.ShapeDtypeStruct
    return pl.pallas_call(
        _s5_prompt_kernel,
        grid=(b, nt),
        in_specs=[pl.BlockSpec((tc, ch), lambda bi, i: (bi * nt + i, 0)),
                  _const_spec(bbr.shape), _const_spec(bbi.shape), _const_spec(lr.shape), _const_spec(li.shape),
                  st, st, _const_spec(ccr.shape), _const_spec(cci.shape), _const_spec(d.shape),
                  _const_spec(gw.shape), _const_spec(gb.shape)],
        out_specs=[pl.BlockSpec((tc, ch), lambda bi, i: (bi * nt + i, 0)), st, st],
        out_shape=[sd((n, ch), F32), sd((b, 1, ns), F32), sd((b, 1, ns), F32)],
        scratch_shapes=[pltpu.VMEM((tc, ns), F32), pltpu.VMEM((tc, ns), F32),
                        pltpu.VMEM((1, ns), F32), pltpu.VMEM((1, ns), F32)],
        compiler_params=_params("parallel", "arbitrary"),
        name="s5_prompt",
    )(u, bbr, bbi, lr, li, h0r, h0i, ccr, cci, d, gw, gb)


def _s5_sample_kernel(u_ref, bbr_ref, bbi_ref, lr_ref, li_ref, h0r_ref, h0i_ref, ccr_ref, cci_ref,
                      d_ref, gw_ref, gb_ref, y_ref, hr_out, hi_out):
    lr, li = lr_ref[...], li_ref[...]
    hr, hi = h0r_ref[...], h0i_ref[...]
    for t in range(u_ref.shape[0]):
        u = u_ref[t]
        ub = u.astype(BF16)
        xr = _dot_block_diag(ub, bbr_ref)
        xi = _dot_block_diag(ub, bbi_ref)
        hr, hi = lr * hr - li * hi + xr, lr * hi + li * hr + xi
        y_ref[t] = _s5_tail(u, hr, hi, ccr_ref[...], cci_ref[...], d_ref[...], gw_ref[...], gb_ref[...])
    hr_out[...] = hr
    hi_out[...] = hi


def _s5_sample(u_tm, h0r, h0i, prm):
    bbr, bbi, lr, li, ccr, cci, d, gw, gb = prm
    sd = jax.ShapeDtypeStruct
    return pl.pallas_call(
        _s5_sample_kernel,
        out_shape=[sd(u_tm.shape, F32), sd(h0r.shape, F32), sd(h0i.shape, F32)],
        compiler_params=pltpu.CompilerParams(vmem_limit_bytes=VMEM_LIMIT_BYTES),
        name="s5_sample",
    )(u_tm, bbr, bbi, lr, li, h0r, h0i, ccr, cci, d, gw, gb)


def _dil_prompt_kernel(q_ref, kp_ref, kc_ref, vp_ref, vc_ref, o_ref, l_ref):
    n = pl.program_id(1)
    blk = C_BLK
    nsub = q_ref.shape[1] // blk
    w = q_ref.shape[2]
    pw = 2 * DH_C
    kall = jnp.concatenate([kp_ref[0], kc_ref[0]], axis=0).astype(BF16)
    vall = jnp.concatenate([vp_ref[0], vc_ref[0]], axis=0).astype(BF16)
    qi = lax.broadcasted_iota(jnp.int32, (blk, 1), 0) + blk
    ki = lax.broadcasted_iota(jnp.int32, (1, 2 * blk), 1)
    dist = qi - ki
    band = (dist >= 0) & (dist <= blk)
    lo_k = jnp.where(n > 0, 0, blk)
    oks = [band & (ki >= lo_k)] + [band] * (nsub - 1)
    oks = [jnp.concatenate([ok, ok], axis=0) for ok in oks]
    first = lax.broadcasted_iota(jnp.int32, (1, pw), 1) < DH_C
    units = [(j, slice(p * pw, (p + 1) * pw)) for j in range(nsub) for p in range(w // pw)]
    rows = lambda j: slice(j * blk, (j + 2) * blk)
    qbd = []
    for j, sl in units:
        q = q_ref[0, j * blk:(j + 1) * blk, sl] * (DH_C ** -0.5)
        qbd.append(jnp.concatenate([jnp.where(first, q, 0.0), jnp.where(first, 0.0, q)], axis=0).astype(BF16))
    s = [jnp.where(oks[j], _dot_nt(qb, kall[rows(j), sl]), NEG) for qb, (j, sl) in zip(qbd, units)]
    m = [jnp.max(x, -1, keepdims=True) for x in s]
    pr = [jnp.exp(x - mm) for x, mm in zip(s, m)]
    den = [jnp.sum(x, -1, keepdims=True) for x in pr]
    oh = [_dot((x / d).astype(BF16), vall[rows(j), sl]) for x, d, (j, sl) in zip(pr, den, units)]
    lh = [mm + jnp.log(d) for mm, d in zip(m, den)]
    npair = w // pw
    for j in range(nsub):
        o_ref[0, j * blk:(j + 1) * blk, :] = jnp.concatenate(
            [jnp.where(first, x[:blk], x[blk:]) for x in oh[j * npair:(j + 1) * npair]], axis=-1)
        l_ref[0, j * blk:(j + 1) * blk, :] = jnp.concatenate(
            [jnp.where(first, x[:blk], x[blk:]) for x in lh[j * npair:(j + 1) * npair]], axis=-1)


def _dil_prompt(qd, kd, vd):
    bd, ns, w = qd.shape
    nsub = 2 if (ns // C_BLK) % 2 == 0 else 1
    cur = pl.BlockSpec((1, nsub * C_BLK, w), lambda b, n: (b, n, 0))
    prev = pl.BlockSpec((1, C_BLK, w), lambda b, n: (b, jnp.maximum(n * nsub - 1, 0), 0))
    sd = jax.ShapeDtypeStruct((bd, ns, w), F32)
    return pl.pallas_call(
        _dil_prompt_kernel,
        grid=(bd, ns // (nsub * C_BLK)),
        in_specs=[cur, prev, cur, prev, cur],
        out_specs=[cur, cur],
        out_shape=[sd, sd],
        compiler_params=_params("parallel", "parallel"),
        name="dilated_attn_prompt",
    )(qd, kd, kd, vd, vd)


def _dil_sample_kernel(q_ref, kn_ref, vn_ref, kt_ref, vt_ref, o_ref):
    w = q_ref.shape[2]
    nh = w // DH_C
    buf = kt_ref.shape[2]
    rows = nh * SUBLANES
    qbd = _block_diag_queries(q_ref[0] * (DH_C ** -0.5), nh, DH_C)
    row_q = lax.broadcasted_iota(jnp.int32, (rows, 1), 0) % SUBLANES

    def reach(delta):
        mult = jnp.zeros(delta.shape, F32)
        for d in C_DILATIONS:
            ok = (delta >= 0) & (lax.rem(delta, d) == 0) & (delta <= d * C_BLK)
            mult = mult + jnp.where(ok, 1.0, 0.0)
        return mult

    mult = reach(buf + row_q - lax.broadcasted_iota(jnp.int32, (1, buf), 1))
    mult_n = reach(row_q - lax.broadcasted_iota(jnp.int32, (1, SUBLANES), 1))
    s = jnp.where(mult > 0, _dot(qbd, kt_ref[0].astype(BF16)), NEG)
    s_n = jnp.where(mult_n > 0, _dot_nt(qbd, kn_ref[0].astype(BF16)), NEG)
    m = jnp.maximum(jnp.max(s, -1, keepdims=True), jnp.max(s_n, -1, keepdims=True))
    p = mult * jnp.exp(s - m)
    p_n = mult_n * jnp.exp(s_n - m)
    inv = 1.0 / (jnp.sum(p, -1, keepdims=True) + jnp.sum(p_n, -1, keepdims=True))
    o = _dot_nt((p * inv).astype(BF16), vt_ref[0].astype(BF16)) + _dot((p_n * inv).astype(BF16), vn_ref[0].astype(BF16))
    lane_h = lax.broadcasted_iota(jnp.int32, (1, w), 1) // DH_C
    out = jnp.zeros((SUBLANES, w), F32)
    for h in range(nh):
        out = out + jnp.where(lane_h == h, o[h * SUBLANES:(h + 1) * SUBLANES, :], 0.0)
    o_ref[0] = out


def _dil_sample(q8, kn8, vn8, cache_kt, cache_vt):
    bs, w, buf = cache_kt.shape
    tok = pl.BlockSpec((1, SUBLANES, w), lambda b: (b, 0, 0))
    cache = pl.BlockSpec((1, w, buf), lambda b: (b, 0, 0))
    return pl.pallas_call(
        _dil_sample_kernel,
        grid=(bs,),
        in_specs=[tok, tok, tok, cache, cache],
        out_specs=tok,
        out_shape=jax.ShapeDtypeStruct((bs, SUBLANES, w), F32),
        compiler_params=_params("parallel"),
        name="dilated_attn_sample",
    )(q8, kn8, vn8, cache_kt, cache_vt)


def _softplus(x):
    return jnp.maximum(x, 0.0) + jnp.log1p(jnp.exp(-jnp.abs(x)))


def _rwkv_pre_kernel(pd_ref, pv_ref, mu_ref, w0_ref, w2_ref, a0_ref, a2_ref, g2_ref, kk_ref, ka_ref, rk_ref, seg_ref,
                     r_o, ld_o, k_o, v_o, kk_o, b_o, g_o, bonus_o, *, seq_tiles):
    pd = pd_ref[...]
    if seq_tiles is None:
        prev = pv_ref[...]
    else:
        first = pl.program_id(0) % seq_tiles == 0
        before = jnp.where(first, 0.0, pv_ref[SUBLANES - 1:SUBLANES, :])
        row = lax.broadcasted_iota(jnp.int32, (pd.shape[0], 1), 0)
        prev = jnp.where(row == 0, before, pltpu.roll(pd, 1, 0))
    xm = pd + (prev - pd) * mu_ref[...]
    o1, o2, o3 = D_W, 2 * D_W, 3 * D_W
    o5 = o3 + D_LORA_W + D_LORA_A
    r, k, v = xm[:, :o1], xm[:, o1:o2], xm[:, o2:o3]
    wa, gl = xm[:, o3:o5], xm[:, o5:]
    lw = _dot(jnp.tanh(wa).astype(BF16), w2_ref[...])
    la = _dot(wa.astype(BF16), a2_ref[...])
    g = _dot(_sigmoid(gl).astype(BF16), g2_ref[...])
    w_log = -_softplus(-(w0_ref[...] + lw)) - 0.5
    a = _sigmoid(a0_ref[...] + la)
    seg = seg_ref[...]
    kk = k * kk_ref[...]
    kk = kk / jnp.maximum(jnp.sqrt(_dot_exact_rhs(kk * kk, seg)), 1e-12)
    k2 = k * (1.0 + (a - 1.0) * ka_ref[...])
    r_o[...] = r
    ld_o[...] = -jnp.exp(w_log)
    k_o[...] = k2
    v_o[...] = v
    kk_o[...] = kk
    b_o[...] = kk * a
    g_o[...] = g
    bonus_o[...] = _dot_exact_rhs(r * k2 * rk_ref[...], seg) * v


def _rwkv_pre(pd, prev, prm, seq_tiles=None):
    n, cols = pd.shape
    tm = min(TOKEN_TILE, n)
    tok = lambda wd: pl.BlockSpec((tm, wd), lambda i: (i, 0))
    if prev is None:
        prev = pd
        prev_spec = pl.BlockSpec((SUBLANES, cols), lambda i: (jnp.maximum(i * (tm // SUBLANES) - 1, 0), 0))
    else:
        prev_spec = tok(cols)
    return pl.pallas_call(
        functools.partial(_rwkv_pre_kernel, seq_tiles=seq_tiles),
        grid=(n // tm,),
        in_specs=[tok(cols), prev_spec] + [_const_spec(p.shape) for p in prm],
        out_specs=[tok(D_W)] * 8,
        out_shape=[jax.ShapeDtypeStruct((n, D_W), F32)] * 8,
        compiler_params=_params("parallel"),
        name="rwkv_pre",
    )(pd, prev, *prm)


def _rwkv_chunk_kernel(r_ref, ld_ref, k_ref, v_ref, kk_ref, b_ref, y_ref, s_out, st):
    c = pl.program_id(1)
    nb, ch, w = r_ref.shape
    pw = 2 * N_D
    wide = lambda ref: jnp.concatenate([ref[i] for i in range(nb)], axis=-1)

    @pl.when(c == 0)
    def _():
        st[...] = jnp.zeros(st.shape, F32)

    ri = lax.broadcasted_iota(jnp.int32, (ch, ch), 0)
    ci = lax.broadcasted_iota(jnp.int32, (ch, ch), 1)
    tri_incl = ci <= ri
    tri_strict = ci < ri
    eye_c = jnp.where(ci == ri, 1.0, 0.0)
    ld = wide(ld_ref)
    r_w, k_w, b_w = wide(r_ref), wide(k_ref), wide(b_ref)
    cum = _dot_exact_rhs_left(jnp.where(tri_incl, 1.0, 0.0).astype(BF16), ld)
    cum_end = cum[ch - 1:ch, :]
    g_inc = jnp.exp(cum)
    g_inv = jnp.exp(-cum)
    g_end = jnp.exp(cum_end - cum)
    rho = r_w * g_inc
    kap = wide(kk_ref) * jnp.exp(cum - ld)
    kh = k_w * g_inv
    bh = b_w * g_inv
    khg = k_w * g_end
    bhg = b_w * g_end
    gam_end = jnp.exp(cum_end)
    v = wide(v_ref)

    pi = lax.broadcasted_iota(jnp.int32, (pw, pw), 0)
    pj = lax.broadcasted_iota(jnp.int32, (pw, pw), 1)
    same_head = (pi // N_D) == (pj // N_D)
    lane = lax.broadcasted_iota(jnp.int32, (1, pw), 1)
    first = lane < N_D

    npair = nb * w // pw
    heads = [(pr, hh) for pr in range(npair) for hh in range(2)]
    sls = [slice(pr * pw, (pr + 1) * pw) for pr in range(npair)]
    sel = (first, jnp.logical_not(first))
    bf = lambda x: x.astype(BF16)
    kap_p = [kap[:, s] for s in sls]
    rho_p = [rho[:, s] for s in sls]
    kap_b = [bf(x) for x in kap_p]
    kh_b = [bf(kh[:, s]) for s in sls]
    bh_b = [bf(bh[:, s]) for s in sls]
    v_b = [bf(v[:, s]) for s in sls]
    kap_m = [bf(jnp.where(sel[hh], kap_p[pr], 0.0)) for pr, hh in heads]
    rho_m = [bf(jnp.where(sel[hh], rho_p[pr], 0.0)) for pr, hh in heads]
    a_b = [jnp.where(tri_strict, _dot_nt(kap_m[i], bh_b[pr]), 0.0) for i, (pr, _) in enumerate(heads)]
    a_k = [bf(jnp.where(tri_strict, _dot_nt(kap_m[i], kh_b[pr]), 0.0)) for i, (pr, _) in enumerate(heads)]
    ap_b = [bf(jnp.where(tri_incl, _dot_nt(rho_m[i], bh_b[pr]), 0.0)) for i, (pr, _) in enumerate(heads)]
    ap_k = [bf(jnp.where(tri_incl, _dot_nt(rho_m[i], kh_b[pr]), 0.0)) for i, (pr, _) in enumerate(heads)]
    tl = [eye_c - a for a in a_b]
    pw2 = [_dot(bf(a), bf(a)) for a in a_b]
    span = 2
    while span < ch:
        tl = [t + _dot(bf(t), bf(p2)) for t, p2 in zip(tl, pw2)]
        span *= 2
        if span < ch:
            pw2 = [_dot(bf(p2), bf(p2)) for p2 in pw2]
    tl_b = [bf(t) for t in tl]
    kap2_h = [_dot(tl_b[i], kap_b[pr]) for i, (pr, _) in enumerate(heads)]
    akv = [bf(_dot(a_k[i], v_b[pr])) for i, (pr, _) in enumerate(heads)]
    wr_h = [_dot(tl_b[i], akv[i]) for i in range(len(heads))]
    rho2_h = [rho_p[pr] - _dot(ap_b[i], bf(kap2_h[i])) for i, (pr, _) in enumerate(heads)]
    yloc_h = [_dot(ap_k[i], v_b[pr]) - _dot(ap_b[i], bf(wr_h[i])) for i, (pr, _) in enumerate(heads)]
    pair = lambda xs: [jnp.where(first, xs[2 * pr], xs[2 * pr + 1]) for pr in range(npair)]
    kap2, wr, rho2, yloc = pair(kap2_h), pair(wr_h), pair(rho2_h), pair(yloc_h)
    bhg_b = [bf(bhg[:, s]) for s in sls]
    khg_b = [bf(khg[:, s]) for s in sls]
    diag = [jnp.where(pi == pj, jnp.broadcast_to(gam_end[:, s], (pw, pw)), 0.0) for s in sls]
    phi = [jnp.where(same_head, diag[pr] - _dot_tn(bhg_b[pr], bf(kap2[pr])), 0.0) for pr in range(npair)]
    gmat = [jnp.where(same_head, _dot_tn(khg_b[pr], v_b[pr]) - _dot_tn(bhg_b[pr], bf(wr[pr])), 0.0)
            for pr in range(npair)]
    s_b = [bf(st[pr]) for pr in range(npair)]
    per_b = w // pw
    for pr in range(npair):
        y_ref[pr // per_b, :, (pr % per_b) * pw:(pr % per_b + 1) * pw] = _dot(bf(rho2[pr]), s_b[pr]) + yloc[pr]
    for pr in range(npair):
        st[pr] = _dot(bf(phi[pr]), s_b[pr]) + gmat[pr]

    @pl.when(c == pl.num_programs(1) - 1)
    def _():
        for pr in range(npair):
            s_out[pr // per_b, pr % per_b] = st[pr]


def _dot_exact_rhs_left(m01, x):
    hi, mid, lo = _split3(x)
    return _dot(m01, hi) + _dot(m01, mid) + _dot(m01, lo)


def _rwkv_chunk(r, ld, k2, v, kk, bb, b, t):
    n, w = r.shape
    ch = min(RWKV_CHUNK, t)
    nc = t // ch
    npair = w // (2 * N_D)
    nb = next(d for d in (4, 2, 1) if b % d == 0)
    tok = pl.BlockSpec((nb, ch, w), lambda bi, c: (bi, c, 0))
    seq = lambda z: z.reshape(b, t, w)
    y, s_pairs = pl.pallas_call(
        _rwkv_chunk_kernel,
        grid=(b // nb, nc),
        in_specs=[tok] * 6,
        out_specs=[tok, pl.BlockSpec((nb, npair, 2 * N_D, 2 * N_D), lambda bi, c: (bi, 0, 0, 0))],
        out_shape=[jax.ShapeDtypeStruct((b, t, w), F32), jax.ShapeDtypeStruct((b, npair, 2 * N_D, 2 * N_D), F32)],
        scratch_shapes=[pltpu.VMEM((nb * npair, 2 * N_D, 2 * N_D), F32)],
        compiler_params=_params("parallel", "arbitrary"),
        name="rwkv_chunk_scan",
    )(seq(r), seq(ld), seq(k2), seq(v), seq(kk), seq(bb))
    return y.reshape(n, w), s_pairs


def _rwkv_lane_kernel(r_ref, ld_ref, k_ref, v_ref, kk_ref, b_ref, s_ref, y_ref, s_out):
    steps = r_ref.shape[0]

    def body(vi, carry):
        s = s_ref[0, vi]
        for t in range(steps):
            sk = jnp.sum(s * kk_ref[t, 0], axis=0, keepdims=True)
            vv = v_ref[t, 0, pl.ds(vi, 1), :]
            s = s * jnp.exp(ld_ref[t, 0]) - sk * b_ref[t, 0] + vv * k_ref[t, 0]
            y_ref[t, 0, pl.ds(vi, 1), :] = jnp.sum(s * r_ref[t, 0], axis=0, keepdims=True)
        s_out[0, vi] = s
        return carry

    lax.fori_loop(0, s_ref.shape[1], body, 0)


def _rwkv_lane(rt, ldt, kt, vt, kkt, bt, s0):
    steps, nh, nd, bs = rt.shape
    tok = pl.BlockSpec((steps, 1, nd, bs), lambda h: (0, h, 0, 0))
    stt = pl.BlockSpec((1, nd, nd, bs), lambda h: (h, 0, 0, 0))
    return pl.pallas_call(
        _rwkv_lane_kernel,
        grid=(nh,),
        in_specs=[tok] * 6 + [stt],
        out_specs=[tok, stt],
        out_shape=[jax.ShapeDtypeStruct(rt.shape, F32), jax.ShapeDtypeStruct(s0.shape, F32)],
        compiler_params=_params("parallel"),
        name="rwkv_lane_scan",
    )(rt, ldt, kt, vt, kkt, bt, s0)


def _rwkv_post_value(y, bonus, g, gw, gb, seg):
    mu = _dot_exact_rhs(y, seg) * (1.0 / N_D)
    yc = y - mu
    var = _dot_exact_rhs(yc * yc, seg) * (1.0 / N_D)
    yn = yc * lax.rsqrt(var + GN_EPS) * gw + gb
    return (yn + bonus) * g


def _pad_tokens(x, bs, s_len):
    x = x.reshape(bs, s_len, x.shape[-1])
    return jnp.pad(x, ((0, 0), (0, SUBLANES - s_len), (0, 0)))


def _block_diag_in(bb):
    g, c, p = bb.shape
    return jnp.einsum('gcp,gh->gchp', bb, jnp.eye(g, dtype=bb.dtype)).reshape(g * c, g * p)


def _block_diag_out(cc):
    g, c, p = cc.shape
    return jnp.einsum('gcp,gh->gphc', cc, jnp.eye(g, dtype=cc.dtype)).reshape(g * p, g * c)


def _shifted(pd, shift0, b, t):
    pd3 = pd.reshape(b, t, pd.shape[-1])
    return jnp.concatenate([shift0[:, None, :], pd3[:, :-1]], axis=1).reshape(b * t, pd.shape[-1])


def _trunk(x, mod_all, row0, rows, per_token_mod, sample, st, p):
    b, t, d = x.shape
    n = b * t
    xt = x.reshape(n, d)
    tm = min(TOKEN_TILE, n)
    outs = {}
    for l in range(DEPTH):
        mod = mod_all[l, row0:row0 + rows]
        if per_token_mod:
            mod_arr = jnp.repeat(mod, t, axis=0).reshape(n // tm, tm, 9 * d)
        else:
            mod_arr = mod.reshape(rows * 9, 1, d)

        def mods(i, mod_arr=mod_arr):
            def one(c):
                if per_token_mod:
                    return _Mod(mod_arr, pl.BlockSpec((1, tm, d), lambda ti: (ti, 0, c)))
                return _Mod(mod_arr, pl.BlockSpec((1, 1, d), lambda ti: ((ti // (t // tm)) * 9 + c, 0, 0)))
            return [one(3 * i + j) for j in range(3)], t // tm

        m0, tpg = mods(0)
        xt = _ffn(xt, m0, tpg, p['wg'][l][0], p['wu'][l][0], p['wd'][l][0], p['ln_g'][l, 0], p['ln_b'][l, 0])
        m1, _ = mods(1)
        if l % 2 == 0:
            e = l // 2
            lam_init = 0.8 - 0.6 * math.exp(-0.3 * l)
            widths = (512, 512, 512, S5_CH)
            dl, subln = p['diff_lambda'][e], p['diff_subln'][e]
            if sample:
                q, k, v, u = _inproj(xt, m1, tpg, p['even_w_in'][e], widths)
            else:
                q, k, v, u, vt, kt = _inproj(xt, m1, tpg, p['even_w_in'][e], widths,
                                             t_weights=((p['even_wv_t'][e], True), (p['even_wk_t'][e], False)),
                                             t_block=min(ATTN_K_BLOCK, t), seq_tiles=t // tm)
            if sample:
                att8 = _diff_sample(_pad_tokens(q, b, t), _pad_tokens(k, b, t), _pad_tokens(v, b, t),
                                    st['pool_k'][e], st['pool_v'][e], st['page_table'], dl, subln, lam_init)
                att = att8[:, :t].reshape(n, -1)
                u_tm = u.reshape(b, t, -1).transpose(1, 0, 2)
                y_tm, hr, hi = _s5_sample(u_tm, st['s5_re'][e].reshape(b, S5_N), st['s5_im'][e].reshape(b, S5_N),
                                          p['s5'][e])
                y5 = y_tm.transpose(1, 0, 2).reshape(n, -1)
            else:
                att = _diff_prompt(q, k, vt, dl, subln, b, t, lam_init)
                zero = jnp.zeros((b, 1, S5_N), F32)
                y5, hr, hi = _s5_prompt(u, b, t, zero, zero, p['s5'][e])
            if sample:
                k_out = k.reshape(b, t, H_A, 2, DH_A)
            else:
                k_out = kt.reshape(b, H_A, 2, DH_A, t).transpose(0, 4, 1, 2, 3)
            outs.setdefault('ak', []).append(k_out)
            outs.setdefault('av', []).append(v.reshape(b, t, H_A, 2 * DH_A))
            outs.setdefault('s5r', []).append(hr.reshape(b, S5_G, S5_P))
            outs.setdefault('s5i', []).append(hi.reshape(b, S5_G, S5_P))
            xt = _outproj(xt, m1[2], tpg, [att], y5, p['even_w_out'][e], p['ln_g'][l, 1], p['ln_b'][l, 1])
        else:
            o = l // 2
            widths = (512, 512, 512, pd_cols(p))
            rp = p['rwkv'][o]
            dil_of = None
            if sample:
                q, k, v, pd = _inproj(xt, m1, tpg, p['odd_w_in'][o], widths)
            else:
                dils = C_DILATIONS[1:]
                res = _inproj(xt, m1, tpg, p['odd_w_in'][o], widths,
                              t_weights=((p['odd_wk_t'][o], False), (p['odd_wv_t'][o], False)),
                              seq_tiles=t // tm, n_dil=3, dils=dils, dil_dtype=BF16)
                q, k, v, pd = res[:4]
                split = {dd: res[4 + 3 * i:7 + 3 * i] for i, dd in enumerate(dils)}
                kt, vt = res[4 + 3 * len(dils):]
            if sample:
                att8 = _dil_sample(_pad_tokens(q, b, t), _pad_tokens(k, b, t), _pad_tokens(v, b, t),
                                   st['cache_c_k'][o], st['cache_c_v'][o])
                atts = [att8[:, :t].reshape(n, -1)]
                prev = _shifted(pd, st['d_shift'][o], b, t)
                r, ld, k2, vv, kk, bb, g, bonus = _rwkv_pre(pd, prev, rp['pre'])
                tl = lambda z: z.reshape(b, t, H_D, N_D).transpose(1, 2, 3, 0)
                s0 = st['d_wkv'][o].transpose(1, 2, 3, 0)
                y_l, s_new = _rwkv_lane(tl(r), tl(ld), tl(k2), tl(vv), tl(kk), tl(bb), s0)
                y = y_l.transpose(3, 0, 1, 2).reshape(n, D_W)
                s_new = s_new.transpose(3, 0, 1, 2)
                k_keep, v_keep = k.reshape(b, t, H_C, DH_C), v.reshape(b, t, H_C, DH_C)
            else:
                atts_o, atts_l = [], []
                for dil in C_DILATIONS:
                    if dil == 1:
                        ob, lb = _dil_prompt(q.reshape(b, t, -1), k.reshape(b, t, -1), v.reshape(b, t, -1))
                        ob, lb = ob.reshape(n, -1), lb.reshape(n, -1)
                    else:
                        ob, lb = _dil_prompt(*[z.reshape(b * dil, t // dil, -1) for z in split[dil]])
                        ob, lb = ob.reshape(b, dil, t // dil, -1), lb.reshape(b, dil, t // dil, -1)
                    atts_o.append(ob)
                    atts_l.append(lb)
                atts = atts_o + atts_l
                dil_of = C_DILATIONS + C_DILATIONS
                r, ld, k2, vv, kk, bb, g, bonus = _rwkv_pre(pd, None, rp['pre'], seq_tiles=t // tm)
                y, s_pairs = _rwkv_chunk(r, ld, k2, vv, kk, bb, b, t)
                sp = s_pairs.reshape(b, H_D // 2, 2, N_D, 2, N_D)
                s_new = jnp.stack([sp[:, :, 0, :, 0, :], sp[:, :, 1, :, 1, :]], axis=2)
                s_new = s_new.reshape(b, H_D, N_D, N_D).transpose(0, 1, 3, 2)
                keep = min(C_BLK * C_DILATIONS[-1], t)
                keep_t = lambda zt: zt[:, :, t - keep:].reshape(b, H_C, DH_C, keep).transpose(0, 3, 1, 2)
                k_keep, v_keep = keep_t(kt), keep_t(vt)
            outs.setdefault('ck', []).append(k_keep)
            outs.setdefault('cv', []).append(v_keep)
            outs.setdefault('dw', []).append(s_new)
            outs.setdefault('ds', []).append(pd.reshape(b, t, -1)[:, -1])
            xt = _outproj(xt, m1[2], tpg, atts, y, p['odd_w_out'][o], p['ln_g'][l, 1], p['ln_b'][l, 1],
                          dil_of=dil_of, seq_tiles=t // tm, post=(bonus, g, rp['gn_w'], rp['gn_b'], rp['seg']))
        m2, _ = mods(2)
        xt = _ffn(xt, m2, tpg, p['wg'][l][1], p['wu'][l][1], p['wd'][l][1], p['ln_g'][l, 2], p['ln_b'][l, 2])
    stacked = [jnp.stack(outs[key], 0) for key in ('ak', 'av', 's5r', 's5i', 'ck', 'cv', 'dw', 'ds')]
    return xt.reshape(b, t, d), stacked


def pd_cols(p):
    return p['odd_w_in'].shape[-1] - 3 * H_C * DH_C


def kernel(x_prompt, x_sample, cache_a_k, cache_a_v, state_s5_re, state_s5_im, cache_c_k, cache_c_v, state_d_wkv, state_d_shift, page_table, c_prompt, c_sample, ada_w, ada_b, ln_g, ln_b, ffn_w_gate, ffn_w_up, ffn_w_down, even_w_in, even_w_out, diff_lambda, diff_subln, s5_a_re, s5_a_im, s5_log_dt, s5_b_re, s5_b_im, s5_c_re, s5_c_im, s5_d, s5_glu_w, s5_glu_b, odd_w_in, odd_w_out, rwkv_mu, rwkv_w0, rwkv_w2, rwkv_a0, rwkv_a2, rwkv_g2, rwkv_k_k, rwkv_k_a, rwkv_r_k, rwkv_gn_w, rwkv_gn_b):
    bp, bs = x_prompt.shape[0], x_sample.shape[0]
    n_even, n_odd = even_w_in.shape[0], odd_w_in.shape[0]
    bf = lambda w: w.astype(BF16)

    seg = jnp.kron(jnp.eye(H_D, dtype=F32), jnp.ones((N_D, N_D), F32)).astype(BF16)
    s5 = []
    for e in range(n_even):
        lr, li, bbr, bbi = _s5_prep(s5_a_re[e], s5_a_im[e], s5_log_dt[e], s5_b_re[e], s5_b_im[e])
        s5.append((bf(_block_diag_in(bbr)), bf(_block_diag_in(bbi)), lr.reshape(1, S5_N), li.reshape(1, S5_N),
                   bf(_block_diag_out(s5_c_re[e])), bf(_block_diag_out(s5_c_im[e])),
                   s5_d[e].reshape(1, S5_CH), bf(s5_glu_w[e]), s5_glu_b[e].reshape(1, S5_CH)))
    rwkv = []
    for o in range(n_odd):
        row = lambda z: z.reshape(1, -1)
        w2p = jnp.concatenate([rwkv_w2[o], jnp.zeros_like(rwkv_a2[o])], axis=0)
        a2p = jnp.concatenate([jnp.zeros_like(rwkv_w2[o]), rwkv_a2[o]], axis=0)
        pre = (row(rwkv_mu[o]), row(rwkv_w0[o]), bf(w2p), row(rwkv_a0[o]), bf(a2p), bf(rwkv_g2[o]),
               row(rwkv_k_k[o]), row(rwkv_k_a[o]), row(rwkv_r_k[o]), seg)
        rwkv.append(dict(pre=pre, gn_w=rwkv_gn_w[o], gn_b=rwkv_gn_b[o], seg=seg))
    p = dict(wg=bf(ffn_w_gate), wu=bf(ffn_w_up), wd=bf(ffn_w_down), ln_g=ln_g, ln_b=ln_b,
             even_w_in=bf(even_w_in), even_w_out=bf(even_w_out), odd_w_in=bf(odd_w_in), odd_w_out=bf(odd_w_out),
             even_wv_t=bf(jnp.swapaxes(even_w_in[:, :, 2 * A_W:3 * A_W], 1, 2)),
             even_wk_t=bf(jnp.swapaxes(even_w_in[:, :, A_W:2 * A_W], 1, 2)),
             odd_wk_t=bf(jnp.swapaxes(odd_w_in[:, :, C_W:2 * C_W], 1, 2)),
             odd_wv_t=bf(jnp.swapaxes(odd_w_in[:, :, 2 * C_W:3 * C_W], 1, 2)),
             diff_lambda=diff_lambda, diff_subln=diff_subln, s5=s5, rwkv=rwkv)

    mod_all = _ada(jnp.concatenate([c_prompt, c_sample], axis=0), ada_w, ada_b)

    y_prompt, st_p = _trunk(x_prompt, mod_all, 0, bp, False, False, None, p)
    n_pool, page = cache_a_k.shape[1], cache_a_k.shape[2]
    win_buf = cache_c_k.shape[2]
    pos_minor = lambda c: jnp.transpose(c, (0, 1, 3, 4, 2)).reshape(n_odd, bs, -1, win_buf)
    st = dict(pool_k=jnp.transpose(cache_a_k, (0, 1, 3, 4, 5, 2)).reshape(n_even, n_pool, -1, page),
              pool_v=cache_a_v,
              page_table=page_table, s5_re=state_s5_re, s5_im=state_s5_im,
              cache_c_k=pos_minor(cache_c_k), cache_c_v=pos_minor(cache_c_v),
              d_wkv=state_d_wkv, d_shift=state_d_shift)
    y_sample, st_s = _trunk(x_sample, mod_all, bp, bs, True, True, st, p)
    a_k_p, a_v_p, s5_re_p, s5_im_p, c_k_p, c_v_p, d_wkv_p, d_shift_p = st_p
    a_k_s, a_v_s, s5_re_s, s5_im_s, c_k_s, c_v_s, d_wkv_s, d_shift_s = st_s
    return (y_prompt, y_sample, a_k_p, a_k_s, a_v_p, a_v_s, s5_re_p, s5_re_s, s5_im_p, s5_im_s,
            c_k_p, c_k_s, c_v_p, c_v_s, d_wkv_p, d_wkv_s, d_shift_p, d_shift_s)
```
